```python
import math
import jax, jax.numpy as jnp
from jax import lax
import numpy as np

D_MODEL = 4096
BATCH = 4
SEQ = 4096
DEPTH = 1

HEAD_DIM = 64
D_MIX = D_MODEL
D_RWKV = D_MIX // 2
D_ATTN = D_MIX - D_RWKV
N_RWKV_HEADS = D_RWKV // HEAD_DIM
N_Q_HEADS = D_ATTN // HEAD_DIM
N_KV_HEADS = 8
GQA_GROUP = N_Q_HEADS // N_KV_HEADS
D_KV = N_KV_HEADS * HEAD_DIM
WINDOW = 128
BLOCK = 128


def _lora_dim(c, factor, power):
    return max(32, int(round(factor * c ** power / 32)) * 32)


D_DECAY_LORA = _lora_dim(D_RWKV, 1.8, 0.5)
D_AAA_LORA = _lora_dim(D_RWKV, 1.8, 0.5)
D_GATE_LORA = _lora_dim(D_RWKV, 0.6, 0.8)
N_RWKV_COLS = 3 * D_RWKV + D_DECAY_LORA + D_AAA_LORA + D_GATE_LORA
RWKV_SPLITS = [D_RWKV, 2 * D_RWKV, 3 * D_RWKV, 3 * D_RWKV + D_DECAY_LORA,
               3 * D_RWKV + D_DECAY_LORA + D_AAA_LORA]
N_ATTN_COLS = D_ATTN + 2 * D_KV
N_IN_COLS = N_RWKV_COLS + N_ATTN_COLS
RWKV_GN_EPS = 64e-5
NORM_EPS = 1e-6

N_GROUPS = 8
EXPERTS_PER_GROUP = 8
N_EXPERTS = N_GROUPS * EXPERTS_PER_GROUP
TOP_K_IN_GROUP = 2
D_EXPERT = D_MODEL // 8
MOE_BLOCK = 128

kernel_name = 'hymba_rwkv7_swa_sink_alibi_hmoe_adaln_block'


def rmsnorm(x, g):
    xf = x.astype(jnp.float32)
    y = xf * lax.rsqrt(jnp.mean(xf * xf, axis=-1, keepdims=True) + NORM_EPS)
    return (y * g).astype(x.dtype)


def modulate(u, shift, scale):
    return u * (1.0 + scale[:, None, :]) + shift[:, None, :]


def token_shift(p):
    return jnp.pad(p, ((0, 0), (1, 0), (0, 0)))[:, :-1]


def rwkv7_time_mix(p, mu, w0, w_up, a0, a_up, g_up, k_k, k_a, r_k, lnx_w, lnx_b):
    B, S, _ = p.shape
    H, N = N_RWKV_HEADS, HEAD_DIM
    f32 = jnp.float32
    p = p + (token_shift(p) - p) * mu
    r, k, v, xw, xa, xg = jnp.split(p, RWKV_SPLITS, axis=-1)
    w = -jax.nn.softplus(-(w0 + jnp.tanh(xw) @ w_up)) - 0.5
    decay = jnp.exp(-jnp.exp(w.astype(f32)))
    a = jax.nn.sigmoid(a0 + xa @ a_up)
    g = jax.nn.sigmoid(xg) @ g_up

    def heads(t):
        return t.astype(f32).reshape(B, S, H, N)

    kk = heads(k * k_k)
    kk = kk / jnp.maximum(jnp.sqrt(jnp.sum(kk * kk, axis=-1, keepdims=True)), 1e-12)
    k = k * (1.0 + (a - 1.0) * k_a)
    r_h, k_h, v_h, a_h, w_h = heads(r), heads(k), heads(v), heads(a), heads(decay)

    def step(state, inp):
        r_t, w_t, k_t, v_t, kk_t, b_t = inp
        sa = jnp.einsum('bhij,bhj->bhi', state, -kk_t)
        state = (state * w_t[:, :, None, :] + sa[..., :, None] * b_t[..., None, :]
                 + v_t[..., :, None] * k_t[..., None, :])
        y_t = jnp.einsum('bhij,bhj->bhi', state, r_t)
        return state, y_t

    def tm(t):
        return jnp.moveaxis(t, 1, 0)

    state0 = jnp.zeros((B, H, N, N), f32)
    _, y = lax.scan(step, state0, (tm(r_h), tm(w_h), tm(k_h), tm(v_h), tm(kk), tm(kk * a_h)))
    y = jnp.moveaxis(y, 0, 1)
    mean = jnp.mean(y, axis=-1, keepdims=True)
    var = jnp.mean(jnp.square(y - mean), axis=-1, keepdims=True)
    y = ((y - mean) * lax.rsqrt(var + RWKV_GN_EPS)).reshape(B, S, D_RWKV) * lnx_w + lnx_b
    bonus = (jnp.sum(r_h * k_h * r_k, axis=-1, keepdims=True) * v_h).reshape(B, S, D_RWKV)
    return ((y + bonus) * g).astype(p.dtype)


def sliding_window_attention(q, k, v, sinks, out_g):
    B, S, _ = q.shape
    nb = S // BLOCK
    f32 = jnp.float32
    qb = q.reshape(B, nb, BLOCK, N_KV_HEADS, GQA_GROUP, HEAD_DIM)

    def band(t):
        t = t.reshape(B, S, N_KV_HEADS, HEAD_DIM)
        tp = jnp.pad(t, ((0, 0), (BLOCK, 0), (0, 0), (0, 0))).reshape(B, nb + 1, BLOCK, N_KV_HEADS, HEAD_DIM)
        return jnp.concatenate([tp[:, :-1], tp[:, 1:]], axis=2)

    kb, vb = band(k), band(v)
    qi = jnp.arange(BLOCK)[:, None]
    kj = jnp.arange(2 * BLOCK)[None, :]
    dist = qi + BLOCK - kj
    in_window = (dist >= 0) & (dist < WINDOW)
    slopes = jnp.exp2(-8.0 * jnp.arange(1, N_Q_HEADS + 1, dtype=f32) / N_Q_HEADS)
    slopes = slopes.reshape(N_KV_HEADS, GQA_GROUP)
    alibi = -slopes[:, :, None, None] * dist.astype(f32)
    sink = sinks.astype(f32).reshape(N_KV_HEADS, GQA_GROUP)[None, :, :, None]
    scale = HEAD_DIM ** -0.5

    def one_block(args):
        n, qn, kn, vn = args
        s = jnp.einsum('bqhgd,bkhd->bhgqk', qn, kn, preferred_element_type=f32) * scale + alibi
        valid = in_window & (n * BLOCK - BLOCK + kj >= 0)
        s = jnp.where(valid, s, -jnp.inf)
        m = jnp.maximum(jnp.max(s, axis=-1), sink)
        pr = jnp.exp(s - m[..., None])
        denom = jnp.sum(pr, axis=-1) + jnp.exp(sink - m)
        pr = pr / denom[..., None]
        return jnp.einsum('bhgqk,bkhd->bqhgd', pr.astype(vn.dtype), vn)

    o = lax.map(one_block, (jnp.arange(nb), jnp.moveaxis(qb, 1, 0),
                            jnp.moveaxis(kb, 1, 0), jnp.moveaxis(vb, 1, 0)))
    o = jnp.moveaxis(o, 0, 1).reshape(B, S, N_Q_HEADS, HEAD_DIM).astype(f32)
    o = o * lax.rsqrt(jnp.mean(o * o, axis=-1, keepdims=True) + NORM_EPS)
    return (o.reshape(B, S, D_ATTN) * out_g).astype(q.dtype)


def hierarchical_moe(u, router_group, router_group_bias, router_expert, router_expert_bias,
                     we_gate, we_up, we_down):
    B, S, D = u.shape
    T = B * S
    f32 = jnp.float32
    xf = u.reshape(T, D)
    pg = jax.nn.softmax((xf @ router_group).astype(f32) + router_group_bias, axis=-1)
    g_idx = jnp.argmax(pg, axis=-1)
    g_w = jnp.take_along_axis(pg, g_idx[:, None], axis=-1)[:, 0]
    le = ((xf @ router_expert).astype(f32) + router_expert_bias).reshape(T, N_GROUPS, EXPERTS_PER_GROUP)
    le = jnp.take_along_axis(le, g_idx[:, None, None], axis=1)[:, 0]
    pe = jax.nn.softmax(le, axis=-1)
    top_w, top_i = lax.top_k(pe, TOP_K_IN_GROUP)
    top_w = top_w / jnp.sum(top_w, axis=-1, keepdims=True)
    expert_id = g_idx[:, None] * EXPERTS_PER_GROUP + top_i
    weight = g_w[:, None] * top_w

    M = T * TOP_K_IN_GROUP
    e_flat = expert_id.reshape(M).astype(jnp.int32)
    w_flat = weight.reshape(M)
    tok_flat = jnp.repeat(jnp.arange(T, dtype=jnp.int32), TOP_K_IN_GROUP)
    order = jnp.argsort(e_flat)
    e_sorted = e_flat[order]
    counts = jnp.bincount(e_flat, length=N_EXPERTS)
    starts = jnp.cumsum(counts) - counts
    padded = ((counts + MOE_BLOCK - 1) // MOE_BLOCK) * MOE_BLOCK
    pends = jnp.cumsum(padded)
    pstarts = pends - padded
    dest = pstarts[e_sorted] + (jnp.arange(M) - starts[e_sorted])
    NB = -(-M // MOE_BLOCK) + N_EXPERTS
    P = NB * MOE_BLOCK
    row_tok = jnp.full((P,), T, jnp.int32).at[dest].set(tok_flat[order])
    row_w = jnp.zeros((P,), f32).at[dest].set(w_flat[order])
    block_expert = jnp.minimum(jnp.searchsorted(pends, jnp.arange(NB) * MOE_BLOCK, side='right'),
                               N_EXPERTS - 1)
    x_pad = jnp.concatenate([xf, jnp.zeros((1, D), xf.dtype)], axis=0)

    def body(acc, inp):
        tok, w, e = inp
        xb = x_pad[tok]
        h = jax.nn.silu(xb @ we_gate[e]) * (xb @ we_up[e])
        yb = ((h @ we_down[e]) * w[:, None]).astype(acc.dtype)
        return acc.at[tok].add(yb), None

    acc0 = jnp.zeros((T + 1, D), u.dtype)
    acc, _ = lax.scan(body, acc0, (row_tok.reshape(NB, MOE_BLOCK), row_w.reshape(NB, MOE_BLOCK), block_expert))
    return acc[:T].reshape(B, S, D)


def setup_inputs(seed: int = 0) -> dict:
    key = jax.random.key(seed)
    ks = jax.random.split(key, 32)
    L, D = DEPTH, D_MODEL
    f32 = jnp.float32

    def nrm(k, shape, scale):
        return jax.random.normal(k, shape, f32) * scale

    chan = jnp.arange(D_RWKV, dtype=f32) / (D_RWKV - 1)
    w0 = -6.0 + 5.0 * chan ** 0.7
    return {
        'x': nrm(ks[0], (BATCH, SEQ, D), 1.0),
        'c': nrm(ks[1], (BATCH, D), 1.0),
        'w_cond': nrm(ks[2], (L, D, 6 * D), D ** -0.5),
        'b_cond': nrm(ks[3], (L, 6 * D), 0.01),
        'norm1_g': 1.0 + nrm(ks[4], (L, D), 0.01),
        'w_in': nrm(ks[5], (L, D, N_IN_COLS), D ** -0.5),
        'rwkv_mu': jax.random.uniform(ks[6], (L, N_RWKV_COLS), f32),
        'rwkv_w0': w0[None, :] + nrm(ks[7], (L, D_RWKV), 0.1),
        'rwkv_w_up': nrm(ks[8], (L, D_DECAY_LORA, D_RWKV), 0.1),
        'rwkv_a0': nrm(ks[9], (L, D_RWKV), 0.1),
        'rwkv_a_up': nrm(ks[10], (L, D_AAA_LORA, D_RWKV), D_AAA_LORA ** -0.5),
        'rwkv_g_up': nrm(ks[11], (L, D_GATE_LORA, D_RWKV), D_GATE_LORA ** -0.5),
        'rwkv_k_k': 0.85 + nrm(ks[12], (L, D_RWKV), 0.05),
        'rwkv_k_a': 1.0 + nrm(ks[13], (L, D_RWKV), 0.05),
        'rwkv_r_k': nrm(ks[14], (L, N_RWKV_HEADS, HEAD_DIM), 0.1),
        'rwkv_lnx_w': 1.0 + nrm(ks[15], (L, D_RWKV), 0.01),
        'rwkv_lnx_b': nrm(ks[16], (L, D_RWKV), 0.01),
        'attn_sinks': nrm(ks[17], (L, N_Q_HEADS), 0.5),
        'attn_out_g': 1.0 + nrm(ks[18], (L, D_ATTN), 0.01),
        'w_out': nrm(ks[19], (L, D_MIX, D), D_MIX ** -0.5),
        'norm2_g': 1.0 + nrm(ks[20], (L, D), 0.01),
        'router_group': nrm(ks[21], (L, D, N_GROUPS), D ** -0.5),
        'router_group_bias': nrm(ks[22], (L, N_GROUPS), 0.01),
        'router_expert': nrm(ks[23], (L, D, N_EXPERTS), D ** -0.5),
        'router_expert_bias': nrm(ks[24], (L, N_EXPERTS), 0.01),
        'expert_w_gate': nrm(ks[25], (L, N_EXPERTS, D, D_EXPERT), D ** -0.5),
        'expert_w_up': nrm(ks[26], (L, N_EXPERTS, D, D_EXPERT), D ** -0.5),
        'expert_w_down': nrm(ks[27], (L, N_EXPERTS, D_EXPERT, D), D_EXPERT ** -0.5),
        'norm_f_g': 1.0 + nrm(ks[28], (D,), 0.01),
    }


def reference(x, c, w_cond, b_cond, norm1_g, w_in, rwkv_mu, rwkv_w0, rwkv_w_up, rwkv_a0,
              rwkv_a_up, rwkv_g_up, rwkv_k_k, rwkv_k_a, rwkv_r_k, rwkv_lnx_w, rwkv_lnx_b,
              attn_sinks, attn_out_g, w_out, norm2_g, router_group, router_group_bias,
              router_expert, router_expert_bias, expert_w_gate, expert_w_up, expert_w_down,
              norm_f_g):
    for l in range(DEPTH):
        mod = jax.nn.silu(c) @ w_cond[l] + b_cond[l]
        sh1, sc1, g1, sh2, sc2, g2 = jnp.split(mod, 6, axis=-1)
        u = modulate(rmsnorm(x, norm1_g[l]), sh1, sc1)
        proj = u @ w_in[l]
        p_rwkv = proj[..., :N_RWKV_COLS]
        q = proj[..., N_RWKV_COLS:N_RWKV_COLS + D_ATTN]
        k = proj[..., N_RWKV_COLS + D_ATTN:N_RWKV_COLS + D_ATTN + D_KV]
        v = proj[..., N_RWKV_COLS + D_ATTN + D_KV:]
        y_rwkv = rwkv7_time_mix(p_rwkv, rwkv_mu[l], rwkv_w0[l], rwkv_w_up[l], rwkv_a0[l],
                                rwkv_a_up[l], rwkv_g_up[l], rwkv_k_k[l], rwkv_k_a[l],
                                rwkv_r_k[l], rwkv_lnx_w[l], rwkv_lnx_b[l])
        y_attn = sliding_window_attention(q, k, v, attn_sinks[l], attn_out_g[l])
        y = jnp.concatenate([y_rwkv, y_attn], axis=-1) @ w_out[l]
        x = x + g1[:, None, :] * y
        u = modulate(rmsnorm(x, norm2_g[l]), sh2, sc2)
        y = hierarchical_moe(u, router_group[l], router_group_bias[l], router_expert[l],
                             router_expert_bias[l], expert_w_gate[l], expert_w_up[l], expert_w_down[l])
        x = x + g2[:, None, :] * y
    return rmsnorm(x, norm_f_g)
```

```python
import functools
import math

import jax
import jax.numpy as jnp
from jax import lax
from jax.experimental import pallas as pl
from jax.experimental.pallas import tpu as pltpu

F32 = jnp.float32
BF16 = jnp.bfloat16

HEAD_DIM = 64
N_KV_HEADS = 8
WINDOW = 128
N_GROUPS = 8
EXPERTS_PER_GROUP = 8
N_EXPERTS = N_GROUPS * EXPERTS_PER_GROUP
RWKV_GN_EPS = 64e-5
NORM_EPS = 1e-6
RWKV_CHUNK = 64
RWKV_HEADS_PER_STEP = 4
MOE_ROWS = 256
LANES = 128
MIB = 1024 * 1024


def _params(sem, vmem_mib=48):
    return pltpu.CompilerParams(dimension_semantics=sem, vmem_limit_bytes=vmem_mib * MIB)


def _round_up(n, m):
    return -(-n // m) * m


def _pick_tile(n, candidates):
    for t in candidates:
        if n % t == 0:
            return t
    raise ValueError(f"no tile in {candidates} divides {n}")


def _cond_kernel(c_ref, w_ref, b_ref, o_ref):
    c = c_ref[...]
    sc = (c * jax.nn.sigmoid(c)).astype(BF16)
    o_ref[...] = jnp.dot(sc, w_ref[...].astype(BF16), preferred_element_type=F32) + b_ref[...]


def _cond(c_pad, w, b):
    rows, d = c_pad.shape
    n = w.shape[1]
    tn = _pick_tile(n, (1024, 512, 256, 128))
    return pl.pallas_call(
        _cond_kernel,
        grid=(n // tn,),
        in_specs=[pl.BlockSpec((rows, d), lambda j: (0, 0)),
                  pl.BlockSpec((d, tn), lambda j: (0, j)),
                  pl.BlockSpec((1, tn), lambda j: (0, j))],
        out_specs=pl.BlockSpec((rows, tn), lambda j: (0, j)),
        out_shape=jax.ShapeDtypeStruct((rows, n), F32),
        compiler_params=_params(("arbitrary",)),
    )(c_pad, w, b)


def _norm_mod_kernel(x_ref, g_ref, sc_ref, sh_ref, o_ref):
    x = x_ref[...]
    y = x * lax.rsqrt(jnp.mean(x * x, axis=-1, keepdims=True) + NORM_EPS)
    y = y * g_ref[...]
    o_ref[...] = (y * (1.0 + sc_ref[0]) + sh_ref[0]).astype(o_ref.dtype)


def _norm_mod(x2d, g, scale, shift, seq, out_dtype):
    t, d = x2d.shape
    tm = _pick_tile(seq, (512, 256, 128))
    per_b = seq // tm
    return pl.pallas_call(
        _norm_mod_kernel,
        grid=(t // tm,),
        in_specs=[pl.BlockSpec((tm, d), lambda i: (i, 0)),
                  pl.BlockSpec((1, d), lambda i: (0, 0)),
                  pl.BlockSpec((1, 1, d), lambda i: (i // per_b, 0, 0)),
                  pl.BlockSpec((1, 1, d), lambda i: (i // per_b, 0, 0))],
        out_specs=pl.BlockSpec((tm, d), lambda i: (i, 0)),
        out_shape=jax.ShapeDtypeStruct((t, d), out_dtype),
        compiler_params=_params(("arbitrary",)),
    )(x2d, g.reshape(1, d), scale[:, None, :], shift[:, None, :])


def _mm_kernel(a_ref, b_ref, o_ref):
    o_ref[...] = jnp.dot(a_ref[...], b_ref[...], preferred_element_type=F32).astype(o_ref.dtype)


def _matmul(a, b, out_dtype):
    m, k = a.shape
    n = b.shape[1]
    tm = _pick_tile(m, (1024, 512, 256, 128))
    tn = _pick_tile(n, (512, 256, 128))
    return pl.pallas_call(
        _mm_kernel,
        grid=(m // tm, n // tn),
        in_specs=[pl.BlockSpec((tm, k), lambda i, j: (i, 0)),
                  pl.BlockSpec((k, tn), lambda i, j: (0, j))],
        out_specs=pl.BlockSpec((tm, tn), lambda i, j: (i, j)),
        out_shape=jax.ShapeDtypeStruct((m, n), out_dtype),
        compiler_params=_params(("arbitrary", "arbitrary")),
    )(a, b)


def _softplus(z):
    return jnp.maximum(z, 0.0) + jnp.log(1.0 + jnp.exp(-jnp.abs(z)))


def _rwkv_kernel(r_ref, k_ref, v_ref, l_ref, mur_ref, muk_ref, muv_ref, mul_ref,
                 w0_ref, a0_ref, kk_ref, ka_ref, rk_ref, lw_ref, lb_ref,
                 ww_ref, wa_ref, wg_ref, o_ref,
                 s_ref, pr_ref, pk_ref, pv_ref, pl_ref, *, n_batch, dw, da):
    c = RWKV_CHUNK
    hw = r_ref.shape[-1]
    nh = hw // HEAD_DIM
    lp = l_ref.shape[-1]
    chunk = pl.program_id(1)

    @pl.when(chunk == 0)
    def _():
        s_ref[...] = jnp.zeros_like(s_ref)
        pr_ref[...] = jnp.zeros_like(pr_ref)
        pk_ref[...] = jnp.zeros_like(pk_ref)
        pv_ref[...] = jnp.zeros_like(pv_ref)
        pl_ref[...] = jnp.zeros_like(pl_ref)

    row_w = lax.broadcasted_iota(jnp.int32, (c, hw), 0)
    col_w = lax.broadcasted_iota(jnp.int32, (c, hw), 1)
    s_idx = col_w % c
    strict = row_w > s_idx
    incl = row_w >= s_idx
    eye_w = (row_w == s_idx).astype(F32)
    merge_masks = [strict & (row_w // 2 == s_idx // 2)]
    size = 2
    while size < c:
        merge_masks.append(strict & (row_w // (2 * size) == s_idx // (2 * size))
                           & (row_w // size != s_idx // size))
        size *= 2
    bd_row = lax.broadcasted_iota(jnp.int32, (hw, hw), 0) // HEAD_DIM
    bd_col = lax.broadcasted_iota(jnp.int32, (hw, hw), 1) // HEAD_DIM
    bd_mask = bd_row == bd_col
    bd_ones = bd_mask.astype(BF16)
    tri = (lax.broadcasted_iota(jnp.int32, (c, c), 0)
           >= lax.broadcasted_iota(jnp.int32, (c, c), 1)).astype(BF16)
    row_l = lax.broadcasted_iota(jnp.int32, (c, lp), 0)
    col_l = lax.broadcasted_iota(jnp.int32, (c, lp), 1)

    def shift_mix(x, prev, mu, row):
        shifted = jnp.where(row == 0, prev, pltpu.roll(x, 1, axis=0))
        return x + (shifted - x) * mu

    def split2(x):
        hi = x.astype(BF16)
        lo = (x - hi.astype(F32)).astype(BF16)
        return hi, lo

    def seg_sum(x):
        hi, lo = split2(x)
        both = jnp.dot(jnp.concatenate([hi, lo], axis=0), bd_ones, preferred_element_type=F32)
        return both[:c] + both[c:]

    def bd(x):
        return jnp.where(bd_mask, jnp.tile(x, (nh, 1)), 0.0).astype(BF16)

    def mm(a, b):
        return jnp.dot(a.astype(BF16), b, preferred_element_type=F32)

    def mm_nt(a, b):
        return lax.dot_general(a.astype(BF16), b, (((1,), (1,)), ((), ())),
                               preferred_element_type=F32)

    for bi in range(n_batch):
        raw_r, raw_k, raw_v, raw_l = r_ref[bi], k_ref[bi], v_ref[bi], l_ref[bi]
        xr = shift_mix(raw_r, pr_ref[bi], mur_ref[...], row_w)
        xk = shift_mix(raw_k, pk_ref[bi], muk_ref[...], row_w)
        xv = shift_mix(raw_v, pv_ref[bi], muv_ref[...], row_w)
        xl = shift_mix(raw_l, pl_ref[bi], mul_ref[...], row_l)
        pr_ref[bi] = raw_r[c - 1:c]
        pk_ref[bi] = raw_k[c - 1:c]
        pv_ref[bi] = raw_v[c - 1:c]
        pl_ref[bi] = raw_l[c - 1:c]

        act = jnp.where(col_l < dw, jnp.tanh(xl),
                        jnp.where(col_l < dw + da, xl, jax.nn.sigmoid(xl))).astype(BF16)
        lora_w = jnp.dot(act, ww_ref[...], preferred_element_type=F32)
        lora_a = jnp.dot(act, wa_ref[...], preferred_element_type=F32)
        gate = jnp.dot(act, wg_ref[...], preferred_element_type=F32)

        w = -_softplus(-(w0_ref[...] + lora_w)) - 0.5
        e = jnp.exp(w)
        lr = jax.nn.sigmoid(a0_ref[...] + lora_a)
        kk = xk * kk_ref[...]
        kk = kk / jnp.maximum(jnp.sqrt(seg_sum(kk * kk)), 1e-12)
        kmod = xk * (1.0 + (lr - 1.0) * ka_ref[...])

        e_hi, e_lo = split2(e)
        cum2 = jnp.dot(tri, jnp.concatenate([e_hi, e_lo], axis=1), preferred_element_type=F32)
        cum = cum2[:, :hw] + cum2[:, hw:]
        dec = jnp.exp(-cum)
        inv = jnp.exp(cum)
        at = -kk * jnp.exp(e - cum)
        rt = xr * dec
        bt = kk * lr * inv
        kt = kmod * inv

        p_lhs = jnp.concatenate([at, rt], axis=0)
        q_rhs = jnp.concatenate([bd(bt), bd(kt)], axis=0)
        g = mm_nt(p_lhs, q_rhs)
        m_ab = jnp.where(strict, g[:c, :hw], 0.0)
        m_ak = jnp.where(strict, g[:c, hw:], 0.0)
        m_rb = jnp.where(incl, g[c:, :hw], 0.0)
        m_rk = jnp.where(incl, g[c:, hw:], 0.0)

        t_inv = eye_w + jnp.where(merge_masks[0], m_ab, 0.0)
        for mask in merge_masks[1:]:
            z = mm(jnp.where(mask, m_ab, 0.0), bd(t_inv))
            t_inv = t_inv + mm(t_inv, bd(z))

        s_bf = s_ref[bi].astype(BF16)
        bd_v = bd(xv)
        x_rhs = mm_nt(at, s_bf) + mm(m_ak, bd_v)
        u = mm(t_inv, bd(x_rhs))
        y = mm_nt(rt, s_bf) + mm(m_rb, bd(u)) + mm(m_rk, bd_v)

        upd = lax.dot_general(jnp.concatenate([u, xv], axis=0).astype(BF16),
                              jnp.concatenate([bt, kt], axis=0).astype(BF16),
                              (((0,), (0,)), ((), ())), preferred_element_type=F32)
        s_ref[bi] = (s_ref[bi] + jnp.where(bd_mask, upd, 0.0)) * dec[c - 1:c]

        mean = seg_sum(y) * (1.0 / HEAD_DIM)
        dy = y - mean
        var = seg_sum(dy * dy) * (1.0 / HEAD_DIM)
        yn = dy * lax.rsqrt(var + RWKV_GN_EPS) * lw_ref[...] + lb_ref[...]
        bonus = seg_sum(xr * kmod * rk_ref[...]) * xv
        o_ref[bi] = ((yn + bonus) * gate).astype(o_ref.dtype)


def _rwkv(proj, mu, w0, a0, k_k, k_a, r_k, lnx_w, lnx_b, ww, wa, wg, *, d_rwkv, lp, dw, da):
    b, s, _ = proj.shape
    c = RWKV_CHUNK
    hw = RWKV_HEADS_PER_STEP * HEAD_DIM
    assert c == HEAD_DIM and d_rwkv % hw == 0 and s % c == 0 and (3 * d_rwkv) % lp == 0
    nhg = d_rwkv // hw
    lblk = (3 * d_rwkv) // lp

    def act_spec(off):
        return pl.BlockSpec((b, c, hw), lambda h, t: (0, t, off + h))

    def vec_spec(off):
        return pl.BlockSpec((1, hw), lambda h, t: (0, off + h))

    vec = vec_spec(0)
    lora_spec = pl.BlockSpec((lp, hw), lambda h, t: (0, h))
    kern = functools.partial(_rwkv_kernel, n_batch=b, dw=dw, da=da)
    return pl.pallas_call(
        kern,
        grid=(nhg, s // c),
        in_specs=[act_spec(0), act_spec(nhg), act_spec(2 * nhg),
                  pl.BlockSpec((b, c, lp), lambda h, t: (0, t, lblk)),
                  vec_spec(0), vec_spec(nhg), vec_spec(2 * nhg),
                  pl.BlockSpec((1, lp), lambda h, t: (0, lblk)),
                  vec, vec, vec, vec, vec, vec, vec,
                  lora_spec, lora_spec, lora_spec],
        out_specs=pl.BlockSpec((b, c, hw), lambda h, t: (0, t, h)),
        out_shape=jax.ShapeDtypeStruct((b, s, d_rwkv), BF16),
        scratch_shapes=[pltpu.VMEM((b, hw, hw), F32),
                        pltpu.VMEM((b, 1, hw), F32), pltpu.VMEM((b, 1, hw), F32),
                        pltpu.VMEM((b, 1, hw), F32), pltpu.VMEM((b, 1, lp), F32)],
        compiler_params=_params(("arbitrary", "arbitrary")),
    )(proj, proj, proj, proj, mu, mu, mu, mu,
      w0, a0, k_k, k_a, r_k, lnx_w, lnx_b, ww, wa, wg)


def _attn_kernel(sink_ref, slope_ref, q_ref, kp_ref, kc_ref, vp_ref, vc_ref, og_ref, o_ref, *, gqa):
    blk = pl.program_id(2)
    pair = pl.program_id(1)
    wd = WINDOW
    qi = lax.broadcasted_iota(jnp.int32, (wd, 2 * wd), 0)
    kj = lax.broadcasted_iota(jnp.int32, (wd, 2 * wd), 1)
    dist = qi + wd - kj
    valid = (dist >= 0) & (dist < wd) & ((kj >= wd) | (blk > 0))
    dist_f = dist.astype(F32)
    scale = HEAD_DIM ** -0.5
    k2 = jnp.concatenate([kp_ref[...], kc_ref[...]], axis=0)
    v2 = jnp.concatenate([vp_ref[...], vc_ref[...]], axis=0)
    outs = []
    for kvh in range(2):
        kh = k2[:, kvh * HEAD_DIM:(kvh + 1) * HEAD_DIM]
        vh = v2[:, kvh * HEAD_DIM:(kvh + 1) * HEAD_DIM]
        for gi in range(gqa):
            hl = kvh * gqa + gi
            head = pair * (2 * gqa) + hl
            slope = slope_ref[head]
            sink = sink_ref[head]
            qh = q_ref[:, hl * HEAD_DIM:(hl + 1) * HEAD_DIM]
            s = lax.dot_general(qh, kh, (((1,), (1,)), ((), ())), preferred_element_type=F32)
            s = s * scale - slope * dist_f
            s = jnp.where(valid, s, -jnp.inf)
            m = jnp.maximum(jnp.max(s, axis=-1, keepdims=True), sink)
            p = jnp.exp(s - m)
            denom = jnp.sum(p, axis=-1, keepdims=True) + jnp.exp(sink - m)
            p = p / denom
            o = jnp.dot(p.astype(BF16), vh, preferred_element_type=F32)
            o = o * lax.rsqrt(jnp.mean(o * o, axis=-1, keepdims=True) + NORM_EPS)
            outs.append(o)
    o_all = jnp.concatenate(outs, axis=-1) * og_ref[...]
    o_ref[...] = o_all.astype(o_ref.dtype)


def _attention(qkv, sinks, out_g, *, batch, seq, d_attn):
    d_kv = N_KV_HEADS * HEAD_DIM
    n_q_heads = d_attn // HEAD_DIM
    gqa = n_q_heads // N_KV_HEADS
    qw = 2 * gqa * HEAD_DIM
    n_pairs = N_KV_HEADS // 2
    nb = seq // WINDOW
    assert qw % LANES == 0 and d_attn % LANES == 0
    k_off = d_attn // LANES
    v_off = (d_attn + d_kv) // LANES

    def cur(off):
        return pl.BlockSpec((WINDOW, LANES), lambda b, p, i, *_: (b * nb + i, off + p))

    def prev(off):
        return pl.BlockSpec((WINDOW, LANES),
                            lambda b, p, i, *_: (b * nb + jnp.maximum(i - 1, 0), off + p))

    slopes = jnp.exp2(-8.0 * jnp.arange(1, n_q_heads + 1, dtype=F32) / n_q_heads)
    grid_spec = pltpu.PrefetchScalarGridSpec(
        num_scalar_prefetch=2,
        grid=(batch, n_pairs, nb),
        in_specs=[pl.BlockSpec((WINDOW, qw), lambda b, p, i, *_: (b * nb + i, p)),
                  prev(k_off), cur(k_off), prev(v_off), cur(v_off),
                  pl.BlockSpec((1, qw), lambda b, p, i, *_: (0, p))],
        out_specs=pl.BlockSpec((WINDOW, qw), lambda b, p, i, *_: (b * nb + i, p)),
    )
    return pl.pallas_call(
        functools.partial(_attn_kernel, gqa=gqa),
        grid_spec=grid_spec,
        out_shape=jax.ShapeDtypeStruct((batch * seq, d_attn), BF16),
        compiler_params=_params(("arbitrary", "arbitrary", "arbitrary")),
    )(sinks, slopes, qkv, qkv, qkv, qkv, qkv, out_g.reshape(1, d_attn))


def _outproj_kernel(yr_ref, ya_ref, w1_ref, w2_ref, x_ref, g_ref, o_ref):
    acc = jnp.dot(yr_ref[...], w1_ref[...], preferred_element_type=F32)
    acc = acc + jnp.dot(ya_ref[...], w2_ref[...], preferred_element_type=F32)
    o_ref[...] = x_ref[...] + g_ref[0] * acc


def _outproj(y_rwkv, y_attn, w_out, x2d, gate, seq):
    t, d = x2d.shape
    k1 = y_rwkv.shape[1]
    k2 = y_attn.shape[1]
    assert k1 == k2
    tm = _pick_tile(seq, (1024, 512, 256, 128))
    tn = _pick_tile(d, (512, 256, 128))
    per_b = seq // tm
    return pl.pallas_call(
        _outproj_kernel,
        grid=(t // tm, d // tn),
        in_specs=[pl.BlockSpec((tm, k1), lambda i, j: (i, 0)),
                  pl.BlockSpec((tm, k2), lambda i, j: (i, 0)),
                  pl.BlockSpec((k1, tn), lambda i, j: (0, j)),
                  pl.BlockSpec((k2, tn), lambda i, j: (1, j)),
                  pl.BlockSpec((tm, tn), lambda i, j: (i, j)),
                  pl.BlockSpec((1, 1, tn), lambda i, j: (i // per_b, 0, j))],
        out_specs=pl.BlockSpec((tm, tn), lambda i, j: (i, j)),
        out_shape=jax.ShapeDtypeStruct((t, d), F32),
        compiler_params=_params(("arbitrary", "arbitrary")),
    )(y_rwkv, y_attn, w_out, w_out, x2d, gate[:, None, :])


def _router_kernel(u_ref, w_ref, b_ref, o_ref):
    logits = jnp.dot(u_ref[...].astype(BF16), w_ref[...], preferred_element_type=F32) + b_ref[...]
    col = lax.broadcasted_iota(jnp.int32, logits.shape, 1).astype(F32)
    ng = float(N_GROUPS)
    epg = float(EXPERTS_PER_GROUP)
    big = 1e9
    is_g = col < ng
    lg = jnp.where(is_g, logits, -jnp.inf)
    mg = jnp.max(lg, axis=-1, keepdims=True)
    gidx = jnp.min(jnp.where(lg == mg, col, big), axis=-1, keepdims=True)
    zg = jnp.sum(jnp.where(is_g, jnp.exp(lg - mg), 0.0), axis=-1, keepdims=True)
    g_w = 1.0 / zg
    lo = ng + gidx * epg
    in_grp = (col >= lo) & (col < lo + epg)
    le = jnp.where(in_grp, logits, -jnp.inf)
    m1 = jnp.max(le, axis=-1, keepdims=True)
    i1 = jnp.min(jnp.where(le == m1, col, big), axis=-1, keepdims=True)
    le2 = jnp.where(col == i1, -jnp.inf, le)
    m2 = jnp.max(le2, axis=-1, keepdims=True)
    i2 = jnp.min(jnp.where(le2 == m2, col, big), axis=-1, keepdims=True)
    e2 = jnp.exp(m2 - m1)
    w1 = 1.0 / (1.0 + e2)
    w2 = e2 / (1.0 + e2)
    out = jnp.where(col == 0.0, i1 - ng,
                    jnp.where(col == 1.0, i2 - ng,
                              jnp.where(col == 2.0, g_w * w1,
                                        jnp.where(col == 3.0, g_w * w2, 0.0))))
    o_ref[...] = out


def _router(u2, w_r, b_r):
    t, d = u2.shape
    tm = _pick_tile(t, (512, 256, 128))
    return pl.pallas_call(
        _router_kernel,
        grid=(t // tm,),
        in_specs=[pl.BlockSpec((tm, d), lambda i: (i, 0)),
                  pl.BlockSpec((d, LANES), lambda i: (0, 0)),
                  pl.BlockSpec((1, LANES), lambda i: (0, 0))],
        out_specs=pl.BlockSpec((tm, LANES), lambda i: (i, 0)),
        out_shape=jax.ShapeDtypeStruct((t, LANES), F32),
        compiler_params=_params(("arbitrary",)),
    )(u2, w_r, b_r)


def _gather_rows(idx_ref, n_rows, src_hbm, dst, sem, col0=0):
    def body(r, carry):
        tok = idx_ref[0, 0, col0 + r]
        pltpu.make_async_copy(src_hbm.at[pl.ds(tok, 1)], dst.at[pl.ds(r, 1)], sem).start()
        return carry
    lax.fori_loop(0, n_rows, body, 0)


def _wait_rows(n_rows, src_hbm, dst, sem):
    pltpu.make_async_copy(src_hbm.at[pl.ds(0, n_rows)], dst, sem).wait()


def _expert_kernel(be_ref, nu_ref, tok_ref, tokn_ref, w_ref, u_hbm, wg_ref, wu_ref, wd_ref,
                   o_ref, xbuf, sem):
    i = pl.program_id(0)
    n_used = nu_ref[0]
    slot = i % 2
    rows = xbuf.shape[1]

    @pl.when(i == 0)
    def _():
        _gather_rows(tok_ref, rows, u_hbm, xbuf.at[0], sem.at[0])

    @pl.when(i + 1 < n_used)
    def _():
        _gather_rows(tokn_ref, rows, u_hbm, xbuf.at[1 - slot], sem.at[1 - slot])

    @pl.when(i < n_used)
    def _():
        _wait_rows(rows, u_hbm, xbuf.at[slot], sem.at[slot])
        xb = xbuf[slot].astype(BF16)
        gate = jnp.dot(xb, wg_ref[0], preferred_element_type=F32)
        up = jnp.dot(xb, wu_ref[0], preferred_element_type=F32)
        h = (gate * jax.nn.sigmoid(gate) * up).astype(BF16)
        y = jnp.dot(h, wd_ref[0], preferred_element_type=F32)
        o_ref[...] = y * w_ref[...]

    @pl.when(i >= n_used)
    def _():
        o_ref[...] = jnp.zeros_like(o_ref)


def _experts(block_expert, n_used, row_tok, row_w, u2, wg, wu, wd):
    t, d = u2.shape
    nb = block_expert.shape[0]
    rows = MOE_ROWS
    de = wg.shape[-1]
    tok3 = row_tok.reshape(nb, 1, rows)
    grid_spec = pltpu.PrefetchScalarGridSpec(
        num_scalar_prefetch=2,
        grid=(nb,),
        in_specs=[pl.BlockSpec((1, 1, rows), lambda i, be, nu: (i, 0, 0), memory_space=pltpu.SMEM),
                  pl.BlockSpec((1, 1, rows), lambda i, be, nu: (jnp.minimum(i + 1, nb - 1), 0, 0),
                               memory_space=pltpu.SMEM),
                  pl.BlockSpec((rows, 1), lambda i, be, nu: (i, 0)),
                  pl.BlockSpec(memory_space=pl.ANY),
                  pl.BlockSpec((1, d, de), lambda i, be, nu: (be[i], 0, 0)),
                  pl.BlockSpec((1, d, de), lambda i, be, nu: (be[i], 0, 0)),
                  pl.BlockSpec((1, de, d), lambda i, be, nu: (be[i], 0, 0))],
        out_specs=pl.BlockSpec((rows, d), lambda i, be, nu: (i, 0)),
        scratch_shapes=[pltpu.VMEM((2, rows, d), F32), pltpu.SemaphoreType.DMA((2,))],
    )
    return pl.pallas_call(
        _expert_kernel,
        grid_spec=grid_spec,
        out_shape=jax.ShapeDtypeStruct((nb * rows, d), F32),
        compiler_params=_params(("arbitrary",), vmem_mib=58),
    )(block_expert, n_used, tok3, tok3, row_w.reshape(nb * rows, 1), u2, wg, wu, wd)


def _combine_kernel(pos_ref, ys_hbm, x_ref, g_ref, ng_ref, o_ref, buf, sem):
    tc = x_ref.shape[0]
    _gather_rows(pos_ref, tc, ys_hbm, buf.at[0], sem.at[0], 0)
    _gather_rows(pos_ref, tc, ys_hbm, buf.at[1], sem.at[0], tc)
    _wait_rows(tc, ys_hbm, buf.at[0], sem.at[0])
    _wait_rows(tc, ys_hbm, buf.at[1], sem.at[0])
    x = x_ref[...] + g_ref[0] * (buf[0] + buf[1])
    y = x * lax.rsqrt(jnp.mean(x * x, axis=-1, keepdims=True) + NORM_EPS)
    o_ref[...] = y * ng_ref[...]


def _combine(pos, ys, x1, gate, norm_g, seq):
    t, d = x1.shape
    tc = _pick_tile(seq, (128,))
    n = t // tc
    per_b = seq // tc
    pos3 = pos.reshape(n, tc, 2).transpose(0, 2, 1).reshape(n, 1, 2 * tc)
    return pl.pallas_call(
        _combine_kernel,
        grid=(n,),
        in_specs=[pl.BlockSpec((1, 1, 2 * tc), lambda i: (i, 0, 0), memory_space=pltpu.SMEM),
                  pl.BlockSpec(memory_space=pl.ANY),
                  pl.BlockSpec((tc, d), lambda i: (i, 0)),
                  pl.BlockSpec((1, 1, d), lambda i: (i // per_b, 0, 0)),
                  pl.BlockSpec((1, d), lambda i: (0, 0))],
        out_specs=pl.BlockSpec((tc, d), lambda i: (i, 0)),
        out_shape=jax.ShapeDtypeStruct((t, d), F32),
        scratch_shapes=[pltpu.VMEM((2, tc, d), F32), pltpu.SemaphoreType.DMA((1,))],
        compiler_params=_params(("arbitrary",)),
    )(pos3, ys, x1, gate[:, None, :], norm_g.reshape(1, d))


def _dispatch(expert_id, weight, n_tok):
    m = expert_id.shape[0]
    rows = MOE_ROWS
    nb = -(-m // rows) + N_EXPERTS
    onehot = (expert_id[:, None] == jnp.arange(N_EXPERTS, dtype=jnp.int32)[None, :]).astype(jnp.int32)
    csum = jnp.cumsum(onehot, axis=0)
    rank = jnp.take_along_axis(csum, expert_id[:, None], axis=1)[:, 0] - 1
    counts = csum[-1]
    padded = ((counts + rows - 1) // rows) * rows
    pends = jnp.cumsum(padded)
    pstarts = pends - padded
    dest = (pstarts[expert_id] + rank).astype(jnp.int32)
    tok = jnp.arange(m, dtype=jnp.int32) // (m // n_tok)
    row_tok = jnp.zeros((nb * rows,), jnp.int32).at[dest].set(tok)
    row_w = jnp.zeros((nb * rows,), F32).at[dest].set(weight)
    n_used = (pends[-1] // rows).astype(jnp.int32)
    blk = jnp.arange(nb, dtype=jnp.int32)
    blk = jnp.minimum(blk, n_used - 1)
    block_expert = jnp.minimum(jnp.searchsorted(pends, blk * rows, side='right'),
                               N_EXPERTS - 1).astype(jnp.int32)
    return block_expert, n_used.reshape(1), row_tok, row_w, dest


def _layer(x, c, w_cond, b_cond, norm1_g, w_in, rwkv_mu, rwkv_w0, rwkv_w_up, rwkv_a0, rwkv_a_up,
           rwkv_g_up, rwkv_k_k, rwkv_k_a, rwkv_r_k, rwkv_lnx_w, rwkv_lnx_b, attn_sinks, attn_out_g,
           w_out, norm2_g, router_group, router_group_bias, router_expert, router_expert_bias,
           expert_w_gate, expert_w_up, expert_w_down):
    b, s, d = x.shape
    t = b * s
    d_rwkv = rwkv_w0.shape[-1]
    d_attn = attn_out_g.shape[-1]
    dw, da, dg = rwkv_w_up.shape[0], rwkv_a_up.shape[0], rwkv_g_up.shape[0]
    n_lora = dw + da + dg
    lp = _round_up(n_lora, LANES)
    while (3 * d_rwkv) % lp:
        lp += LANES
    n_rwkv_cols = 3 * d_rwkv + n_lora

    c_pad = jnp.zeros((_round_up(b, 8), d), F32).at[:b].set(c)
    mod = _cond(c_pad, w_cond, b_cond.reshape(1, -1))[:b]
    sh1, sc1, g1, sh2, sc2, g2 = jnp.split(mod, 6, axis=-1)

    x2d = x.reshape(t, d)
    u1 = _norm_mod(x2d, norm1_g, sc1, sh1, s, BF16)

    zpad = jnp.zeros((d, lp - n_lora), F32)
    w_r = jnp.concatenate([w_in[:, :n_rwkv_cols], zpad], axis=1).astype(BF16)
    w_a = w_in[:, n_rwkv_cols:].astype(BF16)
    proj_r = _matmul(u1, w_r, F32)
    proj_a = _matmul(u1, w_a, BF16)

    def lora_rows(wmat, off):
        return jnp.zeros((lp, d_rwkv), F32).at[off:off + wmat.shape[0]].set(wmat).astype(BF16)

    mu = jnp.concatenate([rwkv_mu, jnp.zeros((lp - n_lora,), F32)]).reshape(1, -1)
    row = lambda v: v.reshape(1, d_rwkv)
    y_rwkv = _rwkv(proj_r.reshape(b, s, -1), mu, row(rwkv_w0), row(rwkv_a0), row(rwkv_k_k),
                   row(rwkv_k_a), row(rwkv_r_k), row(rwkv_lnx_w), row(rwkv_lnx_b),
                   lora_rows(rwkv_w_up, 0), lora_rows(rwkv_a_up, dw), lora_rows(rwkv_g_up, dw + da),
                   d_rwkv=d_rwkv, lp=lp, dw=dw, da=da).reshape(t, d_rwkv)

    y_attn = _attention(proj_a, attn_sinks, attn_out_g, batch=b, seq=s, d_attn=d_attn)

    x1 = _outproj(y_rwkv, y_attn, w_out.astype(BF16), x2d, g1, s)

    u2 = _norm_mod(x1, norm2_g, sc2, sh2, s, F32)
    n_r = N_GROUPS + N_EXPERTS
    w_router = jnp.concatenate([router_group, router_expert, jnp.zeros((d, LANES - n_r), F32)],
                               axis=1).astype(BF16)
    b_router = jnp.concatenate([router_group_bias, router_expert_bias,
                                jnp.zeros((LANES - n_r,), F32)]).reshape(1, LANES)
    route = _router(u2, w_router, b_router)
    expert_id = route[:, :2].astype(jnp.int32).reshape(-1)
    weight = route[:, 2:4].reshape(-1)
    block_expert, n_used, row_tok, row_w, dest = _dispatch(expert_id, weight, t)
    ys = _experts(block_expert, n_used, row_tok, row_w, u2, expert_w_gate.astype(BF16),
                  expert_w_up.astype(BF16), expert_w_down.astype(BF16))
    return ys, dest.reshape(t, 2), x1, g2


def kernel(x, c, w_cond, b_cond, norm1_g, w_in, rwkv_mu, rwkv_w0, rwkv_w_up, rwkv_a0, rwkv_a_up, rwkv_g_up, rwkv_k_k, rwkv_k_a, rwkv_r_k, rwkv_lnx_w, rwkv_lnx_b, attn_sinks, attn_out_g, w_out, norm2_g, router_group, router_group_bias, router_expert, router_expert_bias, expert_w_gate, expert_w_up, expert_w_down, norm_f_g):
    b, s, d = x.shape
    depth = w_cond.shape[0]
    assert depth == 1, "the fused final norm assumes a single layer"
    l = 0
    ys, pos, x1, g2 = _layer(
        x, c, w_cond[l], b_cond[l], norm1_g[l], w_in[l], rwkv_mu[l], rwkv_w0[l], rwkv_w_up[l],
        rwkv_a0[l], rwkv_a_up[l], rwkv_g_up[l], rwkv_k_k[l], rwkv_k_a[l], rwkv_r_k[l],
        rwkv_lnx_w[l], rwkv_lnx_b[l], attn_sinks[l], attn_out_g[l], w_out[l], norm2_g[l],
        router_group[l], router_group_bias[l], router_expert[l], router_expert_bias[l],
        expert_w_gate[l], expert_w_up[l], expert_w_down[l])
    out = _combine(pos, ys, x1, g2, norm_f_g, s)
    return out.reshape(b, s, d)
```

```python
import functools
import math

import jax
import jax.numpy as jnp
from jax import lax
from jax.experimental import pallas as pl
from jax.experimental.pallas import tpu as pltpu

F32 = jnp.float32
BF16 = jnp.bfloat16

HEAD_DIM = 64
N_KV_HEADS = 8
WINDOW = 128
N_GROUPS = 8
EXPERTS_PER_GROUP = 8
N_EXPERTS = N_GROUPS * EXPERTS_PER_GROUP
RWKV_GN_EPS = 64e-5
NORM_EPS = 1e-6
RWKV_CHUNK = 64
RWKV_HEADS_PER_STEP = 4
MOE_ROWS = 256
LANES = 128
MIB = 1024 * 1024


def _params(sem, vmem_mib=48):
    return pltpu.CompilerParams(dimension_semantics=sem, vmem_limit_bytes=vmem_mib * MIB)


def _round_up(n, m):
    return -(-n // m) * m


def _pick_tile(n, candidates):
    for t in candidates:
        if n % t == 0:
            return t
    raise ValueError(f"no tile in {candidates} divides {n}")


def _cond_kernel(c_ref, w_ref, b_ref, o_ref):
    c = c_ref[...]
    sc = (c * jax.nn.sigmoid(c)).astype(BF16)
    o_ref[...] = jnp.dot(sc, w_ref[...].astype(BF16), preferred_element_type=F32) + b_ref[...]


def _cond(c_pad, w, b):
    rows, d = c_pad.shape
    n = w.shape[1]
    tn = _pick_tile(n, (1024, 512, 256, 128))
    return pl.pallas_call(
        _cond_kernel,
        grid=(n // tn,),
        in_specs=[pl.BlockSpec((rows, d), lambda j: (0, 0)),
                  pl.BlockSpec((d, tn), lambda j: (0, j)),
                  pl.BlockSpec((1, tn), lambda j: (0, j))],
        out_specs=pl.BlockSpec((rows, tn), lambda j: (0, j)),
        out_shape=jax.ShapeDtypeStruct((rows, n), F32),
        compiler_params=_params(("arbitrary",)),
    )(c_pad, w, b)


def _norm_mod_kernel(x_ref, g_ref, sc_ref, sh_ref, o_ref):
    x = x_ref[...]
    y = x * lax.rsqrt(jnp.mean(x * x, axis=-1, keepdims=True) + NORM_EPS)
    y = y * g_ref[...]
    o_ref[...] = (y * (1.0 + sc_ref[0]) + sh_ref[0]).astype(o_ref.dtype)


def _norm_mod(x2d, g, scale, shift, seq, out_dtype):
    t, d = x2d.shape
    tm = _pick_tile(seq, (512, 256, 128))
    per_b = seq // tm
    return pl.pallas_call(
        _norm_mod_kernel,
        grid=(t // tm,),
        in_specs=[pl.BlockSpec((tm, d), lambda i: (i, 0)),
                  pl.BlockSpec((1, d), lambda i: (0, 0)),
                  pl.BlockSpec((1, 1, d), lambda i: (i // per_b, 0, 0)),
                  pl.BlockSpec((1, 1, d), lambda i: (i // per_b, 0, 0))],
        out_specs=pl.BlockSpec((tm, d), lambda i: (i, 0)),
        out_shape=jax.ShapeDtypeStruct((t, d), out_dtype),
        compiler_params=_params(("arbitrary",)),
    )(x2d, g.reshape(1, d), scale[:, None, :], shift[:, None, :])


def _mm_kernel(a_ref, b_ref, o_ref):
    o_ref[...] = jnp.dot(a_ref[...], b_ref[...], preferred_element_type=F32).astype(o_ref.dtype)


def _matmul(a, b, out_dtype):
    m, k = a.shape
    n = b.shape[1]
    tm = _pick_tile(m, (1024, 512, 256, 128))
    tn = _pick_tile(n, (512, 256, 128))
    return pl.pallas_call(
        _mm_kernel,
        grid=(m // tm, n // tn),
        in_specs=[pl.BlockSpec((tm, k), lambda i, j: (i, 0)),
                  pl.BlockSpec((k, tn), lambda i, j: (0, j))],
        out_specs=pl.BlockSpec((tm, tn), lambda i, j: (i, j)),
        out_shape=jax.ShapeDtypeStruct((m, n), out_dtype),
        compiler_params=_params(("arbitrary", "arbitrary")),
    )(a, b)


def _softplus(z):
    return jnp.maximum(z, 0.0) + jnp.log(1.0 + jnp.exp(-jnp.abs(z)))


def _rwkv_kernel(r_ref, k_ref, v_ref, l_ref, mur_ref, muk_ref, muv_ref, mul_ref,
                 w0_ref, a0_ref, kk_ref, ka_ref, rk_ref, lw_ref, lb_ref,
                 ww_ref, wa_ref, wg_ref, o_ref,
                 s_ref, pr_ref, pk_ref, pv_ref, pl_ref, *, n_batch, dw, da):
    c = RWKV_CHUNK
    hw = r_ref.shape[-1]
    nh = hw // HEAD_DIM
    lp = l_ref.shape[-1]
    chunk = pl.program_id(1)

    @pl.when(chunk == 0)
    def _():
        s_ref[...] = jnp.zeros_like(s_ref)
        pr_ref[...] = jnp.zeros_like(pr_ref)
        pk_ref[...] = jnp.zeros_like(pk_ref)
        pv_ref[...] = jnp.zeros_like(pv_ref)
        pl_ref[...] = jnp.zeros_like(pl_ref)

    row_w = lax.broadcasted_iota(jnp.int32, (c, hw), 0)
    col_w = lax.broadcasted_iota(jnp.int32, (c, hw), 1)
    s_idx = col_w % c
    strict = row_w > s_idx
    incl = row_w >= s_idx
    eye_w = (row_w == s_idx).astype(F32)
    merge_masks = [strict & (row_w // 2 == s_idx // 2)]
    size = 2
    while size < c:
        merge_masks.append(strict & (row_w // (2 * size) == s_idx // (2 * size))
                           & (row_w // size != s_idx // size))
        size *= 2
    bd_row = lax.broadcasted_iota(jnp.int32, (hw, hw), 0) // HEAD_DIM
    bd_col = lax.broadcasted_iota(jnp.int32, (hw, hw), 1) // HEAD_DIM
    bd_mask = bd_row == bd_col
    bd_ones = bd_mask.astype(BF16)
    tri = (lax.broadcasted_iota(jnp.int32, (c, c), 0)
           >= lax.broadcasted_iota(jnp.int32, (c, c), 1)).astype(BF16)
    row_l = lax.broadcasted_iota(jnp.int32, (c, lp), 0)
    col_l = lax.broadcasted_iota(jnp.int32, (c, lp), 1)

    def shift_mix(x, prev, mu, row):
        shifted = jnp.where(row == 0, prev, pltpu.roll(x, 1, axis=0))
        return x + (shifted - x) * mu

    def split2(x):
        hi = x.astype(BF16)
        lo = (x - hi.astype(F32)).astype(BF16)
        return hi, lo

    nb = range(n_batch)

    def rows_split(x):
        return [x[bi * c:(bi + 1) * c] for bi in nb]

    def seg_sum(xs):
        parts = [split2(x) for x in xs]
        stacked = jnp.concatenate([p[0] for p in parts] + [p[1] for p in parts], axis=0)
        both = jnp.dot(stacked, bd_ones, preferred_element_type=F32)
        half = n_batch * c
        return rows_split(both[:half] + both[half:])

    def bd(x):
        return jnp.where(bd_mask, jnp.tile(x, (nh, 1)), 0.0).astype(BF16)

    def mm(a, b):
        return jnp.dot(a.astype(BF16), b, preferred_element_type=F32)

    def mm_nt(a, b):
        return lax.dot_general(a.astype(BF16), b, (((1,), (1,)), ((), ())),
                               preferred_element_type=F32)

    xr = [shift_mix(r_ref[bi], pr_ref[bi], mur_ref[...], row_w) for bi in nb]
    xk = [shift_mix(k_ref[bi], pk_ref[bi], muk_ref[...], row_w) for bi in nb]
    xv = [shift_mix(v_ref[bi], pv_ref[bi], muv_ref[...], row_w) for bi in nb]
    xl = [shift_mix(l_ref[bi], pl_ref[bi], mul_ref[...], row_l) for bi in nb]
    states = [s_ref[bi] for bi in nb]

    act = jnp.concatenate(
        [jnp.where(col_l < dw, jnp.tanh(x), jnp.where(col_l < dw + da, x, jax.nn.sigmoid(x)))
         for x in xl], axis=0).astype(BF16)
    lora_w = rows_split(jnp.dot(act, ww_ref[...], preferred_element_type=F32))
    lora_a = rows_split(jnp.dot(act, wa_ref[...], preferred_element_type=F32))
    gate = rows_split(jnp.dot(act, wg_ref[...], preferred_element_type=F32))

    e = [jnp.exp(-_softplus(-(w0_ref[...] + lw)) - 0.5) for lw in lora_w]
    lr = [jax.nn.sigmoid(a0_ref[...] + la) for la in lora_a]
    kk = [x * kk_ref[...] for x in xk]
    kk_ss = seg_sum([x * x for x in kk])
    kk = [x / jnp.maximum(jnp.sqrt(ss), 1e-12) for x, ss in zip(kk, kk_ss)]
    kmod = [x * (1.0 + (a - 1.0) * ka_ref[...]) for x, a in zip(xk, lr)]

    cum = []
    for ei in e:
        e_hi, e_lo = split2(ei)
        cum2 = jnp.dot(tri, jnp.concatenate([e_hi, e_lo], axis=1), preferred_element_type=F32)
        cum.append(cum2[:, :hw] + cum2[:, hw:])
    dec = [jnp.exp(-x) for x in cum]
    inv = [jnp.exp(x) for x in cum]
    at = [-kk[bi] * jnp.exp(e[bi] - cum[bi]) for bi in nb]
    rt = [xr[bi] * dec[bi] for bi in nb]
    bt = [kk[bi] * lr[bi] * inv[bi] for bi in nb]
    kt = [kmod[bi] * inv[bi] for bi in nb]

    g = [mm_nt(jnp.concatenate([at[bi], rt[bi]], axis=0),
               jnp.concatenate([bd(bt[bi]), bd(kt[bi])], axis=0)) for bi in nb]
    m_ab = [jnp.where(strict, x[:c, :hw], 0.0) for x in g]
    m_ak = [jnp.where(strict, x[:c, hw:], 0.0) for x in g]
    m_rb = [jnp.where(incl, x[c:, :hw], 0.0) for x in g]
    m_rk = [jnp.where(incl, x[c:, hw:], 0.0) for x in g]

    t_inv = [eye_w + jnp.where(merge_masks[0], m, 0.0) for m in m_ab]
    for mask in merge_masks[1:]:
        z = [mm(jnp.where(mask, m_ab[bi], 0.0), bd(t_inv[bi])) for bi in nb]
        t_inv = [t_inv[bi] + mm(t_inv[bi], bd(z[bi])) for bi in nb]

    s_bf = [s.astype(BF16) for s in states]
    bd_v = [bd(x) for x in xv]
    x_rhs = [mm_nt(at[bi], s_bf[bi]) + mm(m_ak[bi], bd_v[bi]) for bi in nb]
    u = [mm(t_inv[bi], bd(x_rhs[bi])) for bi in nb]
    y = [mm_nt(rt[bi], s_bf[bi]) + mm(m_rb[bi], bd(u[bi])) + mm(m_rk[bi], bd_v[bi]) for bi in nb]
    upd = [lax.dot_general(jnp.concatenate([u[bi], xv[bi]], axis=0).astype(BF16),
                           jnp.concatenate([bt[bi], kt[bi]], axis=0).astype(BF16),
                           (((0,), (0,)), ((), ())), preferred_element_type=F32) for bi in nb]

    mean = [m * (1.0 / HEAD_DIM) for m in seg_sum(y)]
    dy = [y[bi] - mean[bi] for bi in nb]
    var = [v * (1.0 / HEAD_DIM) for v in seg_sum([d * d for d in dy])]
    rk_sum = seg_sum([xr[bi] * kmod[bi] * rk_ref[...] for bi in nb])

    for bi in nb:
        yn = dy[bi] * lax.rsqrt(var[bi] + RWKV_GN_EPS) * lw_ref[...] + lb_ref[...]
        o_ref[bi] = ((yn + rk_sum[bi] * xv[bi]) * gate[bi]).astype(o_ref.dtype)
        s_ref[bi] = (states[bi] + jnp.where(bd_mask, upd[bi], 0.0)) * dec[bi][c - 1:c]
        pr_ref[bi] = r_ref[bi, c - 1:c, :]
        pk_ref[bi] = k_ref[bi, c - 1:c, :]
        pv_ref[bi] = v_ref[bi, c - 1:c, :]
        pl_ref[bi] = l_ref[bi, c - 1:c, :]


def _rwkv(proj, mu, w0, a0, k_k, k_a, r_k, lnx_w, lnx_b, ww, wa, wg, *, d_rwkv, lp, dw, da):
    b, s, _ = proj.shape
    c = RWKV_CHUNK
    hw = RWKV_HEADS_PER_STEP * HEAD_DIM
    assert c == HEAD_DIM and d_rwkv % hw == 0 and s % c == 0 and (3 * d_rwkv) % lp == 0
    nhg = d_rwkv // hw
    lblk = (3 * d_rwkv) // lp

    def act_spec(off):
        return pl.BlockSpec((b, c, hw), lambda h, t: (0, t, off + h))

    def vec_spec(off):
        return pl.BlockSpec((1, hw), lambda h, t: (0, off + h))

    vec = vec_spec(0)
    lora_spec = pl.BlockSpec((lp, hw), lambda h, t: (0, h))
    kern = functools.partial(_rwkv_kernel, n_batch=b, dw=dw, da=da)
    return pl.pallas_call(
        kern,
        grid=(nhg, s // c),
        in_specs=[act_spec(0), act_spec(nhg), act_spec(2 * nhg),
                  pl.BlockSpec((b, c, lp), lambda h, t: (0, t, lblk)),
                  vec_spec(0), vec_spec(nhg), vec_spec(2 * nhg),
                  pl.BlockSpec((1, lp), lambda h, t: (0, lblk)),
                  vec, vec, vec, vec, vec, vec, vec,
                  lora_spec, lora_spec, lora_spec],
        out_specs=pl.BlockSpec((b, c, hw), lambda h, t: (0, t, h)),
        out_shape=jax.ShapeDtypeStruct((b, s, d_rwkv), BF16),
        scratch_shapes=[pltpu.VMEM((b, hw, hw), F32),
                        pltpu.VMEM((b, 1, hw), F32), pltpu.VMEM((b, 1, hw), F32),
                        pltpu.VMEM((b, 1, hw), F32), pltpu.VMEM((b, 1, lp), F32)],
        compiler_params=_params(("arbitrary", "arbitrary")),
    )(proj, proj, proj, proj, mu, mu, mu, mu,
      w0, a0, k_k, k_a, r_k, lnx_w, lnx_b, ww, wa, wg)


def _attn_kernel(sink_ref, slope_ref, q_ref, kp_ref, kc_ref, vp_ref, vc_ref, og_ref, o_ref, *, gqa):
    blk = pl.program_id(2)
    pair = pl.program_id(1)
    wd = WINDOW
    qi = lax.broadcasted_iota(jnp.int32, (wd, 2 * wd), 0)
    kj = lax.broadcasted_iota(jnp.int32, (wd, 2 * wd), 1)
    dist = qi + wd - kj
    valid = (dist >= 0) & (dist < wd) & ((kj >= wd) | (blk > 0))
    dist_f = dist.astype(F32)
    scale = HEAD_DIM ** -0.5
    k2 = jnp.concatenate([kp_ref[...], kc_ref[...]], axis=0)
    v2 = jnp.concatenate([vp_ref[...], vc_ref[...]], axis=0)
    heads = range(2 * gqa)
    ks = [k2[:, kvh * HEAD_DIM:(kvh + 1) * HEAD_DIM] for kvh in range(2)]
    vs = [v2[:, kvh * HEAD_DIM:(kvh + 1) * HEAD_DIM] for kvh in range(2)]
    slopes = [slope_ref[pair * (2 * gqa) + hl] for hl in heads]
    sinks = [sink_ref[pair * (2 * gqa) + hl] for hl in heads]
    s = [lax.dot_general(q_ref[:, hl * HEAD_DIM:(hl + 1) * HEAD_DIM], ks[hl // gqa],
                         (((1,), (1,)), ((), ())), preferred_element_type=F32) for hl in heads]
    s = [jnp.where(valid, s[hl] * scale - slopes[hl] * dist_f, -jnp.inf) for hl in heads]
    m = [jnp.maximum(jnp.max(s[hl], axis=-1, keepdims=True), sinks[hl]) for hl in heads]
    p = [jnp.exp(s[hl] - m[hl]) for hl in heads]
    denom = [jnp.sum(p[hl], axis=-1, keepdims=True) + jnp.exp(sinks[hl] - m[hl]) for hl in heads]
    o = [jnp.dot(p[hl].astype(BF16), vs[hl // gqa], preferred_element_type=F32) / denom[hl]
         for hl in heads]
    o = [x * lax.rsqrt(jnp.mean(x * x, axis=-1, keepdims=True) + NORM_EPS) for x in o]
    o_all = jnp.concatenate(o, axis=-1) * og_ref[...]
    o_ref[...] = o_all.astype(o_ref.dtype)


def _attention(qkv, sinks, out_g, *, batch, seq, d_attn):
    d_kv = N_KV_HEADS * HEAD_DIM
    n_q_heads = d_attn // HEAD_DIM
    gqa = n_q_heads // N_KV_HEADS
    qw = 2 * gqa * HEAD_DIM
    n_pairs = N_KV_HEADS // 2
    nb = seq // WINDOW
    assert qw % LANES == 0 and d_attn % LANES == 0
    k_off = d_attn // LANES
    v_off = (d_attn + d_kv) // LANES

    def cur(off):
        return pl.BlockSpec((WINDOW, LANES), lambda b, p, i, *_: (b * nb + i, off + p))

    def prev(off):
        return pl.BlockSpec((WINDOW, LANES),
                            lambda b, p, i, *_: (b * nb + jnp.maximum(i - 1, 0), off + p))

    slopes = jnp.exp2(-8.0 * jnp.arange(1, n_q_heads + 1, dtype=F32) / n_q_heads)
    grid_spec = pltpu.PrefetchScalarGridSpec(
        num_scalar_prefetch=2,
        grid=(batch, n_pairs, nb),
        in_specs=[pl.BlockSpec((WINDOW, qw), lambda b, p, i, *_: (b * nb + i, p)),
                  prev(k_off), cur(k_off), prev(v_off), cur(v_off),
                  pl.BlockSpec((1, qw), lambda b, p, i, *_: (0, p))],
        out_specs=pl.BlockSpec((WINDOW, qw), lambda b, p, i, *_: (b * nb + i, p)),
    )
    return pl.pallas_call(
        functools.partial(_attn_kernel, gqa=gqa),
        grid_spec=grid_spec,
        out_shape=jax.ShapeDtypeStruct((batch * seq, d_attn), BF16),
        compiler_params=_params(("arbitrary", "arbitrary", "arbitrary")),
    )(sinks, slopes, qkv, qkv, qkv, qkv, qkv, out_g.reshape(1, d_attn))


def _outproj_kernel(yr_ref, ya_ref, w1_ref, w2_ref, x_ref, g_ref, o_ref):
    acc = jnp.dot(yr_ref[...], w1_ref[...], preferred_element_type=F32)
    acc = acc + jnp.dot(ya_ref[...], w2_ref[...], preferred_element_type=F32)
    o_ref[...] = x_ref[...] + g_ref[0] * acc


def _outproj(y_rwkv, y_attn, w_out, x2d, gate, seq):
    t, d = x2d.shape
    k1 = y_rwkv.shape[1]
    k2 = y_attn.shape[1]
    assert k1 == k2
    tm = _pick_tile(seq, (1024, 512, 256, 128))
    tn = _pick_tile(d, (512, 256, 128))
    per_b = seq // tm
    return pl.pallas_call(
        _outproj_kernel,
        grid=(t // tm, d // tn),
        in_specs=[pl.BlockSpec((tm, k1), lambda i, j: (i, 0)),
                  pl.BlockSpec((tm, k2), lambda i, j: (i, 0)),
                  pl.BlockSpec((k1, tn), lambda i, j: (0, j)),
                  pl.BlockSpec((k2, tn), lambda i, j: (1, j)),
                  pl.BlockSpec((tm, tn), lambda i, j: (i, j)),
                  pl.BlockSpec((1, 1, tn), lambda i, j: (i // per_b, 0, j))],
        out_specs=pl.BlockSpec((tm, tn), lambda i, j: (i, j)),
        out_shape=jax.ShapeDtypeStruct((t, d), F32),
        compiler_params=_params(("arbitrary", "arbitrary")),
    )(y_rwkv, y_attn, w_out, w_out, x2d, gate[:, None, :])


def _router_kernel(u_ref, w_ref, b_ref, o_ref):
    logits = jnp.dot(u_ref[...].astype(BF16), w_ref[...], preferred_element_type=F32) + b_ref[...]
    col = lax.broadcasted_iota(jnp.int32, logits.shape, 1).astype(F32)
    ng = float(N_GROUPS)
    epg = float(EXPERTS_PER_GROUP)
    big = 1e9
    is_g = col < ng
    lg = jnp.where(is_g, logits, -jnp.inf)
    mg = jnp.max(lg, axis=-1, keepdims=True)
    gidx = jnp.min(jnp.where(lg == mg, col, big), axis=-1, keepdims=True)
    zg = jnp.sum(jnp.where(is_g, jnp.exp(lg - mg), 0.0), axis=-1, keepdims=True)
    g_w = 1.0 / zg
    lo = ng + gidx * epg
    in_grp = (col >= lo) & (col < lo + epg)
    le = jnp.where(in_grp, logits, -jnp.inf)
    m1 = jnp.max(le, axis=-1, keepdims=True)
    i1 = jnp.min(jnp.where(le == m1, col, big), axis=-1, keepdims=True)
    le2 = jnp.where(col == i1, -jnp.inf, le)
    m2 = jnp.max(le2, axis=-1, keepdims=True)
    i2 = jnp.min(jnp.where(le2 == m2, col, big), axis=-1, keepdims=True)
    e2 = jnp.exp(m2 - m1)
    w1 = 1.0 / (1.0 + e2)
    w2 = e2 / (1.0 + e2)
    out = jnp.where(col == 0.0, i1 - ng,
                    jnp.where(col == 1.0, i2 - ng,
                              jnp.where(col == 2.0, g_w * w1,
                                        jnp.where(col == 3.0, g_w * w2, 0.0))))
    o_ref[...] = out


def _router(u2, w_r, b_r):
    t, d = u2.shape
    tm = _pick_tile(t, (512, 256, 128))
    return pl.pallas_call(
        _router_kernel,
        grid=(t // tm,),
        in_specs=[pl.BlockSpec((tm, d), lambda i: (i, 0)),
                  pl.BlockSpec((d, LANES), lambda i: (0, 0)),
                  pl.BlockSpec((1, LANES), lambda i: (0, 0))],
        out_specs=pl.BlockSpec((tm, LANES), lambda i: (i, 0)),
        out_shape=jax.ShapeDtypeStruct((t, LANES), F32),
        compiler_params=_params(("arbitrary",)),
    )(u2, w_r, b_r)


def _gather_rows(idx_ref, n_rows, src_hbm, dst, sem, col0=0, unroll=False):
    def issue(r):
        tok = idx_ref[0, 0, col0 + r]
        pltpu.make_async_copy(src_hbm.at[pl.ds(tok, 1)], dst.at[pl.ds(r, 1)], sem).start()

    if unroll:
        for r in range(n_rows):
            issue(r)
    else:
        def body(r, carry):
            issue(r)
            return carry
        lax.fori_loop(0, n_rows, body, 0)


def _wait_rows(n_rows, src_hbm, dst, sem):
    pltpu.make_async_copy(src_hbm.at[pl.ds(0, n_rows)], dst, sem).wait()


def _expert_kernel(be_ref, nu_ref, tok_ref, tokn_ref, w_ref, u_hbm, wg_ref, wu_ref, wd_ref,
                   o_ref, xbuf_a, xbuf_b, sem):
    i = pl.program_id(0)
    last = pl.num_programs(0) - 1
    n_used = nu_ref[0]
    rows = xbuf_a.shape[0]

    @pl.when(i == 0)
    def _():
        _gather_rows(tok_ref, rows, u_hbm, xbuf_a, sem.at[0])

    def step(cur, cur_sem, nxt, nxt_sem):
        _wait_rows(rows, u_hbm, cur, cur_sem)

        @pl.when(i < n_used)
        def _():
            _gather_rows(tokn_ref, rows, u_hbm, nxt, nxt_sem, unroll=True)
            xb = cur[...].astype(BF16)
            gate = jnp.dot(xb, wg_ref[0], preferred_element_type=F32)
            up = jnp.dot(xb, wu_ref[0], preferred_element_type=F32)
            h = (gate * jax.nn.sigmoid(gate) * up).astype(BF16)
            y = jnp.dot(h, wd_ref[0], preferred_element_type=F32)
            o_ref[...] = y * w_ref[...]

        @pl.when(i >= n_used)
        def _():
            _gather_rows(tokn_ref, rows, u_hbm, nxt, nxt_sem)
            o_ref[...] = jnp.zeros_like(o_ref)

        @pl.when(i == last)
        def _():
            _wait_rows(rows, u_hbm, nxt, nxt_sem)

    @pl.when(i % 2 == 0)
    def _():
        step(xbuf_a, sem.at[0], xbuf_b, sem.at[1])

    @pl.when(i % 2 == 1)
    def _():
        step(xbuf_b, sem.at[1], xbuf_a, sem.at[0])


def _experts(block_expert, n_used, row_tok, row_w, u2, wg, wu, wd):
    t, d = u2.shape
    nb = block_expert.shape[0]
    rows = MOE_ROWS
    de = wg.shape[-1]
    tok3 = row_tok.reshape(nb, 1, rows)
    grid_spec = pltpu.PrefetchScalarGridSpec(
        num_scalar_prefetch=2,
        grid=(nb,),
        in_specs=[pl.BlockSpec((1, 1, rows), lambda i, be, nu: (i, 0, 0), memory_space=pltpu.SMEM),
                  pl.BlockSpec((1, 1, rows), lambda i, be, nu: (jnp.minimum(i + 1, nb - 1), 0, 0),
                               memory_space=pltpu.SMEM),
                  pl.BlockSpec((rows, 1), lambda i, be, nu: (i, 0)),
                  pl.BlockSpec(memory_space=pl.ANY),
                  pl.BlockSpec((1, d, de), lambda i, be, nu: (be[i], 0, 0)),
                  pl.BlockSpec((1, d, de), lambda i, be, nu: (be[i], 0, 0)),
                  pl.BlockSpec((1, de, d), lambda i, be, nu: (be[i], 0, 0))],
        out_specs=pl.BlockSpec((rows, d), lambda i, be, nu: (i, 0)),
        scratch_shapes=[pltpu.VMEM((rows, d), F32), pltpu.VMEM((rows, d), F32),
                        pltpu.SemaphoreType.DMA((2,))],
    )
    return pl.pallas_call(
        _expert_kernel,
        grid_spec=grid_spec,
        out_shape=jax.ShapeDtypeStruct((nb * rows, d), F32),
        compiler_params=_params(("arbitrary",), vmem_mib=58),
    )(block_expert, n_used, tok3, tok3, row_w.reshape(nb * rows, 1), u2, wg, wu, wd)


def _combine_kernel(pos_ref, posn_ref, ys_hbm, x_ref, g_ref, ng_ref, o_ref, buf_a, buf_b, sem):
    i = pl.program_id(0)
    last = pl.num_programs(0) - 1
    tc = x_ref.shape[0]

    def gather(idx_ref, buf, s, unroll):
        _gather_rows(idx_ref, tc, ys_hbm, buf.at[0], s, 0, unroll)
        _gather_rows(idx_ref, tc, ys_hbm, buf.at[1], s, tc, unroll)

    def wait(buf, s):
        _wait_rows(tc, ys_hbm, buf.at[0], s)
        _wait_rows(tc, ys_hbm, buf.at[1], s)

    @pl.when(i == 0)
    def _():
        gather(pos_ref, buf_a, sem.at[0], False)

    def step(cur, cur_sem, nxt, nxt_sem):
        wait(cur, cur_sem)
        gather(posn_ref, nxt, nxt_sem, True)
        x = x_ref[...] + g_ref[0] * (cur[0] + cur[1])
        y = x * lax.rsqrt(jnp.mean(x * x, axis=-1, keepdims=True) + NORM_EPS)
        o_ref[...] = y * ng_ref[...]

        @pl.when(i == last)
        def _():
            wait(nxt, nxt_sem)

    @pl.when(i % 2 == 0)
    def _():
        step(buf_a, sem.at[0], buf_b, sem.at[1])

    @pl.when(i % 2 == 1)
    def _():
        step(buf_b, sem.at[1], buf_a, sem.at[0])


def _combine(pos, ys, x1, gate, norm_g, seq):
    t, d = x1.shape
    tc = _pick_tile(seq, (128,))
    n = t // tc
    per_b = seq // tc
    pos3 = pos.reshape(n, tc, 2).transpose(0, 2, 1).reshape(n, 1, 2 * tc)
    return pl.pallas_call(
        _combine_kernel,
        grid=(n,),
        in_specs=[pl.BlockSpec((1, 1, 2 * tc), lambda i: (i, 0, 0), memory_space=pltpu.SMEM),
                  pl.BlockSpec((1, 1, 2 * tc), lambda i: (jnp.minimum(i + 1, n - 1), 0, 0),
                               memory_space=pltpu.SMEM),
                  pl.BlockSpec(memory_space=pl.ANY),
                  pl.BlockSpec((tc, d), lambda i: (i, 0)),
                  pl.BlockSpec((1, 1, d), lambda i: (i // per_b, 0, 0)),
                  pl.BlockSpec((1, d), lambda i: (0, 0))],
        out_specs=pl.BlockSpec((tc, d), lambda i: (i, 0)),
        out_shape=jax.ShapeDtypeStruct((t, d), F32),
        scratch_shapes=[pltpu.VMEM((2, tc, d), F32), pltpu.VMEM((2, tc, d), F32),
                        pltpu.SemaphoreType.DMA((2,))],
        compiler_params=_params(("arbitrary",)),
    )(pos3, pos3, ys, x1, gate[:, None, :], norm_g.reshape(1, d))


def _dispatch(expert_id, weight, n_tok):
    m = expert_id.shape[0]
    rows = MOE_ROWS
    nb = -(-m // rows) + N_EXPERTS
    onehot = (expert_id[:, None] == jnp.arange(N_EXPERTS, dtype=jnp.int32)[None, :]).astype(jnp.int32)
    csum = jnp.cumsum(onehot, axis=0)
    rank = jnp.take_along_axis(csum, expert_id[:, None], axis=1)[:, 0] - 1
    counts = csum[-1]
    padded = ((counts + rows - 1) // rows) * rows
    pends = jnp.cumsum(padded)
    pstarts = pends - padded
    dest = (pstarts[expert_id] + rank).astype(jnp.int32)
    tok = jnp.arange(m, dtype=jnp.int32) // (m // n_tok)
    row_tok = jnp.zeros((nb * rows,), jnp.int32).at[dest].set(tok)
    row_w = jnp.zeros((nb * rows,), F32).at[dest].set(weight)
    n_used = (pends[-1] // rows).astype(jnp.int32)
    blk = jnp.arange(nb, dtype=jnp.int32)
    blk = jnp.minimum(blk, n_used - 1)
    block_expert = jnp.minimum(jnp.searchsorted(pends, blk * rows, side='right'),
                               N_EXPERTS - 1).astype(jnp.int32)
    return block_expert, n_used.reshape(1), row_tok, row_w, dest


def _layer(x, c, w_cond, b_cond, norm1_g, w_in, rwkv_mu, rwkv_w0, rwkv_w_up, rwkv_a0, rwkv_a_up,
           rwkv_g_up, rwkv_k_k, rwkv_k_a, rwkv_r_k, rwkv_lnx_w, rwkv_lnx_b, attn_sinks, attn_out_g,
           w_out, norm2_g, router_group, router_group_bias, router_expert, router_expert_bias,
           expert_w_gate, expert_w_up, expert_w_down):
    b, s, d = x.shape
    t = b * s
    d_rwkv = rwkv_w0.shape[-1]
    d_attn = attn_out_g.shape[-1]
    dw, da, dg = rwkv_w_up.shape[0], rwkv_a_up.shape[0], rwkv_g_up.shape[0]
    n_lora = dw + da + dg
    lp = _round_up(n_lora, LANES)
    while (3 * d_rwkv) % lp:
        lp += LANES
    n_rwkv_cols = 3 * d_rwkv + n_lora

    c_pad = jnp.zeros((_round_up(b, 8), d), F32).at[:b].set(c)
    mod = _cond(c_pad, w_cond, b_cond.reshape(1, -1))[:b]
    sh1, sc1, g1, sh2, sc2, g2 = jnp.split(mod, 6, axis=-1)

    x2d = x.reshape(t, d)
    u1 = _norm_mod(x2d, norm1_g, sc1, sh1, s, BF16)

    zpad = jnp.zeros((d, lp - n_lora), F32)
    w_r = jnp.concatenate([w_in[:, :n_rwkv_cols], zpad], axis=1).astype(BF16)
    w_a = w_in[:, n_rwkv_cols:].astype(BF16)
    proj_r = _matmul(u1, w_r, F32)
    proj_a = _matmul(u1, w_a, BF16)

    def lora_rows(wmat, off):
        return jnp.zeros((lp, d_rwkv), F32).at[off:off + wmat.shape[0]].set(wmat).astype(BF16)

    mu = jnp.concatenate([rwkv_mu, jnp.zeros((lp - n_lora,), F32)]).reshape(1, -1)
    row = lambda v: v.reshape(1, d_rwkv)
    y_rwkv = _rwkv(proj_r.reshape(b, s, -1), mu, row(rwkv_w0), row(rwkv_a0), row(rwkv_k_k),
                   row(rwkv_k_a), row(rwkv_r_k), row(rwkv_lnx_w), row(rwkv_lnx_b),
                   lora_rows(rwkv_w_up, 0), lora_rows(rwkv_a_up, dw), lora_rows(rwkv_g_up, dw + da),
                   d_rwkv=d_rwkv, lp=lp, dw=dw, da=da).reshape(t, d_rwkv)

    y_attn = _attention(proj_a, attn_sinks, attn_out_g, batch=b, seq=s, d_attn=d_attn)

    x1 = _outproj(y_rwkv, y_attn, w_out.astype(BF16), x2d, g1, s)

    u2 = _norm_mod(x1, norm2_g, sc2, sh2, s, F32)
    n_r = N_GROUPS + N_EXPERTS
    w_router = jnp.concatenate([router_group, router_expert, jnp.zeros((d, LANES - n_r), F32)],
                               axis=1).astype(BF16)
    b_router = jnp.concatenate([router_group_bias, router_expert_bias,
                                jnp.zeros((LANES - n_r,), F32)]).reshape(1, LANES)
    route = _router(u2, w_router, b_router)
    expert_id = route[:, :2].astype(jnp.int32).reshape(-1)
    weight = route[:, 2:4].reshape(-1)
    block_expert, n_used, row_tok, row_w, dest = _dispatch(expert_id, weight, t)
    ys = _experts(block_expert, n_used, row_tok, row_w, u2, expert_w_gate.astype(BF16),
                  expert_w_up.astype(BF16), expert_w_down.astype(BF16))
    return ys, dest.reshape(t, 2), x1, g2


def kernel(x, c, w_cond, b_cond, norm1_g, w_in, rwkv_mu, rwkv_w0, rwkv_w_up, rwkv_a0, rwkv_a_up, rwkv_g_up, rwkv_k_k, rwkv_k_a, rwkv_r_k, rwkv_lnx_w, rwkv_lnx_b, attn_sinks, attn_out_g, w_out, norm2_g, router_group, router_group_bias, router_expert, router_expert_bias, expert_w_gate, expert_w_up, expert_w_down, norm_f_g):
    b, s, d = x.shape
    depth = w_cond.shape[0]
    assert depth == 1, "the fused final norm assumes a single layer"
    l = 0
    ys, pos, x1, g2 = _layer(
        x, c, w_cond[l], b_cond[l], norm1_g[l], w_in[l], rwkv_mu[l], rwkv_w0[l], rwkv_w_up[l],
        rwkv_a0[l], rwkv_a_up[l], rwkv_g_up[l], rwkv_k_k[l], rwkv_k_a[l], rwkv_r_k[l],
        rwkv_lnx_w[l], rwkv_lnx_b[l], attn_sinks[l], attn_out_g[l], w_out[l], norm2_g[l],
        router_group[l], router_group_bias[l], router_expert[l], router_expert_bias[l],
        expert_w_gate[l], expert_w_up[l], expert_w_down[l])
    out = _combine(pos, ys, x1, g2, norm_f_g, s)
    return out.reshape(b, s, d)
```

```python
import functools
import math

import jax
import jax.numpy as jnp
from jax import lax
from jax.experimental import pallas as pl
from jax.experimental.pallas import tpu as pltpu

F32 = jnp.float32
BF16 = jnp.bfloat16

HEAD_DIM = 64
N_KV_HEADS = 8
WINDOW = 128
N_GROUPS = 8
EXPERTS_PER_GROUP = 8
N_EXPERTS = N_GROUPS * EXPERTS_PER_GROUP
RWKV_GN_EPS = 64e-5
NORM_EPS = 1e-6
RWKV_CHUNK = 64
RWKV_HEADS_PER_STEP = 4
RWKV_GROUPS_PER_STEP = 4
MOE_ROWS = 256
LANES = 128
MIB = 1024 * 1024


def _params(sem, vmem_mib=48):
    return pltpu.CompilerParams(dimension_semantics=sem, vmem_limit_bytes=vmem_mib * MIB)


def _round_up(n, m):
    return -(-n // m) * m


def _pick_tile(n, candidates):
    for t in candidates:
        if n % t == 0:
            return t
    raise ValueError(f"no tile in {candidates} divides {n}")


def _cond_kernel(c_ref, w_ref, b_ref, o_ref):
    c = c_ref[...]
    sc = (c * jax.nn.sigmoid(c)).astype(BF16)
    o_ref[...] = jnp.dot(sc, w_ref[...].astype(BF16), preferred_element_type=F32) + b_ref[...]


def _cond(c_pad, w, b):
    rows, d = c_pad.shape
    n = w.shape[1]
    tn = _pick_tile(n, (1024, 512, 256, 128))
    return pl.pallas_call(
        _cond_kernel,
        grid=(n // tn,),
        in_specs=[pl.BlockSpec((rows, d), lambda j: (0, 0)),
                  pl.BlockSpec((d, tn), lambda j: (0, j)),
                  pl.BlockSpec((1, tn), lambda j: (0, j))],
        out_specs=pl.BlockSpec((rows, tn), lambda j: (0, j)),
        out_shape=jax.ShapeDtypeStruct((rows, n), F32),
        compiler_params=_params(("arbitrary",)),
    )(c_pad, w, b)


def _norm_mod_kernel(x_ref, g_ref, sc_ref, sh_ref, o_ref):
    x = x_ref[...]
    y = x * lax.rsqrt(jnp.mean(x * x, axis=-1, keepdims=True) + NORM_EPS)
    y = y * g_ref[...]
    o_ref[...] = (y * (1.0 + sc_ref[0]) + sh_ref[0]).astype(o_ref.dtype)


def _norm_mod(x2d, g, scale, shift, seq, out_dtype):
    t, d = x2d.shape
    tm = _pick_tile(seq, (512, 256, 128))
    per_b = seq // tm
    return pl.pallas_call(
        _norm_mod_kernel,
        grid=(t // tm,),
        in_specs=[pl.BlockSpec((tm, d), lambda i: (i, 0)),
                  pl.BlockSpec((1, d), lambda i: (0, 0)),
                  pl.BlockSpec((1, 1, d), lambda i: (i // per_b, 0, 0)),
                  pl.BlockSpec((1, 1, d), lambda i: (i // per_b, 0, 0))],
        out_specs=pl.BlockSpec((tm, d), lambda i: (i, 0)),
        out_shape=jax.ShapeDtypeStruct((t, d), out_dtype),
        compiler_params=_params(("arbitrary",)),
    )(x2d, g.reshape(1, d), scale[:, None, :], shift[:, None, :])


def _mm_kernel(a_ref, b_ref, o_ref):
    o_ref[...] = jnp.dot(a_ref[...], b_ref[...], preferred_element_type=F32).astype(o_ref.dtype)


def _matmul(a, b, out_dtype):
    m, k = a.shape
    n = b.shape[1]
    tm = _pick_tile(m, (1024, 512, 256, 128))
    tn = _pick_tile(n, (512, 256, 128))
    return pl.pallas_call(
        _mm_kernel,
        grid=(m // tm, n // tn),
        in_specs=[pl.BlockSpec((tm, k), lambda i, j: (i, 0)),
                  pl.BlockSpec((k, tn), lambda i, j: (0, j))],
        out_specs=pl.BlockSpec((tm, tn), lambda i, j: (i, j)),
        out_shape=jax.ShapeDtypeStruct((m, n), out_dtype),
        compiler_params=_params(("arbitrary", "arbitrary")),
    )(a, b)


def _softplus(z):
    return jnp.maximum(z, 0.0) + jnp.log(1.0 + jnp.exp(-jnp.abs(z)))


def _rwkv_kernel(r_ref, k_ref, v_ref, l_ref, mur_ref, muk_ref, muv_ref, mul_ref,
                 w0_ref, a0_ref, kk_ref, ka_ref, rk_ref, lw_ref, lb_ref,
                 ww_ref, wa_ref, wg_ref, o_ref,
                 s_ref, pr_ref, pk_ref, pv_ref, pl_ref, *, n_batch, dw, da):
    c = RWKV_CHUNK
    nh = RWKV_HEADS_PER_STEP
    hw = nh * HEAD_DIM
    groups = r_ref.shape[-1] // hw
    lp = l_ref.shape[-1]
    chunk = pl.program_id(1)

    @pl.when(chunk == 0)
    def _():
        s_ref[...] = jnp.zeros_like(s_ref)
        pr_ref[...] = jnp.zeros_like(pr_ref)
        pk_ref[...] = jnp.zeros_like(pk_ref)
        pv_ref[...] = jnp.zeros_like(pv_ref)
        pl_ref[...] = jnp.zeros_like(pl_ref)

    row_w = lax.broadcasted_iota(jnp.int32, (c, hw), 0)
    col_w = lax.broadcasted_iota(jnp.int32, (c, hw), 1)
    s_idx = col_w % c
    strict = row_w > s_idx
    incl = row_w >= s_idx
    eye_w = (row_w == s_idx).astype(F32)
    merge_masks = [strict & (row_w // 2 == s_idx // 2)]
    size = 2
    while size < c:
        merge_masks.append(strict & (row_w // (2 * size) == s_idx // (2 * size))
                           & (row_w // size != s_idx // size))
        size *= 2
    bd_row = lax.broadcasted_iota(jnp.int32, (hw, hw), 0) // HEAD_DIM
    bd_col = lax.broadcasted_iota(jnp.int32, (hw, hw), 1) // HEAD_DIM
    bd_mask = bd_row == bd_col
    bd_ones = bd_mask.astype(BF16)
    tri = (lax.broadcasted_iota(jnp.int32, (c, c), 0)
           >= lax.broadcasted_iota(jnp.int32, (c, c), 1)).astype(BF16)
    row_l = lax.broadcasted_iota(jnp.int32, (c, lp), 0)
    col_l = lax.broadcasted_iota(jnp.int32, (c, lp), 1)

    def shift_mix(x, prev, mu, row):
        shifted = jnp.where(row == 0, prev, pltpu.roll(x, 1, axis=0))
        return x + (shifted - x) * mu

    def split2(x):
        hi = x.astype(BF16)
        lo = (x - hi.astype(F32)).astype(BF16)
        return hi, lo

    chains = [(bi, gi) for gi in range(groups) for bi in range(n_batch)]
    nb = range(len(chains))

    def cols(x, gi):
        return x[..., gi * hw:(gi + 1) * hw]

    def vec(ref, ci):
        return cols(ref[...], chains[ci][1])

    def rows_split(x):
        return [x[i * c:(i + 1) * c] for i in range(x.shape[0] // c)]

    def chain_split(x):
        return [cols(x[bi * c:(bi + 1) * c], gi) for bi, gi in chains]

    def seg_sum(xs):
        parts = [split2(x) for x in xs]
        stacked = jnp.concatenate([p[0] for p in parts] + [p[1] for p in parts], axis=0)
        both = jnp.dot(stacked, bd_ones, preferred_element_type=F32)
        half = len(xs) * c
        return rows_split(both[:half] + both[half:])

    def bd(x):
        return jnp.where(bd_mask, jnp.tile(x, (nh, 1)), 0.0).astype(BF16)

    def mm(a, b):
        return jnp.dot(a.astype(BF16), b, preferred_element_type=F32)

    def mm_nt(a, b):
        return lax.dot_general(a.astype(BF16), b, (((1,), (1,)), ((), ())),
                               preferred_element_type=F32)

    def mixed(x_ref, p_ref, mu_ref):
        return [shift_mix(cols(x_ref[bi], gi), cols(p_ref[bi], gi), cols(mu_ref[...], gi), row_w)
                for bi, gi in chains]

    xr = mixed(r_ref, pr_ref, mur_ref)
    xk = mixed(k_ref, pk_ref, muk_ref)
    xv = mixed(v_ref, pv_ref, muv_ref)
    xl = [shift_mix(l_ref[bi], pl_ref[bi], mul_ref[...], row_l) for bi in range(n_batch)]
    states = [s_ref[bi, gi] for bi, gi in chains]

    act = jnp.concatenate(
        [jnp.where(col_l < dw, jnp.tanh(x), jnp.where(col_l < dw + da, x, jax.nn.sigmoid(x)))
         for x in xl], axis=0).astype(BF16)
    kw = _round_up(dw, LANES)
    ka = _round_up(dw + da, LANES)
    lora_w = chain_split(jnp.dot(act[:, :kw], ww_ref[:kw, :], preferred_element_type=F32))
    lora_a = chain_split(jnp.dot(act[:, :ka], wa_ref[:ka, :], preferred_element_type=F32))
    gate = chain_split(jnp.dot(act, wg_ref[...], preferred_element_type=F32))

    e = [jnp.exp(-_softplus(-(vec(w0_ref, ci) + lora_w[ci])) - 0.5) for ci in nb]
    lr = [jax.nn.sigmoid(vec(a0_ref, ci) + lora_a[ci]) for ci in nb]
    kk = [xk[ci] * vec(kk_ref, ci) for ci in nb]
    kk_ss = seg_sum([x * x for x in kk])
    kk = [x / jnp.maximum(jnp.sqrt(ss), 1e-12) for x, ss in zip(kk, kk_ss)]
    kmod = [xk[ci] * (1.0 + (lr[ci] - 1.0) * vec(ka_ref, ci)) for ci in nb]

    cum = []
    for ei in e:
        e_hi, e_lo = split2(ei)
        cum2 = jnp.dot(tri, jnp.concatenate([e_hi, e_lo], axis=1), preferred_element_type=F32)
        cum.append(cum2[:, :hw] + cum2[:, hw:])
    dec = [jnp.exp(-x) for x in cum]
    inv = [jnp.exp(x) for x in cum]
    at = [-kk[bi] * jnp.exp(e[bi] - cum[bi]) for bi in nb]
    rt = [xr[bi] * dec[bi] for bi in nb]
    bt = [kk[bi] * lr[bi] * inv[bi] for bi in nb]
    kt = [kmod[bi] * inv[bi] for bi in nb]

    ar = [jnp.concatenate([at[bi], rt[bi]], axis=0).astype(BF16) for bi in nb]
    g = [mm_nt(ar[bi], jnp.concatenate([bd(bt[bi]), bd(kt[bi])], axis=0)) for bi in nb]
    m_ab = [jnp.where(strict, x[:c, :hw], 0.0) for x in g]
    m_ak = [jnp.where(strict, x[:c, hw:], 0.0) for x in g]
    m_rb = [jnp.where(incl, x[c:, :hw], 0.0) for x in g]
    m_rk = [jnp.where(incl, x[c:, hw:], 0.0) for x in g]

    t_inv = [eye_w + jnp.where(merge_masks[0], m, 0.0) for m in m_ab]
    for mask in merge_masks[1:]:
        z = [mm(jnp.where(mask, m_ab[bi], 0.0), bd(t_inv[bi])) for bi in nb]
        t_inv = [t_inv[bi] + mm(t_inv[bi], bd(z[bi])) for bi in nb]

    s_bf = [s.astype(BF16) for s in states]
    bd_v = [bd(x) for x in xv]
    ar_s = [mm_nt(ar[bi], s_bf[bi]) for bi in nb]
    mv = [mm(jnp.concatenate([m_ak[bi], m_rk[bi]], axis=0), bd_v[bi]) for bi in nb]
    u = [mm(t_inv[bi], bd(ar_s[bi][:c] + mv[bi][:c])) for bi in nb]
    y = [ar_s[bi][c:] + mm(m_rb[bi], bd(u[bi])) + mv[bi][c:] for bi in nb]
    upd = [lax.dot_general(jnp.concatenate([u[bi], xv[bi]], axis=0).astype(BF16),
                           jnp.concatenate([bt[bi], kt[bi]], axis=0).astype(BF16),
                           (((0,), (0,)), ((), ())), preferred_element_type=F32) for bi in nb]

    mean = [m * (1.0 / HEAD_DIM) for m in seg_sum(y)]
    dy = [y[bi] - mean[bi] for bi in nb]
    var = [v * (1.0 / HEAD_DIM) for v in seg_sum([d * d for d in dy])]
    rk_sum = seg_sum([xr[ci] * kmod[ci] * vec(rk_ref, ci) for ci in nb])

    for ci, (bi, gi) in enumerate(chains):
        yn = dy[ci] * lax.rsqrt(var[ci] + RWKV_GN_EPS) * vec(lw_ref, ci) + vec(lb_ref, ci)
        o_ref[bi, :, gi * hw:(gi + 1) * hw] = ((yn + rk_sum[ci] * xv[ci]) * gate[ci]).astype(o_ref.dtype)
        s_ref[bi, gi] = (states[ci] + jnp.where(bd_mask, upd[ci], 0.0)) * dec[ci][c - 1:c]
    for bi in range(n_batch):
        pr_ref[bi] = r_ref[bi, c - 1:c, :]
        pk_ref[bi] = k_ref[bi, c - 1:c, :]
        pv_ref[bi] = v_ref[bi, c - 1:c, :]
        pl_ref[bi] = l_ref[bi, c - 1:c, :]


def _rwkv(proj, mu, w0, a0, k_k, k_a, r_k, lnx_w, lnx_b, ww, wa, wg, *, d_rwkv, lp, dw, da):
    b, s, _ = proj.shape
    c = RWKV_CHUNK
    hw1 = RWKV_HEADS_PER_STEP * HEAD_DIM
    groups = RWKV_GROUPS_PER_STEP if d_rwkv % (RWKV_GROUPS_PER_STEP * hw1) == 0 else 1
    hw = groups * hw1
    assert c == HEAD_DIM and d_rwkv % hw == 0 and s % c == 0 and (3 * d_rwkv) % lp == 0
    nhg = d_rwkv // hw
    lblk = (3 * d_rwkv) // lp

    def act_spec(off):
        return pl.BlockSpec((b, c, hw), lambda h, t: (0, t, off + h))

    def vec_spec(off):
        return pl.BlockSpec((1, hw), lambda h, t: (0, off + h))

    vec = vec_spec(0)
    lora_spec = pl.BlockSpec((lp, hw), lambda h, t: (0, h))
    kern = functools.partial(_rwkv_kernel, n_batch=b, dw=dw, da=da)
    return pl.pallas_call(
        kern,
        grid=(nhg, s // c),
        in_specs=[act_spec(0), act_spec(nhg), act_spec(2 * nhg),
                  pl.BlockSpec((b, c, lp), lambda h, t: (0, t, lblk)),
                  vec_spec(0), vec_spec(nhg), vec_spec(2 * nhg),
                  pl.BlockSpec((1, lp), lambda h, t: (0, lblk)),
                  vec, vec, vec, vec, vec, vec, vec,
                  lora_spec, lora_spec, lora_spec],
        out_specs=pl.BlockSpec((b, c, hw), lambda h, t: (0, t, h)),
        out_shape=jax.ShapeDtypeStruct((b, s, d_rwkv), BF16),
        scratch_shapes=[pltpu.VMEM((b, groups, hw1, hw1), F32),
                        pltpu.VMEM((b, 1, hw), F32), pltpu.VMEM((b, 1, hw), F32),
                        pltpu.VMEM((b, 1, hw), F32), pltpu.VMEM((b, 1, lp), F32)],
        compiler_params=_params(("arbitrary", "arbitrary")),
    )(proj, proj, proj, proj, mu, mu, mu, mu,
      w0, a0, k_k, k_a, r_k, lnx_w, lnx_b, ww, wa, wg)


def _attn_kernel(sink_ref, slope_ref, q_ref, kp_ref, kc_ref, vp_ref, vc_ref, og_ref, o_ref, *, gqa):
    blk = pl.program_id(2)
    pair = pl.program_id(1)
    wd = WINDOW
    qi = lax.broadcasted_iota(jnp.int32, (wd, 2 * wd), 0)
    kj = lax.broadcasted_iota(jnp.int32, (wd, 2 * wd), 1)
    dist = qi + wd - kj
    valid = (dist >= 0) & (dist < wd) & ((kj >= wd) | (blk > 0))
    dist_f = dist.astype(F32)
    scale = HEAD_DIM ** -0.5
    k2 = jnp.concatenate([kp_ref[...], kc_ref[...]], axis=0)
    v2 = jnp.concatenate([vp_ref[...], vc_ref[...]], axis=0)
    heads = range(2 * gqa)
    ks = [k2[:, kvh * HEAD_DIM:(kvh + 1) * HEAD_DIM] for kvh in range(2)]
    vs = [v2[:, kvh * HEAD_DIM:(kvh + 1) * HEAD_DIM] for kvh in range(2)]
    slopes = [slope_ref[pair * (2 * gqa) + hl] for hl in heads]
    sinks = [sink_ref[pair * (2 * gqa) + hl] for hl in heads]
    s = [lax.dot_general(q_ref[:, hl * HEAD_DIM:(hl + 1) * HEAD_DIM], ks[hl // gqa],
                         (((1,), (1,)), ((), ())), preferred_element_type=F32) for hl in heads]
    s = [jnp.where(valid, s[hl] * scale - slopes[hl] * dist_f, -jnp.inf) for hl in heads]
    m = [jnp.maximum(jnp.max(s[hl], axis=-1, keepdims=True), sinks[hl]) for hl in heads]
    p = [jnp.exp(s[hl] - m[hl]) for hl in heads]
    denom = [jnp.sum(p[hl], axis=-1, keepdims=True) + jnp.exp(sinks[hl] - m[hl]) for hl in heads]
    o = [jnp.dot(p[hl].astype(BF16), vs[hl // gqa], preferred_element_type=F32) / denom[hl]
         for hl in heads]
    o = [x * lax.rsqrt(jnp.mean(x * x, axis=-1, keepdims=True) + NORM_EPS) for x in o]
    o_all = jnp.concatenate(o, axis=-1) * og_ref[...]
    o_ref[...] = o_all.astype(o_ref.dtype)


def _attention(qkv, sinks, out_g, *, batch, seq, d_attn):
    d_kv = N_KV_HEADS * HEAD_DIM
    n_q_heads = d_attn // HEAD_DIM
    gqa = n_q_heads // N_KV_HEADS
    qw = 2 * gqa * HEAD_DIM
    n_pairs = N_KV_HEADS // 2
    nb = seq // WINDOW
    assert qw % LANES == 0 and d_attn % LANES == 0
    k_off = d_attn // LANES
    v_off = (d_attn + d_kv) // LANES

    def cur(off):
        return pl.BlockSpec((WINDOW, LANES), lambda b, p, i, *_: (b * nb + i, off + p))

    def prev(off):
        return pl.BlockSpec((WINDOW, LANES),
                            lambda b, p, i, *_: (b * nb + jnp.maximum(i - 1, 0), off + p))

    slopes = jnp.exp2(-8.0 * jnp.arange(1, n_q_heads + 1, dtype=F32) / n_q_heads)
    grid_spec = pltpu.PrefetchScalarGridSpec(
        num_scalar_prefetch=2,
        grid=(batch, n_pairs, nb),
        in_specs=[pl.BlockSpec((WINDOW, qw), lambda b, p, i, *_: (b * nb + i, p)),
                  prev(k_off), cur(k_off), prev(v_off), cur(v_off),
                  pl.BlockSpec((1, qw), lambda b, p, i, *_: (0, p))],
        out_specs=pl.BlockSpec((WINDOW, qw), lambda b, p, i, *_: (b * nb + i, p)),
    )
    return pl.pallas_call(
        functools.partial(_attn_kernel, gqa=gqa),
        grid_spec=grid_spec,
        out_shape=jax.ShapeDtypeStruct((batch * seq, d_attn), BF16),
        compiler_params=_params(("arbitrary", "arbitrary", "arbitrary")),
    )(sinks, slopes, qkv, qkv, qkv, qkv, qkv, out_g.reshape(1, d_attn))


def _outproj_kernel(yr_ref, ya_ref, w1_ref, w2_ref, x_ref, g_ref, o_ref):
    acc = jnp.dot(yr_ref[...], w1_ref[...], preferred_element_type=F32)
    acc = acc + jnp.dot(ya_ref[...], w2_ref[...], preferred_element_type=F32)
    o_ref[...] = x_ref[...] + g_ref[0] * acc


def _outproj(y_rwkv, y_attn, w_out, x2d, gate, seq):
    t, d = x2d.shape
    k1 = y_rwkv.shape[1]
    k2 = y_attn.shape[1]
    assert k1 == k2
    tm = _pick_tile(seq, (1024, 512, 256, 128))
    tn = _pick_tile(d, (512, 256, 128))
    per_b = seq // tm
    return pl.pallas_call(
        _outproj_kernel,
        grid=(t // tm, d // tn),
        in_specs=[pl.BlockSpec((tm, k1), lambda i, j: (i, 0)),
                  pl.BlockSpec((tm, k2), lambda i, j: (i, 0)),
                  pl.BlockSpec((k1, tn), lambda i, j: (0, j)),
                  pl.BlockSpec((k2, tn), lambda i, j: (1, j)),
                  pl.BlockSpec((tm, tn), lambda i, j: (i, j)),
                  pl.BlockSpec((1, 1, tn), lambda i, j: (i // per_b, 0, j))],
        out_specs=pl.BlockSpec((tm, tn), lambda i, j: (i, j)),
        out_shape=jax.ShapeDtypeStruct((t, d), F32),
        compiler_params=_params(("arbitrary", "arbitrary")),
    )(y_rwkv, y_attn, w_out, w_out, x2d, gate[:, None, :])


def _slab_pitch(ns):
    return ns + 8 if ns % 16 == 0 else _round_up(ns, 8)


def _to_slabs(ref, x):
    rows, d = x.shape
    ns = d // LANES
    pitch = _slab_pitch(ns)
    ref[...] = jnp.zeros_like(ref)
    for s in range(ns):
        ref[pl.ds(s, rows, stride=pitch), :] = x[:, s * LANES:(s + 1) * LANES]


def _from_slabs(ref, ns):
    pitch = _slab_pitch(ns)
    rows = ref.shape[0] // pitch
    return jnp.concatenate([ref[pl.ds(s, rows, stride=pitch), :] for s in range(ns)], axis=-1)


def _norm_route_kernel(x_ref, g_ref, sc_ref, sh_ref, w_ref, b_ref, u_ref, o_ref):
    x = x_ref[...]
    y = x * lax.rsqrt(jnp.mean(x * x, axis=-1, keepdims=True) + NORM_EPS)
    y = y * g_ref[...]
    y = y * (1.0 + sc_ref[0]) + sh_ref[0]
    _to_slabs(u_ref, y)
    logits = jnp.dot(y.astype(BF16), w_ref[...], preferred_element_type=F32) + b_ref[...]
    col = lax.broadcasted_iota(jnp.int32, logits.shape, 1).astype(F32)
    ng = float(N_GROUPS)
    epg = float(EXPERTS_PER_GROUP)
    big = 1e9
    is_g = col < ng
    lg = jnp.where(is_g, logits, -jnp.inf)
    mg = jnp.max(lg, axis=-1, keepdims=True)
    gidx = jnp.min(jnp.where(lg == mg, col, big), axis=-1, keepdims=True)
    zg = jnp.sum(jnp.where(is_g, jnp.exp(lg - mg), 0.0), axis=-1, keepdims=True)
    g_w = 1.0 / zg
    lo = ng + gidx * epg
    in_grp = (col >= lo) & (col < lo + epg)
    le = jnp.where(in_grp, logits, -jnp.inf)
    m1 = jnp.max(le, axis=-1, keepdims=True)
    i1 = jnp.min(jnp.where(le == m1, col, big), axis=-1, keepdims=True)
    le2 = jnp.where(col == i1, -jnp.inf, le)
    m2 = jnp.max(le2, axis=-1, keepdims=True)
    i2 = jnp.min(jnp.where(le2 == m2, col, big), axis=-1, keepdims=True)
    e2 = jnp.exp(m2 - m1)
    w1 = 1.0 / (1.0 + e2)
    w2 = e2 / (1.0 + e2)
    out = jnp.where(col == 0.0, i1 - ng,
                    jnp.where(col == 1.0, i2 - ng,
                              jnp.where(col == 2.0, g_w * w1,
                                        jnp.where(col == 3.0, g_w * w2, 0.0))))
    o_ref[...] = out


def _norm_route(x2d, g, scale, shift, w_r, b_r, seq):
    t, d = x2d.shape
    tm = _pick_tile(seq, (256, 128))
    per_b = seq // tm
    ns = d // LANES
    return pl.pallas_call(
        _norm_route_kernel,
        grid=(t // tm,),
        in_specs=[pl.BlockSpec((tm, d), lambda i: (i, 0)),
                  pl.BlockSpec((1, d), lambda i: (0, 0)),
                  pl.BlockSpec((1, 1, d), lambda i: (i // per_b, 0, 0)),
                  pl.BlockSpec((1, 1, d), lambda i: (i // per_b, 0, 0)),
                  pl.BlockSpec((d, LANES), lambda i: (0, 0)),
                  pl.BlockSpec((1, LANES), lambda i: (0, 0))],
        out_specs=[pl.BlockSpec((tm * _slab_pitch(ns), LANES), lambda i: (i, 0)),
                   pl.BlockSpec((tm, LANES), lambda i: (i, 0))],
        out_shape=[jax.ShapeDtypeStruct((t * _slab_pitch(ns), LANES), F32),
                   jax.ShapeDtypeStruct((t, LANES), F32)],
        compiler_params=_params(("arbitrary",)),
    )(x2d, g.reshape(1, d), scale[:, None, :], shift[:, None, :], w_r, b_r)


def _gather_rows(idx_ref, n_rows, src_hbm, dst, sem, col0=0, unroll=False, slab_rows=0):
    pitch = _slab_pitch(slab_rows) if slab_rows else 1
    size = slab_rows if slab_rows else 1

    def issue(r):
        tok = idx_ref[0, 0, col0 + r]
        if slab_rows:
            src = src_hbm.at[pl.ds(pl.multiple_of(tok * pitch, 8), size)]
            dst_r = dst.at[pl.ds(pl.multiple_of(r * pitch, 8), size)]
        else:
            src, dst_r = src_hbm.at[pl.ds(tok, 1)], dst.at[pl.ds(r, 1)]
        pltpu.make_async_copy(src, dst_r, sem).start()

    if unroll:
        for r in range(n_rows):
            issue(r)
    else:
        def body(r, carry):
            issue(r)
            return carry
        lax.fori_loop(0, n_rows, body, 0)


def _wait_rows(n_rows, src_hbm, dst, sem):
    pltpu.make_async_copy(src_hbm.at[pl.ds(0, n_rows)], dst.at[pl.ds(0, n_rows)], sem).wait()


def _expert_kernel(be_ref, nu_ref, tok_ref, tokn_ref, w_ref, u_hbm, wg_ref, wu_ref, wd_ref,
                   o_ref, xbuf_a, xbuf_b, sem):
    i = pl.program_id(0)
    last = pl.num_programs(0) - 1
    n_used = nu_ref[0]
    rows, d = o_ref.shape
    ns = d // LANES

    @pl.when(i == 0)
    def _():
        _gather_rows(tok_ref, rows, u_hbm, xbuf_a, sem.at[0], slab_rows=ns)

    def step(cur, cur_sem, nxt, nxt_sem):
        @pl.when(i <= n_used)
        def _():
            _wait_rows(rows * ns, u_hbm, cur, cur_sem)

        @pl.when(i < n_used)
        def _():
            _gather_rows(tokn_ref, rows, u_hbm, nxt, nxt_sem, unroll=True, slab_rows=ns)
            xb = _from_slabs(cur, ns).astype(BF16)
            gate = jnp.dot(xb, wg_ref[0], preferred_element_type=F32)
            up = jnp.dot(xb, wu_ref[0], preferred_element_type=F32)
            h = (gate * jax.nn.sigmoid(gate) * up).astype(BF16)
            y = jnp.dot(h, wd_ref[0], preferred_element_type=F32)
            o_ref[...] = y * w_ref[...]

        @pl.when(i >= n_used)
        def _():
            o_ref[...] = jnp.zeros_like(o_ref)

        @pl.when((i == last) & (i < n_used))
        def _():
            _wait_rows(rows * ns, u_hbm, nxt, nxt_sem)

    @pl.when(i % 2 == 0)
    def _():
        step(xbuf_a, sem.at[0], xbuf_b, sem.at[1])

    @pl.when(i % 2 == 1)
    def _():
        step(xbuf_b, sem.at[1], xbuf_a, sem.at[0])


def _experts(block_expert, n_used, row_tok, row_w, u2, wg, wu, wd):
    d = wg.shape[1]
    pitch = _slab_pitch(d // LANES)
    nb = block_expert.shape[0]
    rows = MOE_ROWS
    de = wg.shape[-1]
    tok3 = row_tok.reshape(nb, 1, rows)
    grid_spec = pltpu.PrefetchScalarGridSpec(
        num_scalar_prefetch=2,
        grid=(nb,),
        in_specs=[pl.BlockSpec((1, 1, rows), lambda i, be, nu: (i, 0, 0), memory_space=pltpu.SMEM),
                  pl.BlockSpec((1, 1, rows), lambda i, be, nu: (jnp.minimum(i + 1, nb - 1), 0, 0),
                               memory_space=pltpu.SMEM),
                  pl.BlockSpec((rows, 1), lambda i, be, nu: (i, 0)),
                  pl.BlockSpec(memory_space=pl.ANY),
                  pl.BlockSpec((1, d, de), lambda i, be, nu: (be[i], 0, 0)),
                  pl.BlockSpec((1, d, de), lambda i, be, nu: (be[i], 0, 0)),
                  pl.BlockSpec((1, de, d), lambda i, be, nu: (be[i], 0, 0))],
        out_specs=pl.BlockSpec((rows, d), lambda i, be, nu: (i, 0)),
        scratch_shapes=[pltpu.VMEM((rows * pitch, LANES), F32),
                        pltpu.VMEM((rows * pitch, LANES), F32),
                        pltpu.SemaphoreType.DMA((2,))],
    )
    return pl.pallas_call(
        _expert_kernel,
        grid_spec=grid_spec,
        out_shape=jax.ShapeDtypeStruct((nb * rows, d), F32),
        compiler_params=_params(("arbitrary",), vmem_mib=58),
    )(block_expert, n_used, tok3, tok3, row_w.reshape(nb * rows, 1), u2, wg, wu, wd)


def _combine_kernel(pos_ref, posn_ref, ys_hbm, x_ref, g_ref, ng_ref, o_ref, buf_a, buf_b, sem):
    i = pl.program_id(0)
    last = pl.num_programs(0) - 1
    tc = x_ref.shape[0]

    def gather(idx_ref, buf, s, unroll):
        _gather_rows(idx_ref, tc, ys_hbm, buf.at[0], s, 0, unroll)
        _gather_rows(idx_ref, tc, ys_hbm, buf.at[1], s, tc, unroll)

    def wait(buf, s):
        _wait_rows(tc, ys_hbm, buf.at[0], s)
        _wait_rows(tc, ys_hbm, buf.at[1], s)

    @pl.when(i == 0)
    def _():
        gather(pos_ref, buf_a, sem.at[0], False)

    def step(cur, cur_sem, nxt, nxt_sem):
        wait(cur, cur_sem)
        gather(posn_ref, nxt, nxt_sem, True)
        x = x_ref[...] + g_ref[0] * (cur[0] + cur[1])
        y = x * lax.rsqrt(jnp.mean(x * x, axis=-1, keepdims=True) + NORM_EPS)
        o_ref[...] = y * ng_ref[...]

        @pl.when(i == last)
        def _():
            wait(nxt, nxt_sem)

    @pl.when(i % 2 == 0)
    def _():
        step(buf_a, sem.at[0], buf_b, sem.at[1])

    @pl.when(i % 2 == 1)
    def _():
        step(buf_b, sem.at[1], buf_a, sem.at[0])


def _combine(pos, ys, x1, gate, norm_g, seq):
    t, d = x1.shape
    tc = _pick_tile(seq, (128,))
    n = t // tc
    per_b = seq // tc
    pos3 = pos.reshape(n, tc, 2).transpose(0, 2, 1).reshape(n, 1, 2 * tc)
    return pl.pallas_call(
        _combine_kernel,
        grid=(n,),
        in_specs=[pl.BlockSpec((1, 1, 2 * tc), lambda i: (i, 0, 0), memory_space=pltpu.SMEM),
                  pl.BlockSpec((1, 1, 2 * tc), lambda i: (jnp.minimum(i + 1, n - 1), 0, 0),
                               memory_space=pltpu.SMEM),
                  pl.BlockSpec(memory_space=pl.ANY),
                  pl.BlockSpec((tc, d), lambda i: (i, 0)),
                  pl.BlockSpec((1, 1, d), lambda i: (i // per_b, 0, 0)),
                  pl.BlockSpec((1, d), lambda i: (0, 0))],
        out_specs=pl.BlockSpec((tc, d), lambda i: (i, 0)),
        out_shape=jax.ShapeDtypeStruct((t, d), F32),
        scratch_shapes=[pltpu.VMEM((2, tc, d), F32), pltpu.VMEM((2, tc, d), F32),
                        pltpu.SemaphoreType.DMA((2,))],
        compiler_params=_params(("arbitrary",)),
    )(pos3, pos3, ys, x1, gate[:, None, :], norm_g.reshape(1, d))


def _dispatch(expert_id, weight, n_tok):
    m = expert_id.shape[0]
    rows = MOE_ROWS
    nb = -(-m // rows) + N_EXPERTS
    onehot = (expert_id[:, None] == jnp.arange(N_EXPERTS, dtype=jnp.int32)[None, :]).astype(jnp.int32)
    csum = jnp.cumsum(onehot, axis=0)
    rank = jnp.take_along_axis(csum, expert_id[:, None], axis=1)[:, 0] - 1
    counts = csum[-1]
    padded = ((counts + rows - 1) // rows) * rows
    pends = jnp.cumsum(padded)
    pstarts = pends - padded
    dest = (pstarts[expert_id] + rank).astype(jnp.int32)
    tok = jnp.arange(m, dtype=jnp.int32) // (m // n_tok)
    row_tok = jnp.zeros((nb * rows,), jnp.int32).at[dest].set(tok)
    row_w = jnp.zeros((nb * rows,), F32).at[dest].set(weight)
    n_used = (pends[-1] // rows).astype(jnp.int32)
    blk = jnp.arange(nb, dtype=jnp.int32)
    blk = jnp.minimum(blk, n_used - 1)
    block_expert = jnp.minimum(jnp.searchsorted(pends, blk * rows, side='right'),
                               N_EXPERTS - 1).astype(jnp.int32)
    return block_expert, n_used.reshape(1), row_tok, row_w, dest


def _layer(x, c, w_cond, b_cond, norm1_g, w_in, rwkv_mu, rwkv_w0, rwkv_w_up, rwkv_a0, rwkv_a_up,
           rwkv_g_up, rwkv_k_k, rwkv_k_a, rwkv_r_k, rwkv_lnx_w, rwkv_lnx_b, attn_sinks, attn_out_g,
           w_out, norm2_g, router_group, router_group_bias, router_expert, router_expert_bias,
           expert_w_gate, expert_w_up, expert_w_down):
    b, s, d = x.shape
    t = b * s
    d_rwkv = rwkv_w0.shape[-1]
    d_attn = attn_out_g.shape[-1]
    dw, da, dg = rwkv_w_up.shape[0], rwkv_a_up.shape[0], rwkv_g_up.shape[0]
    n_lora = dw + da + dg
    lp = _round_up(n_lora, LANES)
    while (3 * d_rwkv) % lp:
        lp += LANES
    n_rwkv_cols = 3 * d_rwkv + n_lora

    c_pad = jnp.zeros((_round_up(b, 8), d), F32).at[:b].set(c)
    mod = _cond(c_pad, w_cond, b_cond.reshape(1, -1))[:b]
    sh1, sc1, g1, sh2, sc2, g2 = jnp.split(mod, 6, axis=-1)

    x2d = x.reshape(t, d)
    u1 = _norm_mod(x2d, norm1_g, sc1, sh1, s, BF16)

    zpad = jnp.zeros((d, lp - n_lora), F32)
    w_r = jnp.concatenate([w_in[:, :n_rwkv_cols], zpad], axis=1).astype(BF16)
    w_a = w_in[:, n_rwkv_cols:].astype(BF16)
    proj_r = _matmul(u1, w_r, F32)
    proj_a = _matmul(u1, w_a, BF16)

    def lora_rows(wmat, off):
        return jnp.zeros((lp, d_rwkv), F32).at[off:off + wmat.shape[0]].set(wmat).astype(BF16)

    mu = jnp.concatenate([rwkv_mu, jnp.zeros((lp - n_lora,), F32)]).reshape(1, -1)
    row = lambda v: v.reshape(1, d_rwkv)
    y_rwkv = _rwkv(proj_r.reshape(b, s, -1), mu, row(rwkv_w0), row(rwkv_a0), row(rwkv_k_k),
                   row(rwkv_k_a), row(rwkv_r_k), row(rwkv_lnx_w), row(rwkv_lnx_b),
                   lora_rows(rwkv_w_up, 0), lora_rows(rwkv_a_up, dw), lora_rows(rwkv_g_up, dw + da),
                   d_rwkv=d_rwkv, lp=lp, dw=dw, da=da).reshape(t, d_rwkv)

    y_attn = _attention(proj_a, attn_sinks, attn_out_g, batch=b, seq=s, d_attn=d_attn)

    x1 = _outproj(y_rwkv, y_attn, w_out.astype(BF16), x2d, g1, s)

    n_r = N_GROUPS + N_EXPERTS
    w_router = jnp.concatenate([router_group, router_expert, jnp.zeros((d, LANES - n_r), F32)],
                               axis=1).astype(BF16)
    b_router = jnp.concatenate([router_group_bias, router_expert_bias,
                                jnp.zeros((LANES - n_r,), F32)]).reshape(1, LANES)
    u2, route = _norm_route(x1, norm2_g, sc2, sh2, w_router, b_router, s)
    expert_id = route[:, :2].astype(jnp.int32).reshape(-1)
    weight = route[:, 2:4].reshape(-1)
    block_expert, n_used, row_tok, row_w, dest = _dispatch(expert_id, weight, t)
    ys = _experts(block_expert, n_used, row_tok, row_w, u2, expert_w_gate.astype(BF16),
                  expert_w_up.astype(BF16), expert_w_down.astype(BF16))
    return ys, dest.reshape(t, 2), x1, g2


def kernel(x, c, w_cond, b_cond, norm1_g, w_in, rwkv_mu, rwkv_w0, rwkv_w_up, rwkv_a0, rwkv_a_up, rwkv_g_up, rwkv_k_k, rwkv_k_a, rwkv_r_k, rwkv_lnx_w, rwkv_lnx_b, attn_sinks, attn_out_g, w_out, norm2_g, router_group, router_group_bias, router_expert, router_expert_bias, expert_w_gate, expert_w_up, expert_w_down, norm_f_g):
    b, s, d = x.shape
    depth = w_cond.shape[0]
    assert depth == 1, "the fused final norm assumes a single layer"
    l = 0
    ys, pos, x1, g2 = _layer(
        x, c, w_cond[l], b_cond[l], norm1_g[l], w_in[l], rwkv_mu[l], rwkv_w0[l], rwkv_w_up[l],
        rwkv_a0[l], rwkv_a_up[l], rwkv_g_up[l], rwkv_k_k[l], rwkv_k_a[l], rwkv_r_k[l],
        rwkv_lnx_w[l], rwkv_lnx_b[l], attn_sinks[l], attn_out_g[l], w_out[l], norm2_g[l],
        router_group[l], router_group_bias[l], router_expert[l], router_expert_bias[l],
        expert_w_gate[l], expert_w_up[l], expert_w_down[l])
    out = _combine(pos, ys, x1, g2, norm_f_g, s)
    return out.reshape(b, s, d)
```

```python
import functools
import math

import jax
import jax.numpy as jnp
from jax import lax
from jax.experimental import pallas as pl
from jax.experimental.pallas import tpu as pltpu

F32 = jnp.float32
BF16 = jnp.bfloat16

HEAD_DIM = 64
N_KV_HEADS = 8
WINDOW = 128
N_GROUPS = 8
EXPERTS_PER_GROUP = 8
N_EXPERTS = N_GROUPS * EXPERTS_PER_GROUP
RWKV_GN_EPS = 64e-5
NORM_EPS = 1e-6
RWKV_CHUNK = 64
RWKV_HEADS_PER_STEP = 4
RWKV_GROUPS_PER_STEP = 4
MOE_ROWS = 256
LANES = 128
MIB = 1024 * 1024


def _params(sem, vmem_mib=48):
    return pltpu.CompilerParams(dimension_semantics=sem, vmem_limit_bytes=vmem_mib * MIB)


def _round_up(n, m):
    return -(-n // m) * m


def _pick_tile(n, candidates):
    for t in candidates:
        if n % t == 0:
            return t
    raise ValueError(f"no tile in {candidates} divides {n}")


def _cond_kernel(c_ref, w_ref, b_ref, o_ref):
    c = c_ref[...]
    sc = (c * jax.nn.sigmoid(c)).astype(BF16)
    o_ref[...] = jnp.dot(sc, w_ref[...].astype(BF16), preferred_element_type=F32) + b_ref[...]


def _cond(c_pad, w, b):
    rows, d = c_pad.shape
    n = w.shape[1]
    tn = _pick_tile(n, (1024, 512, 256, 128))
    return pl.pallas_call(
        _cond_kernel,
        grid=(n // tn,),
        in_specs=[pl.BlockSpec((rows, d), lambda j: (0, 0)),
                  pl.BlockSpec((d, tn), lambda j: (0, j)),
                  pl.BlockSpec((1, tn), lambda j: (0, j))],
        out_specs=pl.BlockSpec((rows, tn), lambda j: (0, j)),
        out_shape=jax.ShapeDtypeStruct((rows, n), F32),
        compiler_params=_params(("arbitrary",)),
    )(c_pad, w, b)


def _norm_mod_kernel(x_ref, g_ref, sc_ref, sh_ref, o_ref):
    x = x_ref[...]
    y = x * lax.rsqrt(jnp.mean(x * x, axis=-1, keepdims=True) + NORM_EPS)
    y = y * g_ref[...]
    o_ref[...] = (y * (1.0 + sc_ref[0]) + sh_ref[0]).astype(o_ref.dtype)


def _norm_mod(x2d, g, scale, shift, seq, out_dtype):
    t, d = x2d.shape
    tm = _pick_tile(seq, (512, 256, 128))
    per_b = seq // tm
    return pl.pallas_call(
        _norm_mod_kernel,
        grid=(t // tm,),
        in_specs=[pl.BlockSpec((tm, d), lambda i: (i, 0)),
                  pl.BlockSpec((1, d), lambda i: (0, 0)),
                  pl.BlockSpec((1, 1, d), lambda i: (i // per_b, 0, 0)),
                  pl.BlockSpec((1, 1, d), lambda i: (i // per_b, 0, 0))],
        out_specs=pl.BlockSpec((tm, d), lambda i: (i, 0)),
        out_shape=jax.ShapeDtypeStruct((t, d), out_dtype),
        compiler_params=_params(("arbitrary",)),
    )(x2d, g.reshape(1, d), scale[:, None, :], shift[:, None, :])


def _mm_kernel(a_ref, b_ref, *refs):
    n_side = (len(refs) - 1) // 2
    o_ref = refs[n_side]
    o_ref[...] = jnp.dot(a_ref[...], b_ref[...], preferred_element_type=F32).astype(o_ref.dtype)
    for src, dst in zip(refs[:n_side], refs[n_side + 1:]):
        dst[...] = src[...].astype(dst.dtype)


def _matmul(a, b, out_dtype, side=()):
    m, k = a.shape
    n = b.shape[1]
    tm = _pick_tile(m, (1024, 512, 256, 128))
    tn = _pick_tile(n, (512, 256, 128))
    ni, nj = m // tm, n // tn
    if any(s.shape[0] > ni * nj for s in side):
        prod, _ = _matmul(a, b, out_dtype)
        return prod, [s.astype(BF16) for s in side]

    def side_spec(s):
        units = s.shape[0]
        return pl.BlockSpec((1,) + s.shape[1:],
                            lambda i, j: (jnp.minimum(i * nj + j, units - 1), 0, 0))

    vmem = 48 + 2 * sum(6 * math.prod(s.shape[1:]) for s in side) // MIB
    outs = pl.pallas_call(
        _mm_kernel,
        grid=(ni, nj),
        in_specs=[pl.BlockSpec((tm, k), lambda i, j: (i, 0)),
                  pl.BlockSpec((k, tn), lambda i, j: (0, j))] + [side_spec(s) for s in side],
        out_specs=[pl.BlockSpec((tm, tn), lambda i, j: (i, j))] + [side_spec(s) for s in side],
        out_shape=[jax.ShapeDtypeStruct((m, n), out_dtype)]
                  + [jax.ShapeDtypeStruct(s.shape, BF16) for s in side],
        compiler_params=_params(("arbitrary", "arbitrary"), vmem_mib=min(vmem, 58)),
    )(a, b, *side)
    return outs[0], outs[1:]


def _softplus(z):
    return jnp.maximum(z, 0.0) + jnp.log(1.0 + jnp.exp(-jnp.abs(z)))


def _rwkv_kernel(r_ref, k_ref, v_ref, l_ref, mur_ref, muk_ref, muv_ref, mul_ref,
                 w0_ref, a0_ref, kk_ref, ka_ref, rk_ref, lw_ref, lb_ref,
                 ww_ref, wa_ref, wg_ref, o_ref,
                 s_ref, pr_ref, pk_ref, pv_ref, pl_ref, *, n_batch, dw, da):
    c = RWKV_CHUNK
    nh = RWKV_HEADS_PER_STEP
    hw = nh * HEAD_DIM
    groups = r_ref.shape[-1] // hw
    lp = l_ref.shape[-1]
    chunk = pl.program_id(1)

    @pl.when(chunk == 0)
    def _():
        s_ref[...] = jnp.zeros_like(s_ref)
        pr_ref[...] = jnp.zeros_like(pr_ref)
        pk_ref[...] = jnp.zeros_like(pk_ref)
        pv_ref[...] = jnp.zeros_like(pv_ref)
        pl_ref[...] = jnp.zeros_like(pl_ref)

    row_w = lax.broadcasted_iota(jnp.int32, (c, hw), 0)
    col_w = lax.broadcasted_iota(jnp.int32, (c, hw), 1)
    s_idx = col_w % c
    strict = row_w > s_idx
    incl = row_w >= s_idx
    eye_w = (row_w == s_idx).astype(F32)
    merge_masks = [strict & (row_w // 2 == s_idx // 2)]
    size = 2
    while size < c:
        merge_masks.append(strict & (row_w // (2 * size) == s_idx // (2 * size))
                           & (row_w // size != s_idx // size))
        size *= 2
    bd_row = lax.broadcasted_iota(jnp.int32, (hw, hw), 0) // HEAD_DIM
    bd_col = lax.broadcasted_iota(jnp.int32, (hw, hw), 1) // HEAD_DIM
    bd_mask = bd_row == bd_col
    bd_ones = bd_mask.astype(BF16)
    tri = (lax.broadcasted_iota(jnp.int32, (c, c), 0)
           >= lax.broadcasted_iota(jnp.int32, (c, c), 1)).astype(BF16)
    row_l = lax.broadcasted_iota(jnp.int32, (c, lp), 0)
    col_l = lax.broadcasted_iota(jnp.int32, (c, lp), 1)

    def shift_mix(x, prev, mu, row):
        shifted = jnp.where(row == 0, prev, pltpu.roll(x, 1, axis=0))
        return x + (shifted - x) * mu

    def split2(x):
        hi = x.astype(BF16)
        lo = (x - hi.astype(F32)).astype(BF16)
        return hi, lo

    chains = [(bi, gi) for gi in range(groups) for bi in range(n_batch)]
    nb = range(len(chains))

    def cols(x, gi):
        return x[..., gi * hw:(gi + 1) * hw]

    def vec(ref, ci):
        return cols(ref[...], chains[ci][1])

    def rows_split(x):
        return [x[i * c:(i + 1) * c] for i in range(x.shape[0] // c)]

    def chain_split(x):
        return [cols(x[bi * c:(bi + 1) * c], gi) for bi, gi in chains]

    def seg_sum(xs):
        parts = [split2(x) for x in xs]
        stacked = jnp.concatenate([p[0] for p in parts] + [p[1] for p in parts], axis=0)
        both = jnp.dot(stacked, bd_ones, preferred_element_type=F32)
        half = len(xs) * c
        return rows_split(both[:half] + both[half:])

    def bd(x):
        return jnp.where(bd_mask, jnp.tile(x, (nh, 1)), 0.0).astype(BF16)

    def mm(a, b):
        return jnp.dot(a.astype(BF16), b, preferred_element_type=F32)

    def mm_nt(a, b):
        return lax.dot_general(a.astype(BF16), b, (((1,), (1,)), ((), ())),
                               preferred_element_type=F32)

    def mixed(x_ref, p_ref, mu_ref):
        return [shift_mix(cols(x_ref[bi], gi), cols(p_ref[bi], gi), cols(mu_ref[...], gi), row_w)
                for bi, gi in chains]

    xr = mixed(r_ref, pr_ref, mur_ref)
    xk = mixed(k_ref, pk_ref, muk_ref)
    xv = mixed(v_ref, pv_ref, muv_ref)
    xl = [shift_mix(l_ref[bi], pl_ref[bi], mul_ref[...], row_l) for bi in range(n_batch)]
    states = [s_ref[bi, gi] for bi, gi in chains]

    act = jnp.concatenate(
        [jnp.where(col_l < dw, jnp.tanh(x), jnp.where(col_l < dw + da, x, jax.nn.sigmoid(x)))
         for x in xl], axis=0).astype(BF16)
    kw = _round_up(dw, LANES)
    ka = _round_up(dw + da, LANES)
    lora_w = chain_split(jnp.dot(act[:, :kw], ww_ref[:kw, :], preferred_element_type=F32))
    lora_a = chain_split(jnp.dot(act[:, :ka], wa_ref[:ka, :], preferred_element_type=F32))
    gate = chain_split(jnp.dot(act, wg_ref[...], preferred_element_type=F32))

    e = [jnp.exp(-_softplus(-(vec(w0_ref, ci) + lora_w[ci])) - 0.5) for ci in nb]
    lr = [jax.nn.sigmoid(vec(a0_ref, ci) + lora_a[ci]) for ci in nb]
    kk = [xk[ci] * vec(kk_ref, ci) for ci in nb]
    kk_ss = seg_sum([x * x for x in kk])
    kk = [x / jnp.maximum(jnp.sqrt(ss), 1e-12) for x, ss in zip(kk, kk_ss)]
    kmod = [xk[ci] * (1.0 + (lr[ci] - 1.0) * vec(ka_ref, ci)) for ci in nb]

    cum = []
    for ei in e:
        e_hi, e_lo = split2(ei)
        cum2 = jnp.dot(tri, jnp.concatenate([e_hi, e_lo], axis=1), preferred_element_type=F32)
        cum.append(cum2[:, :hw] + cum2[:, hw:])
    dec = [jnp.exp(-x) for x in cum]
    inv = [jnp.exp(x) for x in cum]
    at = [-kk[bi] * jnp.exp(e[bi] - cum[bi]) for bi in nb]
    rt = [xr[bi] * dec[bi] for bi in nb]
    bt = [kk[bi] * lr[bi] * inv[bi] for bi in nb]
    kt = [kmod[bi] * inv[bi] for bi in nb]

    ar = [jnp.concatenate([at[bi], rt[bi]], axis=0).astype(BF16) for bi in nb]
    g = [mm_nt(ar[bi], jnp.concatenate([bd(bt[bi]), bd(kt[bi])], axis=0)) for bi in nb]
    m_ab = [jnp.where(strict, x[:c, :hw], 0.0) for x in g]
    m_ak = [jnp.where(strict, x[:c, hw:], 0.0) for x in g]
    m_rb = [jnp.where(incl, x[c:, :hw], 0.0) for x in g]
    m_rk = [jnp.where(incl, x[c:, hw:], 0.0) for x in g]

    t_inv = [eye_w + jnp.where(merge_masks[0], m, 0.0) for m in m_ab]
    for mask in merge_masks[1:]:
        z = [mm(jnp.where(mask, m_ab[bi], 0.0), bd(t_inv[bi])) for bi in nb]
        t_inv = [t_inv[bi] + mm(t_inv[bi], bd(z[bi])) for bi in nb]

    s_bf = [s.astype(BF16) for s in states]
    bd_v = [bd(x) for x in xv]
    ar_s = [mm_nt(ar[bi], s_bf[bi]) for bi in nb]
    mv = [mm(jnp.concatenate([m_ak[bi], m_rk[bi]], axis=0), bd_v[bi]) for bi in nb]
    u = [mm(t_inv[bi], bd(ar_s[bi][:c] + mv[bi][:c])) for bi in nb]
    y = [ar_s[bi][c:] + mm(m_rb[bi], bd(u[bi])) + mv[bi][c:] for bi in nb]
    upd = [lax.dot_general(jnp.concatenate([u[bi], xv[bi]], axis=0).astype(BF16),
                           jnp.concatenate([bt[bi], kt[bi]], axis=0).astype(BF16),
                           (((0,), (0,)), ((), ())), preferred_element_type=F32) for bi in nb]

    mean = [m * (1.0 / HEAD_DIM) for m in seg_sum(y)]
    dy = [y[bi] - mean[bi] for bi in nb]
    var = [v * (1.0 / HEAD_DIM) for v in seg_sum([d * d for d in dy])]
    rk_sum = seg_sum([xr[ci] * kmod[ci] * vec(rk_ref, ci) for ci in nb])

    for ci, (bi, gi) in enumerate(chains):
        yn = dy[ci] * lax.rsqrt(var[ci] + RWKV_GN_EPS) * vec(lw_ref, ci) + vec(lb_ref, ci)
        o_ref[bi, :, gi * hw:(gi + 1) * hw] = ((yn + rk_sum[ci] * xv[ci]) * gate[ci]).astype(o_ref.dtype)
        s_ref[bi, gi] = (states[ci] + jnp.where(bd_mask, upd[ci], 0.0)) * dec[ci][c - 1:c]
    for bi in range(n_batch):
        pr_ref[bi] = r_ref[bi, c - 1:c, :]
        pk_ref[bi] = k_ref[bi, c - 1:c, :]
        pv_ref[bi] = v_ref[bi, c - 1:c, :]
        pl_ref[bi] = l_ref[bi, c - 1:c, :]


def _rwkv(proj, mu, w0, a0, k_k, k_a, r_k, lnx_w, lnx_b, ww, wa, wg, *, d_rwkv, lp, dw, da):
    b, s, _ = proj.shape
    c = RWKV_CHUNK
    hw1 = RWKV_HEADS_PER_STEP * HEAD_DIM
    groups = RWKV_GROUPS_PER_STEP if d_rwkv % (RWKV_GROUPS_PER_STEP * hw1) == 0 else 1
    hw = groups * hw1
    assert c == HEAD_DIM and d_rwkv % hw == 0 and s % c == 0 and (3 * d_rwkv) % lp == 0
    nhg = d_rwkv // hw
    lblk = (3 * d_rwkv) // lp

    def act_spec(off):
        return pl.BlockSpec((b, c, hw), lambda h, t: (0, t, off + h))

    def vec_spec(off):
        return pl.BlockSpec((1, hw), lambda h, t: (0, off + h))

    vec = vec_spec(0)
    lora_spec = pl.BlockSpec((lp, hw), lambda h, t: (0, h))
    kern = functools.partial(_rwkv_kernel, n_batch=b, dw=dw, da=da)
    return pl.pallas_call(
        kern,
        grid=(nhg, s // c),
        in_specs=[act_spec(0), act_spec(nhg), act_spec(2 * nhg),
                  pl.BlockSpec((b, c, lp), lambda h, t: (0, t, lblk)),
                  vec_spec(0), vec_spec(nhg), vec_spec(2 * nhg),
                  pl.BlockSpec((1, lp), lambda h, t: (0, lblk)),
                  vec, vec, vec, vec, vec, vec, vec,
                  lora_spec, lora_spec, lora_spec],
        out_specs=pl.BlockSpec((b, c, hw), lambda h, t: (0, t, h)),
        out_shape=jax.ShapeDtypeStruct((b, s, d_rwkv), BF16),
        scratch_shapes=[pltpu.VMEM((b, groups, hw1, hw1), F32),
                        pltpu.VMEM((b, 1, hw), F32), pltpu.VMEM((b, 1, hw), F32),
                        pltpu.VMEM((b, 1, hw), F32), pltpu.VMEM((b, 1, lp), F32)],
        compiler_params=_params(("arbitrary", "arbitrary")),
    )(proj, proj, proj, proj, mu, mu, mu, mu,
      w0, a0, k_k, k_a, r_k, lnx_w, lnx_b, ww, wa, wg)


def _attn_kernel(sink_ref, slope_ref, q_ref, kp_ref, kc_ref, vp_ref, vc_ref, og_ref, o_ref, *, gqa):
    blk = pl.program_id(2)
    pair = pl.program_id(1)
    wd = WINDOW
    qi = lax.broadcasted_iota(jnp.int32, (wd, 2 * wd), 0)
    kj = lax.broadcasted_iota(jnp.int32, (wd, 2 * wd), 1)
    dist = qi + wd - kj
    valid = (dist >= 0) & (dist < wd) & ((kj >= wd) | (blk > 0))
    dist_f = dist.astype(F32)
    scale = HEAD_DIM ** -0.5
    k2 = jnp.concatenate([kp_ref[...], kc_ref[...]], axis=0)
    v2 = jnp.concatenate([vp_ref[...], vc_ref[...]], axis=0)
    heads = range(2 * gqa)
    ks = [k2[:, kvh * HEAD_DIM:(kvh + 1) * HEAD_DIM] for kvh in range(2)]
    vs = [v2[:, kvh * HEAD_DIM:(kvh + 1) * HEAD_DIM] for kvh in range(2)]
    slopes = [slope_ref[pair * (2 * gqa) + hl] for hl in heads]
    sinks = [sink_ref[pair * (2 * gqa) + hl] for hl in heads]
    s = [lax.dot_general(q_ref[:, hl * HEAD_DIM:(hl + 1) * HEAD_DIM], ks[hl // gqa],
                         (((1,), (1,)), ((), ())), preferred_element_type=F32) for hl in heads]
    s = [jnp.where(valid, s[hl] * scale - slopes[hl] * dist_f, -jnp.inf) for hl in heads]
    m = [jnp.maximum(jnp.max(s[hl], axis=-1, keepdims=True), sinks[hl]) for hl in heads]
    p = [jnp.exp(s[hl] - m[hl]) for hl in heads]
    denom = [jnp.sum(p[hl], axis=-1, keepdims=True) + jnp.exp(sinks[hl] - m[hl]) for hl in heads]
    o = [jnp.dot(p[hl].astype(BF16), vs[hl // gqa], preferred_element_type=F32) / denom[hl]
         for hl in heads]
    o = [x * lax.rsqrt(jnp.mean(x * x, axis=-1, keepdims=True) + NORM_EPS) for x in o]
    o_all = jnp.concatenate(o, axis=-1) * og_ref[...]
    o_ref[...] = o_all.astype(o_ref.dtype)


def _attention(qkv, sinks, out_g, *, batch, seq, d_attn):
    d_kv = N_KV_HEADS * HEAD_DIM
    n_q_heads = d_attn // HEAD_DIM
    gqa = n_q_heads // N_KV_HEADS
    qw = 2 * gqa * HEAD_DIM
    n_pairs = N_KV_HEADS // 2
    nb = seq // WINDOW
    assert qw % LANES == 0 and d_attn % LANES == 0
    k_off = d_attn // LANES
    v_off = (d_attn + d_kv) // LANES

    def cur(off):
        return pl.BlockSpec((WINDOW, LANES), lambda b, p, i, *_: (b * nb + i, off + p))

    def prev(off):
        return pl.BlockSpec((WINDOW, LANES),
                            lambda b, p, i, *_: (b * nb + jnp.maximum(i - 1, 0), off + p))

    slopes = jnp.exp2(-8.0 * jnp.arange(1, n_q_heads + 1, dtype=F32) / n_q_heads)
    grid_spec = pltpu.PrefetchScalarGridSpec(
        num_scalar_prefetch=2,
        grid=(batch, n_pairs, nb),
        in_specs=[pl.BlockSpec((WINDOW, qw), lambda b, p, i, *_: (b * nb + i, p)),
                  prev(k_off), cur(k_off), prev(v_off), cur(v_off),
                  pl.BlockSpec((1, qw), lambda b, p, i, *_: (0, p))],
        out_specs=pl.BlockSpec((WINDOW, qw), lambda b, p, i, *_: (b * nb + i, p)),
    )
    return pl.pallas_call(
        functools.partial(_attn_kernel, gqa=gqa),
        grid_spec=grid_spec,
        out_shape=jax.ShapeDtypeStruct((batch * seq, d_attn), BF16),
        compiler_params=_params(("arbitrary", "arbitrary", "arbitrary")),
    )(sinks, slopes, qkv, qkv, qkv, qkv, qkv, out_g.reshape(1, d_attn))


def _outproj_kernel(yr_ref, ya_ref, w1_ref, w2_ref, x_ref, g_ref, o_ref):
    acc = jnp.dot(yr_ref[...], w1_ref[...], preferred_element_type=F32)
    acc = acc + jnp.dot(ya_ref[...], w2_ref[...], preferred_element_type=F32)
    o_ref[...] = x_ref[...] + g_ref[0] * acc


def _outproj(y_rwkv, y_attn, w_out, x2d, gate, seq):
    t, d = x2d.shape
    k1 = y_rwkv.shape[1]
    k2 = y_attn.shape[1]
    assert k1 == k2
    tm = _pick_tile(seq, (1024, 512, 256, 128))
    tn = _pick_tile(d, (512, 256, 128))
    per_b = seq // tm
    return pl.pallas_call(
        _outproj_kernel,
        grid=(t // tm, d // tn),
        in_specs=[pl.BlockSpec((tm, k1), lambda i, j: (i, 0)),
                  pl.BlockSpec((tm, k2), lambda i, j: (i, 0)),
                  pl.BlockSpec((k1, tn), lambda i, j: (0, j)),
                  pl.BlockSpec((k2, tn), lambda i, j: (1, j)),
                  pl.BlockSpec((tm, tn), lambda i, j: (i, j)),
                  pl.BlockSpec((1, 1, tn), lambda i, j: (i // per_b, 0, j))],
        out_specs=pl.BlockSpec((tm, tn), lambda i, j: (i, j)),
        out_shape=jax.ShapeDtypeStruct((t, d), F32),
        compiler_params=_params(("arbitrary", "arbitrary")),
    )(y_rwkv, y_attn, w_out, w_out, x2d, gate[:, None, :])


def _slab_pitch(ns):
    return ns + 8 if ns % 16 == 0 else _round_up(ns, 8)


def _to_slabs(ref, x):
    rows, d = x.shape
    ns = d // LANES
    pitch = _slab_pitch(ns)
    ref[...] = jnp.zeros_like(ref)
    for s in range(ns):
        ref[pl.ds(s, rows, stride=pitch), :] = x[:, s * LANES:(s + 1) * LANES]


def _from_slabs(ref, ns):
    pitch = _slab_pitch(ns)
    rows = ref.shape[0] // pitch
    return jnp.concatenate([ref[pl.ds(s, rows, stride=pitch), :] for s in range(ns)], axis=-1)


def _norm_route_kernel(x_ref, g_ref, sc_ref, sh_ref, w_ref, b_ref, u_ref, o_ref):
    x = x_ref[...]
    y = x * lax.rsqrt(jnp.mean(x * x, axis=-1, keepdims=True) + NORM_EPS)
    y = y * g_ref[...]
    y = y * (1.0 + sc_ref[0]) + sh_ref[0]
    _to_slabs(u_ref, y)
    logits = jnp.dot(y.astype(BF16), w_ref[...], preferred_element_type=F32) + b_ref[...]
    col = lax.broadcasted_iota(jnp.int32, logits.shape, 1).astype(F32)
    ng = float(N_GROUPS)
    epg = float(EXPERTS_PER_GROUP)
    big = 1e9
    is_g = col < ng
    lg = jnp.where(is_g, logits, -jnp.inf)
    mg = jnp.max(lg, axis=-1, keepdims=True)
    gidx = jnp.min(jnp.where(lg == mg, col, big), axis=-1, keepdims=True)
    zg = jnp.sum(jnp.where(is_g, jnp.exp(lg - mg), 0.0), axis=-1, keepdims=True)
    g_w = 1.0 / zg
    lo = ng + gidx * epg
    in_grp = (col >= lo) & (col < lo + epg)
    le = jnp.where(in_grp, logits, -jnp.inf)
    m1 = jnp.max(le, axis=-1, keepdims=True)
    i1 = jnp.min(jnp.where(le == m1, col, big), axis=-1, keepdims=True)
    le2 = jnp.where(col == i1, -jnp.inf, le)
    m2 = jnp.max(le2, axis=-1, keepdims=True)
    i2 = jnp.min(jnp.where(le2 == m2, col, big), axis=-1, keepdims=True)
    e2 = jnp.exp(m2 - m1)
    w1 = 1.0 / (1.0 + e2)
    w2 = e2 / (1.0 + e2)
    out = jnp.where(col == 0.0, i1 - ng,
                    jnp.where(col == 1.0, i2 - ng,
                              jnp.where(col == 2.0, g_w * w1,
                                        jnp.where(col == 3.0, g_w * w2, 0.0))))
    o_ref[...] = out


def _norm_route(x2d, g, scale, shift, w_r, b_r, seq):
    t, d = x2d.shape
    tm = _pick_tile(seq, (256, 128))
    per_b = seq // tm
    ns = d // LANES
    return pl.pallas_call(
        _norm_route_kernel,
        grid=(t // tm,),
        in_specs=[pl.BlockSpec((tm, d), lambda i: (i, 0)),
                  pl.BlockSpec((1, d), lambda i: (0, 0)),
                  pl.BlockSpec((1, 1, d), lambda i: (i // per_b, 0, 0)),
                  pl.BlockSpec((1, 1, d), lambda i: (i // per_b, 0, 0)),
                  pl.BlockSpec((d, LANES), lambda i: (0, 0)),
                  pl.BlockSpec((1, LANES), lambda i: (0, 0))],
        out_specs=[pl.BlockSpec((tm * _slab_pitch(ns), LANES), lambda i: (i, 0)),
                   pl.BlockSpec((tm, LANES), lambda i: (i, 0))],
        out_shape=[jax.ShapeDtypeStruct((t * _slab_pitch(ns), LANES), F32),
                   jax.ShapeDtypeStruct((t, LANES), F32)],
        compiler_params=_params(("arbitrary",)),
    )(x2d, g.reshape(1, d), scale[:, None, :], shift[:, None, :], w_r, b_r)


def _gather_rows(idx_ref, n_rows, src_hbm, dst, sem, col0=0, unroll=False, slab_rows=0):
    pitch = _slab_pitch(slab_rows) if slab_rows else 1
    size = slab_rows if slab_rows else 1

    def issue(r):
        tok = idx_ref[0, 0, col0 + r]
        if slab_rows:
            src = src_hbm.at[pl.ds(pl.multiple_of(tok * pitch, 8), size)]
            dst_r = dst.at[pl.ds(pl.multiple_of(r * pitch, 8), size)]
        else:
            src, dst_r = src_hbm.at[pl.ds(tok, 1)], dst.at[pl.ds(r, 1)]
        pltpu.make_async_copy(src, dst_r, sem).start()

    if unroll is True:
        for r in range(n_rows):
            issue(r)
    else:
        def body(r, carry):
            issue(r)
            return carry
        lax.fori_loop(0, n_rows, body, 0, unroll=max(int(unroll), 1))


def _wait_rows(n_rows, src_hbm, dst, sem):
    pltpu.make_async_copy(src_hbm.at[pl.ds(0, n_rows)], dst.at[pl.ds(0, n_rows)], sem).wait()


def _expert_kernel(be_ref, nu_ref, tok_ref, tokn_ref, u_hbm, wg_ref, wu_ref, wd_ref,
                   o_ref, xbuf_a, xbuf_b, sem):
    i = pl.program_id(0)
    last = pl.num_programs(0) - 1
    n_used = nu_ref[0]
    rows, d = o_ref.shape
    ns = d // LANES

    @pl.when(i == 0)
    def _():
        _gather_rows(tok_ref, rows, u_hbm, xbuf_a, sem.at[0], slab_rows=ns)

    def step(cur, cur_sem, nxt, nxt_sem):
        @pl.when(i <= n_used)
        def _():
            _wait_rows(rows * ns, u_hbm, cur, cur_sem)

        @pl.when(i < n_used)
        def _():
            _gather_rows(tokn_ref, rows, u_hbm, nxt, nxt_sem, unroll=8, slab_rows=ns)
            xb = _from_slabs(cur, ns).astype(BF16)
            gate = jnp.dot(xb, wg_ref[0], preferred_element_type=F32)
            up = jnp.dot(xb, wu_ref[0], preferred_element_type=F32)
            h = (gate * jax.nn.sigmoid(gate) * up).astype(BF16)
            o_ref[...] = jnp.dot(h, wd_ref[0], preferred_element_type=F32)

        @pl.when(i >= n_used)
        def _():
            o_ref[...] = jnp.zeros_like(o_ref)

        @pl.when((i == last) & (i < n_used))
        def _():
            _wait_rows(rows * ns, u_hbm, nxt, nxt_sem)

    @pl.when(i % 2 == 0)
    def _():
        step(xbuf_a, sem.at[0], xbuf_b, sem.at[1])

    @pl.when(i % 2 == 1)
    def _():
        step(xbuf_b, sem.at[1], xbuf_a, sem.at[0])


def _experts(block_expert, n_used, row_tok, u2, wg, wu, wd):
    d = wg.shape[1]
    pitch = _slab_pitch(d // LANES)
    nb = block_expert.shape[0]
    rows = MOE_ROWS
    de = wg.shape[-1]
    tok3 = row_tok.reshape(nb, 1, rows)
    grid_spec = pltpu.PrefetchScalarGridSpec(
        num_scalar_prefetch=2,
        grid=(nb,),
        in_specs=[pl.BlockSpec((1, 1, rows), lambda i, be, nu: (i, 0, 0), memory_space=pltpu.SMEM),
                  pl.BlockSpec((1, 1, rows), lambda i, be, nu: (jnp.minimum(i + 1, nb - 1), 0, 0),
                               memory_space=pltpu.SMEM),
                  pl.BlockSpec(memory_space=pl.ANY),
                  pl.BlockSpec((1, d, de), lambda i, be, nu: (be[i], 0, 0)),
                  pl.BlockSpec((1, d, de), lambda i, be, nu: (be[i], 0, 0)),
                  pl.BlockSpec((1, de, d), lambda i, be, nu: (be[i], 0, 0))],
        out_specs=pl.BlockSpec((rows, d), lambda i, be, nu: (i, 0)),
        scratch_shapes=[pltpu.VMEM((rows * pitch, LANES), F32),
                        pltpu.VMEM((rows * pitch, LANES), F32),
                        pltpu.SemaphoreType.DMA((2,))],
    )
    return pl.pallas_call(
        _expert_kernel,
        grid_spec=grid_spec,
        out_shape=jax.ShapeDtypeStruct((nb * rows, d), F32),
        compiler_params=_params(("arbitrary",), vmem_mib=58),
    )(block_expert, n_used, tok3, tok3, u2, wg, wu, wd)


def _combine_kernel(pos_ref, posn_ref, ys_hbm, r_ref, x_ref, g_ref, ng_ref, o_ref, buf_a, buf_b, sem):
    i = pl.program_id(0)
    last = pl.num_programs(0) - 1
    tc = x_ref.shape[0]

    def gather(idx_ref, buf, s, unroll):
        _gather_rows(idx_ref, tc, ys_hbm, buf.at[0], s, 0, unroll)
        _gather_rows(idx_ref, tc, ys_hbm, buf.at[1], s, tc, unroll)

    def wait(buf, s):
        _wait_rows(tc, ys_hbm, buf.at[0], s)
        _wait_rows(tc, ys_hbm, buf.at[1], s)

    @pl.when(i == 0)
    def _():
        gather(pos_ref, buf_a, sem.at[0], False)

    def step(cur, cur_sem, nxt, nxt_sem):
        wait(cur, cur_sem)
        gather(posn_ref, nxt, nxt_sem, True)
        moe = cur[0] * r_ref[:, 2:3] + cur[1] * r_ref[:, 3:4]
        x = x_ref[...] + g_ref[0] * moe
        y = x * lax.rsqrt(jnp.mean(x * x, axis=-1, keepdims=True) + NORM_EPS)
        o_ref[...] = y * ng_ref[...]

        @pl.when(i == last)
        def _():
            wait(nxt, nxt_sem)

    @pl.when(i % 2 == 0)
    def _():
        step(buf_a, sem.at[0], buf_b, sem.at[1])

    @pl.when(i % 2 == 1)
    def _():
        step(buf_b, sem.at[1], buf_a, sem.at[0])


def _combine(pos, route, ys, x1, gate, norm_g, seq):
    t, d = x1.shape
    tc = _pick_tile(seq, (128,))
    n = t // tc
    per_b = seq // tc
    pos3 = pos.reshape(n, tc, 2).transpose(0, 2, 1).reshape(n, 1, 2 * tc)
    return pl.pallas_call(
        _combine_kernel,
        grid=(n,),
        in_specs=[pl.BlockSpec((1, 1, 2 * tc), lambda i: (i, 0, 0), memory_space=pltpu.SMEM),
                  pl.BlockSpec((1, 1, 2 * tc), lambda i: (jnp.minimum(i + 1, n - 1), 0, 0),
                               memory_space=pltpu.SMEM),
                  pl.BlockSpec(memory_space=pl.ANY),
                  pl.BlockSpec((tc, LANES), lambda i: (i, 0)),
                  pl.BlockSpec((tc, d), lambda i: (i, 0)),
                  pl.BlockSpec((1, 1, d), lambda i: (i // per_b, 0, 0)),
                  pl.BlockSpec((1, d), lambda i: (0, 0))],
        out_specs=pl.BlockSpec((tc, d), lambda i: (i, 0)),
        out_shape=jax.ShapeDtypeStruct((t, d), F32),
        scratch_shapes=[pltpu.VMEM((2, tc, d), F32), pltpu.VMEM((2, tc, d), F32),
                        pltpu.SemaphoreType.DMA((2,))],
        compiler_params=_params(("arbitrary",)),
    )(pos3, pos3, ys, route, x1, gate[:, None, :], norm_g.reshape(1, d))


def _dispatch(expert_id, n_tok):
    m = expert_id.shape[0]
    rows = MOE_ROWS
    nb = -(-m // rows) + N_EXPERTS
    onehot = (expert_id[:, None] == jnp.arange(N_EXPERTS, dtype=jnp.int32)[None, :]).astype(jnp.int32)
    csum = jnp.cumsum(onehot, axis=0)
    rank = jnp.take_along_axis(csum, expert_id[:, None], axis=1)[:, 0] - 1
    counts = csum[-1]
    padded = ((counts + rows - 1) // rows) * rows
    pends = jnp.cumsum(padded)
    pstarts = pends - padded
    dest = (pstarts[expert_id] + rank).astype(jnp.int32)
    tok = jnp.arange(m, dtype=jnp.int32) // (m // n_tok)
    row_tok = jnp.zeros((nb * rows,), jnp.int32).at[dest].set(tok)
    n_used = (pends[-1] // rows).astype(jnp.int32)
    blk = jnp.arange(nb, dtype=jnp.int32)
    blk = jnp.minimum(blk, n_used - 1)
    block_expert = jnp.minimum(jnp.searchsorted(pends, blk * rows, side='right'),
                               N_EXPERTS - 1).astype(jnp.int32)
    return block_expert, n_used.reshape(1), row_tok, dest


def _layer(x, c, w_cond, b_cond, norm1_g, w_in, rwkv_mu, rwkv_w0, rwkv_w_up, rwkv_a0, rwkv_a_up,
           rwkv_g_up, rwkv_k_k, rwkv_k_a, rwkv_r_k, rwkv_lnx_w, rwkv_lnx_b, attn_sinks, attn_out_g,
           w_out, norm2_g, router_group, router_group_bias, router_expert, router_expert_bias,
           expert_w_gate, expert_w_up, expert_w_down):
    b, s, d = x.shape
    t = b * s
    d_rwkv = rwkv_w0.shape[-1]
    d_attn = attn_out_g.shape[-1]
    dw, da, dg = rwkv_w_up.shape[0], rwkv_a_up.shape[0], rwkv_g_up.shape[0]
    n_lora = dw + da + dg
    lp = _round_up(n_lora, LANES)
    while (3 * d_rwkv) % lp:
        lp += LANES
    n_rwkv_cols = 3 * d_rwkv + n_lora

    c_pad = jnp.zeros((_round_up(b, 8), d), F32).at[:b].set(c)
    mod = _cond(c_pad, w_cond, b_cond.reshape(1, -1))[:b]
    sh1, sc1, g1, sh2, sc2, g2 = jnp.split(mod, 6, axis=-1)

    x2d = x.reshape(t, d)
    u1 = _norm_mod(x2d, norm1_g, sc1, sh1, s, BF16)

    zpad = jnp.zeros((d, lp - n_lora), F32)
    w_r = jnp.concatenate([w_in[:, :n_rwkv_cols], zpad], axis=1).astype(BF16)
    w_a = w_in[:, n_rwkv_cols:].astype(BF16)
    n_exp, _, d_exp = expert_w_gate.shape
    halves = lambda w: w.reshape(2 * n_exp, d // 2, d_exp)
    proj_r, (wg_b, wu_b) = _matmul(u1, w_r, F32, side=(halves(expert_w_gate), halves(expert_w_up)))
    proj_a, (wd_b,) = _matmul(u1, w_a, BF16, side=(expert_w_down,))
    wg_b = wg_b.reshape(expert_w_gate.shape)
    wu_b = wu_b.reshape(expert_w_up.shape)

    def lora_rows(wmat, off):
        return jnp.zeros((lp, d_rwkv), F32).at[off:off + wmat.shape[0]].set(wmat).astype(BF16)

    mu = jnp.concatenate([rwkv_mu, jnp.zeros((lp - n_lora,), F32)]).reshape(1, -1)
    row = lambda v: v.reshape(1, d_rwkv)
    y_rwkv = _rwkv(proj_r.reshape(b, s, -1), mu, row(rwkv_w0), row(rwkv_a0), row(rwkv_k_k),
                   row(rwkv_k_a), row(rwkv_r_k), row(rwkv_lnx_w), row(rwkv_lnx_b),
                   lora_rows(rwkv_w_up, 0), lora_rows(rwkv_a_up, dw), lora_rows(rwkv_g_up, dw + da),
                   d_rwkv=d_rwkv, lp=lp, dw=dw, da=da).reshape(t, d_rwkv)

    y_attn = _attention(proj_a, attn_sinks, attn_out_g, batch=b, seq=s, d_attn=d_attn)

    x1 = _outproj(y_rwkv, y_attn, w_out.astype(BF16), x2d, g1, s)

    n_r = N_GROUPS + N_EXPERTS
    w_router = jnp.concatenate([router_group, router_expert, jnp.zeros((d, LANES - n_r), F32)],
                               axis=1).astype(BF16)
    b_router = jnp.concatenate([router_group_bias, router_expert_bias,
                                jnp.zeros((LANES - n_r,), F32)]).reshape(1, LANES)
    u2, route = _norm_route(x1, norm2_g, sc2, sh2, w_router, b_router, s)
    expert_id = route[:, :2].astype(jnp.int32).reshape(-1)
    block_expert, n_used, row_tok, dest = _dispatch(expert_id, t)
    ys = _experts(block_expert, n_used, row_tok, u2, wg_b, wu_b, wd_b)
    return ys, dest.reshape(t, 2), route, x1, g2


def kernel(x, c, w_cond, b_cond, norm1_g, w_in, rwkv_mu, rwkv_w0, rwkv_w_up, rwkv_a0, rwkv_a_up, rwkv_g_up, rwkv_k_k, rwkv_k_a, rwkv_r_k, rwkv_lnx_w, rwkv_lnx_b, attn_sinks, attn_out_g, w_out, norm2_g, router_group, router_group_bias, router_expert, router_expert_bias, expert_w_gate, expert_w_up, expert_w_down, norm_f_g):
    b, s, d = x.shape
    depth = w_cond.shape[0]
    assert depth == 1, "the fused final norm assumes a single layer"
    l = 0
    ys, pos, route, x1, g2 = _layer(
        x, c, w_cond[l], b_cond[l], norm1_g[l], w_in[l], rwkv_mu[l], rwkv_w0[l], rwkv_w_up[l],
        rwkv_a0[l], rwkv_a_up[l], rwkv_g_up[l], rwkv_k_k[l], rwkv_k_a[l], rwkv_r_k[l],
        rwkv_lnx_w[l], rwkv_lnx_b[l], attn_sinks[l], attn_out_g[l], w_out[l], norm2_g[l],
        router_group[l], router_group_bias[l], router_expert[l], router_expert_bias[l],
        expert_w_gate[l], expert_w_up[l], expert_w_down[l])
    out = _combine(pos, route, ys, x1, g2, norm_f_g, s)
    return out.reshape(b, s, d)
```

```python
import functools
import math

import jax
import jax.numpy as jnp
from jax import lax
from jax.experimental import pallas as pl
from jax.experimental.pallas import tpu as pltpu

F32 = jnp.float32
BF16 = jnp.bfloat16

HEAD_DIM = 64
N_KV_HEADS = 8
WINDOW = 128
N_GROUPS = 8
EXPERTS_PER_GROUP = 8
N_EXPERTS = N_GROUPS * EXPERTS_PER_GROUP
RWKV_GN_EPS = 64e-5
NORM_EPS = 1e-6
RWKV_CHUNK = 64
RWKV_HEADS_PER_STEP = 4
RWKV_GROUPS_PER_STEP = 4
MOE_ROWS = 256
LANES = 128
MIB = 1024 * 1024


def _params(sem, vmem_mib=48):
    return pltpu.CompilerParams(dimension_semantics=sem, vmem_limit_bytes=vmem_mib * MIB)


def _round_up(n, m):
    return -(-n // m) * m


def _pick_tile(n, candidates):
    for t in candidates:
        if n % t == 0:
            return t
    raise ValueError(f"no tile in {candidates} divides {n}")


def _cond_kernel(c_ref, w_ref, b_ref, o_ref):
    c = c_ref[...]
    sc = (c * jax.nn.sigmoid(c)).astype(BF16)
    o_ref[...] = jnp.dot(sc, w_ref[...].astype(BF16), preferred_element_type=F32) + b_ref[...]


def _cond(c_pad, w, b):
    rows, d = c_pad.shape
    n = w.shape[1]
    tn = _pick_tile(n, (1024, 512, 256, 128))
    return pl.pallas_call(
        _cond_kernel,
        grid=(n // tn,),
        in_specs=[pl.BlockSpec((rows, d), lambda j: (0, 0)),
                  pl.BlockSpec((d, tn), lambda j: (0, j)),
                  pl.BlockSpec((1, tn), lambda j: (0, j))],
        out_specs=pl.BlockSpec((rows, tn), lambda j: (0, j)),
        out_shape=jax.ShapeDtypeStruct((rows, n), F32),
        compiler_params=_params(("arbitrary",)),
    )(c_pad, w, b)


def _norm_mod_kernel(x_ref, g_ref, sc_ref, sh_ref, o_ref):
    x = x_ref[...]
    y = x * lax.rsqrt(jnp.mean(x * x, axis=-1, keepdims=True) + NORM_EPS)
    y = y * g_ref[...]
    o_ref[...] = (y * (1.0 + sc_ref[0]) + sh_ref[0]).astype(o_ref.dtype)


def _norm_mod(x2d, g, scale, shift, seq, out_dtype):
    t, d = x2d.shape
    tm = _pick_tile(seq, (512, 256, 128))
    per_b = seq // tm
    return pl.pallas_call(
        _norm_mod_kernel,
        grid=(t // tm,),
        in_specs=[pl.BlockSpec((tm, d), lambda i: (i, 0)),
                  pl.BlockSpec((1, d), lambda i: (0, 0)),
                  pl.BlockSpec((1, 1, d), lambda i: (i // per_b, 0, 0)),
                  pl.BlockSpec((1, 1, d), lambda i: (i // per_b, 0, 0))],
        out_specs=pl.BlockSpec((tm, d), lambda i: (i, 0)),
        out_shape=jax.ShapeDtypeStruct((t, d), out_dtype),
        compiler_params=_params(("arbitrary",)),
    )(x2d, g.reshape(1, d), scale[:, None, :], shift[:, None, :])


def _mm_kernel(a_ref, b_ref, *refs):
    n_side = (len(refs) - 1) // 2
    o_ref = refs[n_side]
    o_ref[...] = jnp.dot(a_ref[...], b_ref[...], preferred_element_type=F32).astype(o_ref.dtype)
    for src, dst in zip(refs[:n_side], refs[n_side + 1:]):
        dst[...] = src[...].astype(dst.dtype)


def _matmul(a, b, out_dtype, side=()):
    m, k = a.shape
    n = b.shape[1]
    tm = _pick_tile(m, (1024, 512, 256, 128))
    tn = _pick_tile(n, (512, 256, 128))
    ni, nj = m // tm, n // tn
    if any(s.shape[0] > ni * nj for s in side):
        prod, _ = _matmul(a, b, out_dtype)
        return prod, [s.astype(BF16) for s in side]

    def side_spec(s):
        units = s.shape[0]
        return pl.BlockSpec((1,) + s.shape[1:],
                            lambda i, j: (jnp.minimum(i * nj + j, units - 1), 0, 0))

    vmem = 48 + 2 * sum(6 * math.prod(s.shape[1:]) for s in side) // MIB
    outs = pl.pallas_call(
        _mm_kernel,
        grid=(ni, nj),
        in_specs=[pl.BlockSpec((tm, k), lambda i, j: (i, 0)),
                  pl.BlockSpec((k, tn), lambda i, j: (0, j))] + [side_spec(s) for s in side],
        out_specs=[pl.BlockSpec((tm, tn), lambda i, j: (i, j))] + [side_spec(s) for s in side],
        out_shape=[jax.ShapeDtypeStruct((m, n), out_dtype)]
                  + [jax.ShapeDtypeStruct(s.shape, BF16) for s in side],
        compiler_params=_params(("arbitrary", "arbitrary"), vmem_mib=min(vmem, 58)),
    )(a, b, *side)
    return outs[0], outs[1:]


def _softplus(z):
    return jnp.maximum(z, 0.0) + jnp.log(1.0 + jnp.exp(-jnp.abs(z)))


def _rwkv_kernel(r_ref, k_ref, v_ref, l_ref, mur_ref, muk_ref, muv_ref, mul_ref,
                 w0_ref, a0_ref, kk_ref, ka_ref, rk_ref, lw_ref, lb_ref,
                 ww_ref, wa_ref, wg_ref, o_ref,
                 s_ref, pr_ref, pk_ref, pv_ref, pl_ref, *, n_batch, dw, da):
    c = RWKV_CHUNK
    nh = RWKV_HEADS_PER_STEP
    hw = nh * HEAD_DIM
    groups = r_ref.shape[-1] // hw
    lp = l_ref.shape[-1]
    chunk = pl.program_id(1)

    @pl.when(chunk == 0)
    def _():
        s_ref[...] = jnp.zeros_like(s_ref)
        pr_ref[...] = jnp.zeros_like(pr_ref)
        pk_ref[...] = jnp.zeros_like(pk_ref)
        pv_ref[...] = jnp.zeros_like(pv_ref)
        pl_ref[...] = jnp.zeros_like(pl_ref)

    row_w = lax.broadcasted_iota(jnp.int32, (c, hw), 0)
    col_w = lax.broadcasted_iota(jnp.int32, (c, hw), 1)
    s_idx = col_w % c
    strict = row_w > s_idx
    incl = row_w >= s_idx
    eye_w = (row_w == s_idx).astype(F32)
    merge_masks = [strict & (row_w // 2 == s_idx // 2)]
    size = 2
    while size < c:
        merge_masks.append(strict & (row_w // (2 * size) == s_idx // (2 * size))
                           & (row_w // size != s_idx // size))
        size *= 2
    bd_row = lax.broadcasted_iota(jnp.int32, (hw, hw), 0) // HEAD_DIM
    bd_col = lax.broadcasted_iota(jnp.int32, (hw, hw), 1) // HEAD_DIM
    bd_mask = bd_row == bd_col
    bd_ones = bd_mask.astype(BF16)
    tri = (lax.broadcasted_iota(jnp.int32, (c, c), 0)
           >= lax.broadcasted_iota(jnp.int32, (c, c), 1)).astype(BF16)
    row_l = lax.broadcasted_iota(jnp.int32, (c, lp), 0)
    col_l = lax.broadcasted_iota(jnp.int32, (c, lp), 1)

    def shift_mix(x, prev, mu, row):
        shifted = jnp.where(row == 0, prev, pltpu.roll(x, 1, axis=0))
        return x + (shifted - x) * mu

    def split2(x):
        hi = x.astype(BF16)
        lo = (x - hi.astype(F32)).astype(BF16)
        return hi, lo

    chains = [(bi, gi) for gi in range(groups) for bi in range(n_batch)]
    nb = range(len(chains))

    def cols(x, gi):
        return x[..., gi * hw:(gi + 1) * hw]

    def vec(ref, ci):
        return cols(ref[...], chains[ci][1])

    def rows_split(x):
        return [x[i * c:(i + 1) * c] for i in range(x.shape[0] // c)]

    def chain_split(x):
        return [cols(x[bi * c:(bi + 1) * c], gi) for bi, gi in chains]

    def seg_sum(xs):
        parts = [split2(x) for x in xs]
        stacked = jnp.concatenate([p[0] for p in parts] + [p[1] for p in parts], axis=0)
        both = jnp.dot(stacked, bd_ones, preferred_element_type=F32)
        half = len(xs) * c
        return rows_split(both[:half] + both[half:])

    def bd(x):
        return jnp.where(bd_mask, jnp.tile(x, (nh, 1)), 0.0).astype(BF16)

    def mm(a, b):
        return jnp.dot(a.astype(BF16), b, preferred_element_type=F32)

    def mm_nt(a, b):
        return lax.dot_general(a.astype(BF16), b, (((1,), (1,)), ((), ())),
                               preferred_element_type=F32)

    def mixed(x_ref, p_ref, mu_ref):
        return [shift_mix(cols(x_ref[bi], gi).astype(F32), cols(p_ref[bi], gi),
                          cols(mu_ref[...], gi), row_w) for bi, gi in chains]

    xr = mixed(r_ref, pr_ref, mur_ref)
    xk = mixed(k_ref, pk_ref, muk_ref)
    xv = mixed(v_ref, pv_ref, muv_ref)
    xl = [shift_mix(l_ref[bi].astype(F32), pl_ref[bi], mul_ref[...], row_l) for bi in range(n_batch)]
    states = [s_ref[bi, gi] for bi, gi in chains]

    act = jnp.concatenate(
        [jnp.where(col_l < dw, jnp.tanh(x), jnp.where(col_l < dw + da, x, jax.nn.sigmoid(x)))
         for x in xl], axis=0).astype(BF16)
    kw = _round_up(dw, LANES)
    ka = _round_up(dw + da, LANES)
    lora_w = chain_split(jnp.dot(act[:, :kw], ww_ref[:kw, :], preferred_element_type=F32))
    lora_a = chain_split(jnp.dot(act[:, :ka], wa_ref[:ka, :], preferred_element_type=F32))
    gate = chain_split(jnp.dot(act, wg_ref[...], preferred_element_type=F32))

    e = [jnp.exp(-_softplus(-(vec(w0_ref, ci) + lora_w[ci])) - 0.5) for ci in nb]
    lr = [jax.nn.sigmoid(vec(a0_ref, ci) + lora_a[ci]) for ci in nb]
    kk = [xk[ci] * vec(kk_ref, ci) for ci in nb]
    kk_ss = seg_sum([x * x for x in kk])
    kk = [x / jnp.maximum(jnp.sqrt(ss), 1e-12) for x, ss in zip(kk, kk_ss)]
    kmod = [xk[ci] * (1.0 + (lr[ci] - 1.0) * vec(ka_ref, ci)) for ci in nb]

    cum = []
    for ei in e:
        e_hi, e_lo = split2(ei)
        cum2 = jnp.dot(tri, jnp.concatenate([e_hi, e_lo], axis=1), preferred_element_type=F32)
        cum.append(cum2[:, :hw] + cum2[:, hw:])
    dec = [jnp.exp(-x) for x in cum]
    inv = [jnp.exp(x) for x in cum]
    at = [-kk[bi] * jnp.exp(e[bi] - cum[bi]) for bi in nb]
    rt = [xr[bi] * dec[bi] for bi in nb]
    bt = [kk[bi] * lr[bi] * inv[bi] for bi in nb]
    kt = [kmod[bi] * inv[bi] for bi in nb]

    ar = [jnp.concatenate([at[bi], rt[bi]], axis=0).astype(BF16) for bi in nb]
    g = [mm_nt(ar[bi], jnp.concatenate([bd(bt[bi]), bd(kt[bi])], axis=0)) for bi in nb]
    m_ab = [jnp.where(strict, x[:c, :hw], 0.0) for x in g]
    m_ak = [jnp.where(strict, x[:c, hw:], 0.0) for x in g]
    m_rb = [jnp.where(incl, x[c:, :hw], 0.0) for x in g]
    m_rk = [jnp.where(incl, x[c:, hw:], 0.0) for x in g]

    t_inv = [eye_w + jnp.where(merge_masks[0], m, 0.0) for m in m_ab]
    for mask in merge_masks[1:]:
        z = [mm(jnp.where(mask, m_ab[bi], 0.0), bd(t_inv[bi])) for bi in nb]
        t_inv = [t_inv[bi] + mm(t_inv[bi], bd(z[bi])) for bi in nb]

    s_bf = [s.astype(BF16) for s in states]
    bd_v = [bd(x) for x in xv]
    ar_s = [mm_nt(ar[bi], s_bf[bi]) for bi in nb]
    mv = [mm(jnp.concatenate([m_ak[bi], m_rk[bi]], axis=0), bd_v[bi]) for bi in nb]
    u = [mm(t_inv[bi], bd(ar_s[bi][:c] + mv[bi][:c])) for bi in nb]
    y = [ar_s[bi][c:] + mm(m_rb[bi], bd(u[bi])) + mv[bi][c:] for bi in nb]
    upd = [lax.dot_general(jnp.concatenate([u[bi], xv[bi]], axis=0).astype(BF16),
                           jnp.concatenate([bt[bi], kt[bi]], axis=0).astype(BF16),
                           (((0,), (0,)), ((), ())), preferred_element_type=F32) for bi in nb]

    mean = [m * (1.0 / HEAD_DIM) for m in seg_sum(y)]
    dy = [y[bi] - mean[bi] for bi in nb]
    var = [v * (1.0 / HEAD_DIM) for v in seg_sum([d * d for d in dy])]
    rk_sum = seg_sum([xr[ci] * kmod[ci] * vec(rk_ref, ci) for ci in nb])

    for ci, (bi, gi) in enumerate(chains):
        yn = dy[ci] * lax.rsqrt(var[ci] + RWKV_GN_EPS) * vec(lw_ref, ci) + vec(lb_ref, ci)
        o_ref[bi, :, gi * hw:(gi + 1) * hw] = ((yn + rk_sum[ci] * xv[ci]) * gate[ci]).astype(o_ref.dtype)
        s_ref[bi, gi] = (states[ci] + jnp.where(bd_mask, upd[ci], 0.0)) * dec[ci][c - 1:c]
    def last_row(x_ref, bi):
        return x_ref[bi, c - 16:c, :].astype(F32)[15:16]

    for bi in range(n_batch):
        pr_ref[bi] = last_row(r_ref, bi)
        pk_ref[bi] = last_row(k_ref, bi)
        pv_ref[bi] = last_row(v_ref, bi)
        pl_ref[bi] = last_row(l_ref, bi)


def _rwkv(proj, mu, w0, a0, k_k, k_a, r_k, lnx_w, lnx_b, ww, wa, wg, *, d_rwkv, lp, dw, da):
    b, s, _ = proj.shape
    c = RWKV_CHUNK
    hw1 = RWKV_HEADS_PER_STEP * HEAD_DIM
    groups = RWKV_GROUPS_PER_STEP if d_rwkv % (RWKV_GROUPS_PER_STEP * hw1) == 0 else 1
    hw = groups * hw1
    assert c == HEAD_DIM and d_rwkv % hw == 0 and s % c == 0 and (3 * d_rwkv) % lp == 0
    nhg = d_rwkv // hw
    lblk = (3 * d_rwkv) // lp

    def act_spec(off):
        return pl.BlockSpec((b, c, hw), lambda h, t: (0, t, off + h))

    def vec_spec(off):
        return pl.BlockSpec((1, hw), lambda h, t: (0, off + h))

    vec = vec_spec(0)
    lora_spec = pl.BlockSpec((lp, hw), lambda h, t: (0, h))
    kern = functools.partial(_rwkv_kernel, n_batch=b, dw=dw, da=da)
    return pl.pallas_call(
        kern,
        grid=(nhg, s // c),
        in_specs=[act_spec(0), act_spec(nhg), act_spec(2 * nhg),
                  pl.BlockSpec((b, c, lp), lambda h, t: (0, t, lblk)),
                  vec_spec(0), vec_spec(nhg), vec_spec(2 * nhg),
                  pl.BlockSpec((1, lp), lambda h, t: (0, lblk)),
                  vec, vec, vec, vec, vec, vec, vec,
                  lora_spec, lora_spec, lora_spec],
        out_specs=pl.BlockSpec((b, c, hw), lambda h, t: (0, t, h)),
        out_shape=jax.ShapeDtypeStruct((b, s, d_rwkv), BF16),
        scratch_shapes=[pltpu.VMEM((b, groups, hw1, hw1), F32),
                        pltpu.VMEM((b, 1, hw), F32), pltpu.VMEM((b, 1, hw), F32),
                        pltpu.VMEM((b, 1, hw), F32), pltpu.VMEM((b, 1, lp), F32)],
        compiler_params=_params(("arbitrary", "arbitrary")),
    )(proj, proj, proj, proj, mu, mu, mu, mu,
      w0, a0, k_k, k_a, r_k, lnx_w, lnx_b, ww, wa, wg)


def _attn_kernel(sink_ref, slope_ref, q_ref, kp_ref, kc_ref, vp_ref, vc_ref, og_ref, o_ref, *, gqa, qb):
    step = pl.program_id(2)
    pair = pl.program_id(1)
    wd = WINDOW
    qi = lax.broadcasted_iota(jnp.int32, (wd, 2 * wd), 0)
    kj = lax.broadcasted_iota(jnp.int32, (wd, 2 * wd), 1)
    dist = qi + wd - kj
    band = (dist >= 0) & (dist < wd)
    valid = [band & ((kj >= wd) | (step > 0))] + [band] * (qb - 1)
    dist_f = dist.astype(F32)
    scale = HEAD_DIM ** -0.5
    k_all = jnp.concatenate([kp_ref[...], kc_ref[...]], axis=0)
    v_all = jnp.concatenate([vp_ref[...], vc_ref[...]], axis=0)
    heads = range(2 * gqa)
    units = [(sb, hl) for sb in range(qb) for hl in heads]
    nu = range(len(units))

    def kv(x, sb, hl):
        kvh = hl // gqa
        return x[sb * wd:(sb + 2) * wd, kvh * HEAD_DIM:(kvh + 1) * HEAD_DIM]

    slopes = [slope_ref[pair * (2 * gqa) + hl] for _, hl in units]
    sinks = [sink_ref[pair * (2 * gqa) + hl] for _, hl in units]
    s = [lax.dot_general(q_ref[sb * wd:(sb + 1) * wd, hl * HEAD_DIM:(hl + 1) * HEAD_DIM],
                         kv(k_all, sb, hl), (((1,), (1,)), ((), ())), preferred_element_type=F32)
         for sb, hl in units]
    s = [jnp.where(valid[units[u][0]], s[u] * scale - slopes[u] * dist_f, -jnp.inf) for u in nu]
    m = [jnp.maximum(jnp.max(s[u], axis=-1, keepdims=True), sinks[u]) for u in nu]
    p = [jnp.exp(s[u] - m[u]) for u in nu]
    denom = [jnp.sum(p[u], axis=-1, keepdims=True) + jnp.exp(sinks[u] - m[u]) for u in nu]
    o = [jnp.dot(p[u].astype(BF16), kv(v_all, *units[u]), preferred_element_type=F32) / denom[u]
         for u in nu]
    o = [x * lax.rsqrt(jnp.mean(x * x, axis=-1, keepdims=True) + NORM_EPS) for x in o]
    nh = len(heads)
    o_all = jnp.concatenate([jnp.concatenate(o[sb * nh:(sb + 1) * nh], axis=-1) for sb in range(qb)],
                            axis=0) * og_ref[...]
    o_ref[...] = o_all.astype(o_ref.dtype)


def _attention(qkv, sinks, out_g, *, batch, seq, d_attn):
    d_kv = N_KV_HEADS * HEAD_DIM
    n_q_heads = d_attn // HEAD_DIM
    gqa = n_q_heads // N_KV_HEADS
    qw = 2 * gqa * HEAD_DIM
    n_pairs = N_KV_HEADS // 2
    nb = seq // WINDOW
    assert qw % LANES == 0 and d_attn % LANES == 0
    k_off = d_attn // LANES
    v_off = (d_attn + d_kv) // LANES

    qb = 2 if nb % 2 == 0 else 1
    ns = nb // qb

    def cur(off):
        return pl.BlockSpec((qb * WINDOW, LANES), lambda b, p, i, *_: (b * ns + i, off + p))

    def prev(off):
        return pl.BlockSpec((WINDOW, LANES),
                            lambda b, p, i, *_: (b * nb + jnp.maximum(qb * i - 1, 0), off + p))

    slopes = jnp.exp2(-8.0 * jnp.arange(1, n_q_heads + 1, dtype=F32) / n_q_heads)
    grid_spec = pltpu.PrefetchScalarGridSpec(
        num_scalar_prefetch=2,
        grid=(batch, n_pairs, ns),
        in_specs=[pl.BlockSpec((qb * WINDOW, qw), lambda b, p, i, *_: (b * ns + i, p)),
                  prev(k_off), cur(k_off), prev(v_off), cur(v_off),
                  pl.BlockSpec((1, qw), lambda b, p, i, *_: (0, p))],
        out_specs=pl.BlockSpec((qb * WINDOW, qw), lambda b, p, i, *_: (b * ns + i, p)),
    )
    return pl.pallas_call(
        functools.partial(_attn_kernel, gqa=gqa, qb=qb),
        grid_spec=grid_spec,
        out_shape=jax.ShapeDtypeStruct((batch * seq, d_attn), BF16),
        compiler_params=_params(("arbitrary", "arbitrary", "arbitrary")),
    )(sinks, slopes, qkv, qkv, qkv, qkv, qkv, out_g.reshape(1, d_attn))


def _outproj_kernel(yr_ref, ya_ref, w1_ref, w2_ref, x_ref, g_ref, o_ref):
    acc = jnp.dot(yr_ref[...], w1_ref[...], preferred_element_type=F32)
    acc = acc + jnp.dot(ya_ref[...], w2_ref[...], preferred_element_type=F32)
    o_ref[...] = x_ref[...] + g_ref[0] * acc


def _outproj(y_rwkv, y_attn, w_out, x2d, gate, seq):
    t, d = x2d.shape
    k1 = y_rwkv.shape[1]
    k2 = y_attn.shape[1]
    assert k1 == k2
    tm = _pick_tile(seq, (1024, 512, 256, 128))
    tn = _pick_tile(d, (512, 256, 128))
    per_b = seq // tm
    return pl.pallas_call(
        _outproj_kernel,
        grid=(t // tm, d // tn),
        in_specs=[pl.BlockSpec((tm, k1), lambda i, j: (i, 0)),
                  pl.BlockSpec((tm, k2), lambda i, j: (i, 0)),
                  pl.BlockSpec((k1, tn), lambda i, j: (0, j)),
                  pl.BlockSpec((k2, tn), lambda i, j: (1, j)),
                  pl.BlockSpec((tm, tn), lambda i, j: (i, j)),
                  pl.BlockSpec((1, 1, tn), lambda i, j: (i // per_b, 0, j))],
        out_specs=pl.BlockSpec((tm, tn), lambda i, j: (i, j)),
        out_shape=jax.ShapeDtypeStruct((t, d), F32),
        compiler_params=_params(("arbitrary", "arbitrary")),
    )(y_rwkv, y_attn, w_out, w_out, x2d, gate[:, None, :])


U32 = jnp.uint32


def _pack_pairs(x):
    n = x.shape[1] // 2
    lo = lax.bitcast_convert_type(x[:, :n].astype(BF16).astype(F32), U32)
    hi = lax.bitcast_convert_type(x[:, n:].astype(BF16).astype(F32), U32)
    return (lo >> 16) | hi


def _unpack_pairs(w):
    lo = lax.bitcast_convert_type(w << 16, F32)
    hi = lax.bitcast_convert_type(w & jnp.uint32(0xFFFF0000), F32)
    return jnp.concatenate([lo, hi], axis=-1)


def _slab_pitch(ns):
    return ns + 8 if ns % 16 == 0 else _round_up(ns, 8)


def _to_slabs(ref, x):
    rows, d = x.shape
    ns = d // LANES
    pitch = _slab_pitch(ns)
    ref[...] = jnp.zeros_like(ref)
    for s in range(ns):
        ref[pl.ds(s, rows, stride=pitch), :] = x[:, s * LANES:(s + 1) * LANES]


def _from_slabs(ref, ns):
    pitch = _slab_pitch(ns)
    rows = ref.shape[0] // pitch
    return jnp.concatenate([ref[pl.ds(s, rows, stride=pitch), :] for s in range(ns)], axis=-1)


def _norm_route_kernel(x_ref, g_ref, sc_ref, sh_ref, w_ref, b_ref, u_ref, o_ref):
    x = x_ref[...]
    y = x * lax.rsqrt(jnp.mean(x * x, axis=-1, keepdims=True) + NORM_EPS)
    y = y * g_ref[...]
    y = y * (1.0 + sc_ref[0]) + sh_ref[0]
    _to_slabs(u_ref, _pack_pairs(y))
    logits = jnp.dot(y.astype(BF16), w_ref[...], preferred_element_type=F32) + b_ref[...]
    col = lax.broadcasted_iota(jnp.int32, logits.shape, 1).astype(F32)
    ng = float(N_GROUPS)
    epg = float(EXPERTS_PER_GROUP)
    big = 1e9
    is_g = col < ng
    lg = jnp.where(is_g, logits, -jnp.inf)
    mg = jnp.max(lg, axis=-1, keepdims=True)
    gidx = jnp.min(jnp.where(lg == mg, col, big), axis=-1, keepdims=True)
    zg = jnp.sum(jnp.where(is_g, jnp.exp(lg - mg), 0.0), axis=-1, keepdims=True)
    g_w = 1.0 / zg
    lo = ng + gidx * epg
    in_grp = (col >= lo) & (col < lo + epg)
    le = jnp.where(in_grp, logits, -jnp.inf)
    m1 = jnp.max(le, axis=-1, keepdims=True)
    i1 = jnp.min(jnp.where(le == m1, col, big), axis=-1, keepdims=True)
    le2 = jnp.where(col == i1, -jnp.inf, le)
    m2 = jnp.max(le2, axis=-1, keepdims=True)
    i2 = jnp.min(jnp.where(le2 == m2, col, big), axis=-1, keepdims=True)
    e2 = jnp.exp(m2 - m1)
    w1 = 1.0 / (1.0 + e2)
    w2 = e2 / (1.0 + e2)
    out = jnp.where(col == 0.0, i1 - ng,
                    jnp.where(col == 1.0, i2 - ng,
                              jnp.where(col == 2.0, g_w * w1,
                                        jnp.where(col == 3.0, g_w * w2, 0.0))))
    o_ref[...] = out


def _norm_route(x2d, g, scale, shift, w_r, b_r, seq):
    t, d = x2d.shape
    tm = _pick_tile(seq, (256, 128))
    per_b = seq // tm
    ns = d // (2 * LANES)
    return pl.pallas_call(
        _norm_route_kernel,
        grid=(t // tm,),
        in_specs=[pl.BlockSpec((tm, d), lambda i: (i, 0)),
                  pl.BlockSpec((1, d), lambda i: (0, 0)),
                  pl.BlockSpec((1, 1, d), lambda i: (i // per_b, 0, 0)),
                  pl.BlockSpec((1, 1, d), lambda i: (i // per_b, 0, 0)),
                  pl.BlockSpec((d, LANES), lambda i: (0, 0)),
                  pl.BlockSpec((1, LANES), lambda i: (0, 0))],
        out_specs=[pl.BlockSpec((tm * _slab_pitch(ns), LANES), lambda i: (i, 0)),
                   pl.BlockSpec((tm, LANES), lambda i: (i, 0))],
        out_shape=[jax.ShapeDtypeStruct((t * _slab_pitch(ns), LANES), U32),
                   jax.ShapeDtypeStruct((t, LANES), F32)],
        compiler_params=_params(("arbitrary",)),
    )(x2d, g.reshape(1, d), scale[:, None, :], shift[:, None, :], w_r, b_r)


def _gather_rows(idx_ref, n_rows, src_hbm, dst, sem, col0=0, unroll=False, slab_rows=0):
    pitch = _slab_pitch(slab_rows) if slab_rows else 1
    size = slab_rows if slab_rows else 1

    def issue(r):
        tok = idx_ref[0, 0, col0 + r]
        if slab_rows:
            src = src_hbm.at[pl.ds(pl.multiple_of(tok * pitch, 8), size)]
            dst_r = dst.at[pl.ds(pl.multiple_of(r * pitch, 8), size)]
        else:
            src, dst_r = src_hbm.at[pl.ds(tok, 1)], dst.at[pl.ds(r, 1)]
        pltpu.make_async_copy(src, dst_r, sem).start()

    if unroll is True:
        for r in range(n_rows):
            issue(r)
    else:
        def body(r, carry):
            issue(r)
            return carry
        lax.fori_loop(0, n_rows, body, 0, unroll=max(int(unroll), 1))


def _wait_rows(n_rows, src_hbm, dst, sem):
    pltpu.make_async_copy(src_hbm.at[pl.ds(0, n_rows)], dst.at[pl.ds(0, n_rows)], sem).wait()


def _expert_kernel(be_ref, nu_ref, tok_ref, tokn_ref, u_hbm, wg_ref, wu_ref, wd_ref,
                   o_ref, xbuf_a, xbuf_b, sem):
    i = pl.program_id(0)
    last = pl.num_programs(0) - 1
    n_used = nu_ref[0]
    rows, n_words = o_ref.shape
    ns = n_words // LANES

    @pl.when(i == 0)
    def _():
        _gather_rows(tok_ref, rows, u_hbm, xbuf_a, sem.at[0], slab_rows=ns)

    def step(cur, cur_sem, nxt, nxt_sem):
        @pl.when(i <= n_used)
        def _():
            _wait_rows(rows * ns, u_hbm, cur, cur_sem)

        @pl.when(i < n_used)
        def _():
            _gather_rows(tokn_ref, rows, u_hbm, nxt, nxt_sem, unroll=8, slab_rows=ns)
            xb = _unpack_pairs(_from_slabs(cur, ns)).astype(BF16)
            gate = jnp.dot(xb, wg_ref[0], preferred_element_type=F32)
            up = jnp.dot(xb, wu_ref[0], preferred_element_type=F32)
            h = (gate * jax.nn.sigmoid(gate) * up).astype(BF16)
            o_ref[...] = _pack_pairs(jnp.dot(h, wd_ref[0], preferred_element_type=F32))

        @pl.when(i >= n_used)
        def _():
            o_ref[...] = jnp.zeros_like(o_ref)

        @pl.when((i == last) & (i < n_used))
        def _():
            _wait_rows(rows * ns, u_hbm, nxt, nxt_sem)

    @pl.when(i % 2 == 0)
    def _():
        step(xbuf_a, sem.at[0], xbuf_b, sem.at[1])

    @pl.when(i % 2 == 1)
    def _():
        step(xbuf_b, sem.at[1], xbuf_a, sem.at[0])


def _experts(block_expert, n_used, row_tok, u2, wg, wu, wd):
    d = wg.shape[1]
    n_words = d // 2
    pitch = _slab_pitch(n_words // LANES)
    nb = block_expert.shape[0]
    rows = MOE_ROWS
    de = wg.shape[-1]
    tok3 = row_tok.reshape(nb, 1, rows)
    grid_spec = pltpu.PrefetchScalarGridSpec(
        num_scalar_prefetch=2,
        grid=(nb,),
        in_specs=[pl.BlockSpec((1, 1, rows), lambda i, be, nu: (i, 0, 0), memory_space=pltpu.SMEM),
                  pl.BlockSpec((1, 1, rows), lambda i, be, nu: (jnp.minimum(i + 1, nb - 1), 0, 0),
                               memory_space=pltpu.SMEM),
                  pl.BlockSpec(memory_space=pl.ANY),
                  pl.BlockSpec((1, d, de), lambda i, be, nu: (be[i], 0, 0)),
                  pl.BlockSpec((1, d, de), lambda i, be, nu: (be[i], 0, 0)),
                  pl.BlockSpec((1, de, d), lambda i, be, nu: (be[i], 0, 0))],
        out_specs=pl.BlockSpec((rows, n_words), lambda i, be, nu: (i, 0)),
        scratch_shapes=[pltpu.VMEM((rows * pitch, LANES), U32),
                        pltpu.VMEM((rows * pitch, LANES), U32),
                        pltpu.SemaphoreType.DMA((2,))],
    )
    return pl.pallas_call(
        _expert_kernel,
        grid_spec=grid_spec,
        out_shape=jax.ShapeDtypeStruct((nb * rows, n_words), U32),
        compiler_params=_params(("arbitrary",), vmem_mib=58),
    )(block_expert, n_used, tok3, tok3, u2, wg, wu, wd)


def _combine_kernel(pos_ref, posn_ref, ys_hbm, r_ref, x_ref, g_ref, ng_ref, o_ref, buf_a, buf_b, sem):
    i = pl.program_id(0)
    last = pl.num_programs(0) - 1
    tc = x_ref.shape[0]

    def gather(idx_ref, buf, s, unroll):
        _gather_rows(idx_ref, tc, ys_hbm, buf.at[0], s, 0, unroll)
        _gather_rows(idx_ref, tc, ys_hbm, buf.at[1], s, tc, unroll)

    def wait(buf, s):
        _wait_rows(tc, ys_hbm, buf.at[0], s)
        _wait_rows(tc, ys_hbm, buf.at[1], s)

    @pl.when(i == 0)
    def _():
        gather(pos_ref, buf_a, sem.at[0], False)

    def step(cur, cur_sem, nxt, nxt_sem):
        wait(cur, cur_sem)
        gather(posn_ref, nxt, nxt_sem, True)
        moe = _unpack_pairs(cur[0]) * r_ref[:, 2:3] + _unpack_pairs(cur[1]) * r_ref[:, 3:4]
        x = x_ref[...] + g_ref[0] * moe
        y = x * lax.rsqrt(jnp.mean(x * x, axis=-1, keepdims=True) + NORM_EPS)
        o_ref[...] = y * ng_ref[...]

        @pl.when(i == last)
        def _():
            wait(nxt, nxt_sem)

    @pl.when(i % 2 == 0)
    def _():
        step(buf_a, sem.at[0], buf_b, sem.at[1])

    @pl.when(i % 2 == 1)
    def _():
        step(buf_b, sem.at[1], buf_a, sem.at[0])


def _combine(pos, route, ys, x1, gate, norm_g, seq):
    t, d = x1.shape
    tc = _pick_tile(seq, (128,))
    n = t // tc
    per_b = seq // tc
    pos3 = pos.reshape(n, tc, 2).transpose(0, 2, 1).reshape(n, 1, 2 * tc)
    return pl.pallas_call(
        _combine_kernel,
        grid=(n,),
        in_specs=[pl.BlockSpec((1, 1, 2 * tc), lambda i: (i, 0, 0), memory_space=pltpu.SMEM),
                  pl.BlockSpec((1, 1, 2 * tc), lambda i: (jnp.minimum(i + 1, n - 1), 0, 0),
                               memory_space=pltpu.SMEM),
                  pl.BlockSpec(memory_space=pl.ANY),
                  pl.BlockSpec((tc, LANES), lambda i: (i, 0)),
                  pl.BlockSpec((tc, d), lambda i: (i, 0)),
                  pl.BlockSpec((1, 1, d), lambda i: (i // per_b, 0, 0)),
                  pl.BlockSpec((1, d), lambda i: (0, 0))],
        out_specs=pl.BlockSpec((tc, d), lambda i: (i, 0)),
        out_shape=jax.ShapeDtypeStruct((t, d), F32),
        scratch_shapes=[pltpu.VMEM((2, tc, d // 2), U32), pltpu.VMEM((2, tc, d // 2), U32),
                        pltpu.SemaphoreType.DMA((2,))],
        compiler_params=_params(("arbitrary",)),
    )(pos3, pos3, ys, route, x1, gate[:, None, :], norm_g.reshape(1, d))


def _dispatch(expert_id, n_tok):
    m = expert_id.shape[0]
    rows = MOE_ROWS
    nb = -(-m // rows) + N_EXPERTS
    onehot = (expert_id[:, None] == jnp.arange(N_EXPERTS, dtype=jnp.int32)[None, :]).astype(jnp.int32)
    csum = jnp.cumsum(onehot, axis=0)
    rank = jnp.take_along_axis(csum, expert_id[:, None], axis=1)[:, 0] - 1
    counts = csum[-1]
    padded = ((counts + rows - 1) // rows) * rows
    pends = jnp.cumsum(padded)
    pstarts = pends - padded
    dest = (pstarts[expert_id] + rank).astype(jnp.int32)
    tok = jnp.arange(m, dtype=jnp.int32) // (m // n_tok)
    row_tok = jnp.zeros((nb * rows,), jnp.int32).at[dest].set(tok)
    n_used = (pends[-1] // rows).astype(jnp.int32)
    blk = jnp.arange(nb, dtype=jnp.int32)
    blk = jnp.minimum(blk, n_used - 1)
    block_expert = jnp.minimum(jnp.searchsorted(pends, blk * rows, side='right'),
                               N_EXPERTS - 1).astype(jnp.int32)
    return block_expert, n_used.reshape(1), row_tok, dest


def _layer(x, c, w_cond, b_cond, norm1_g, w_in, rwkv_mu, rwkv_w0, rwkv_w_up, rwkv_a0, rwkv_a_up,
           rwkv_g_up, rwkv_k_k, rwkv_k_a, rwkv_r_k, rwkv_lnx_w, rwkv_lnx_b, attn_sinks, attn_out_g,
           w_out, norm2_g, router_group, router_group_bias, router_expert, router_expert_bias,
           expert_w_gate, expert_w_up, expert_w_down):
    b, s, d = x.shape
    t = b * s
    d_rwkv = rwkv_w0.shape[-1]
    d_attn = attn_out_g.shape[-1]
    dw, da, dg = rwkv_w_up.shape[0], rwkv_a_up.shape[0], rwkv_g_up.shape[0]
    n_lora = dw + da + dg
    lp = _round_up(n_lora, LANES)
    while (3 * d_rwkv) % lp:
        lp += LANES
    n_rwkv_cols = 3 * d_rwkv + n_lora

    c_pad = jnp.zeros((_round_up(b, 8), d), F32).at[:b].set(c)
    mod = _cond(c_pad, w_cond, b_cond.reshape(1, -1))[:b]
    sh1, sc1, g1, sh2, sc2, g2 = jnp.split(mod, 6, axis=-1)

    x2d = x.reshape(t, d)
    u1 = _norm_mod(x2d, norm1_g, sc1, sh1, s, BF16)

    zpad = jnp.zeros((d, lp - n_lora), F32)
    w_r = jnp.concatenate([w_in[:, :n_rwkv_cols], zpad], axis=1).astype(BF16)
    w_a = w_in[:, n_rwkv_cols:].astype(BF16)
    n_exp, _, d_exp = expert_w_gate.shape
    halves = lambda w: w.reshape(2 * n_exp, d // 2, d_exp)
    proj_r, (wg_b, wu_b) = _matmul(u1, w_r, BF16, side=(halves(expert_w_gate), halves(expert_w_up)))
    proj_a, (wd_b,) = _matmul(u1, w_a, BF16, side=(expert_w_down,))
    wg_b = wg_b.reshape(expert_w_gate.shape)
    wu_b = wu_b.reshape(expert_w_up.shape)

    def lora_rows(wmat, off):
        return jnp.zeros((lp, d_rwkv), F32).at[off:off + wmat.shape[0]].set(wmat).astype(BF16)

    mu = jnp.concatenate([rwkv_mu, jnp.zeros((lp - n_lora,), F32)]).reshape(1, -1)
    row = lambda v: v.reshape(1, d_rwkv)
    y_rwkv = _rwkv(proj_r.reshape(b, s, -1), mu, row(rwkv_w0), row(rwkv_a0), row(rwkv_k_k),
                   row(rwkv_k_a), row(rwkv_r_k), row(rwkv_lnx_w), row(rwkv_lnx_b),
                   lora_rows(rwkv_w_up, 0), lora_rows(rwkv_a_up, dw), lora_rows(rwkv_g_up, dw + da),
                   d_rwkv=d_rwkv, lp=lp, dw=dw, da=da).reshape(t, d_rwkv)

    y_attn = _attention(proj_a, attn_sinks, attn_out_g, batch=b, seq=s, d_attn=d_attn)

    x1 = _outproj(y_rwkv, y_attn, w_out.astype(BF16), x2d, g1, s)

    n_r = N_GROUPS + N_EXPERTS
    w_router = jnp.concatenate([router_group, router_expert, jnp.zeros((d, LANES - n_r), F32)],
                               axis=1).astype(BF16)
    b_router = jnp.concatenate([router_group_bias, router_expert_bias,
                                jnp.zeros((LANES - n_r,), F32)]).reshape(1, LANES)
    u2, route = _norm_route(x1, norm2_g, sc2, sh2, w_router, b_router, s)
    expert_id = route[:, :2].astype(jnp.int32).reshape(-1)
    block_expert, n_used, row_tok, dest = _dispatch(expert_id, t)
    ys = _experts(block_expert, n_used, row_tok, u2, wg_b, wu_b, wd_b)
    return ys, dest.reshape(t, 2), route, x1, g2


def kernel(x, c, w_cond, b_cond, norm1_g, w_in, rwkv_mu, rwkv_w0, rwkv_w_up, rwkv_a0, rwkv_a_up, rwkv_g_up, rwkv_k_k, rwkv_k_a, rwkv_r_k, rwkv_lnx_w, rwkv_lnx_b, attn_sinks, attn_out_g, w_out, norm2_g, router_group, router_group_bias, router_expert, router_expert_bias, expert_w_gate, expert_w_up, expert_w_down, norm_f_g):
    b, s, d = x.shape
    depth = w_cond.shape[0]
    assert depth == 1, "the fused final norm assumes a single layer"
    l = 0
    ys, pos, route, x1, g2 = _layer(
        x, c, w_cond[l], b_cond[l], norm1_g[l], w_in[l], rwkv_mu[l], rwkv_w0[l], rwkv_w_up[l],
        rwkv_a0[l], rwkv_a_up[l], rwkv_g_up[l], rwkv_k_k[l], rwkv_k_a[l], rwkv_r_k[l],
        rwkv_lnx_w[l], rwkv_lnx_b[l], attn_sinks[l], attn_out_g[l], w_out[l], norm2_g[l],
        router_group[l], router_group_bias[l], router_expert[l], router_expert_bias[l],
        expert_w_gate[l], expert_w_up[l], expert_w_down[l])
    out = _combine(pos, route, ys, x1, g2, norm_f_g, s)
    return out.reshape(b, s, d)
```

```python
import functools
import math

import jax
import jax.numpy as jnp
from jax import lax
from jax.experimental import pallas as pl
from jax.experimental.pallas import tpu as pltpu

F32 = jnp.float32
BF16 = jnp.bfloat16

HEAD_DIM = 64
N_KV_HEADS = 8
WINDOW = 128
N_GROUPS = 8
EXPERTS_PER_GROUP = 8
N_EXPERTS = N_GROUPS * EXPERTS_PER_GROUP
RWKV_GN_EPS = 64e-5
NORM_EPS = 1e-6
RWKV_CHUNK = 64
RWKV_HEADS_PER_STEP = 4
RWKV_GROUPS_PER_STEP = 4
MOE_ROWS = 256
LANES = 128
MIB = 1024 * 1024


def _params(sem, vmem_mib=48):
    return pltpu.CompilerParams(dimension_semantics=sem, vmem_limit_bytes=vmem_mib * MIB)


def _round_up(n, m):
    return -(-n // m) * m


def _pick_tile(n, candidates):
    for t in candidates:
        if n % t == 0:
            return t
    raise ValueError(f"no tile in {candidates} divides {n}")


def _cond_kernel(c_ref, w_ref, b_ref, o_ref):
    c = c_ref[...]
    sc = (c * jax.nn.sigmoid(c)).astype(BF16)
    o_ref[...] = jnp.dot(sc, w_ref[...].astype(BF16), preferred_element_type=F32) + b_ref[...]


def _cond(c_pad, w, b):
    rows, d = c_pad.shape
    n = w.shape[1]
    tn = _pick_tile(n, (1024, 512, 256, 128))
    return pl.pallas_call(
        _cond_kernel,
        grid=(n // tn,),
        in_specs=[pl.BlockSpec((rows, d), lambda j: (0, 0)),
                  pl.BlockSpec((d, tn), lambda j: (0, j)),
                  pl.BlockSpec((1, tn), lambda j: (0, j))],
        out_specs=pl.BlockSpec((rows, tn), lambda j: (0, j)),
        out_shape=jax.ShapeDtypeStruct((rows, n), F32),
        compiler_params=_params(("arbitrary",)),
    )(c_pad, w, b)


def _norm_mod_kernel(x_ref, g_ref, sc_ref, sh_ref, o_ref):
    x = x_ref[...]
    y = x * lax.rsqrt(jnp.mean(x * x, axis=-1, keepdims=True) + NORM_EPS)
    y = y * g_ref[...]
    o_ref[...] = (y * (1.0 + sc_ref[0]) + sh_ref[0]).astype(o_ref.dtype)


def _norm_mod(x2d, g, scale, shift, seq, out_dtype):
    t, d = x2d.shape
    tm = _pick_tile(seq, (512, 256, 128))
    per_b = seq // tm
    return pl.pallas_call(
        _norm_mod_kernel,
        grid=(t // tm,),
        in_specs=[pl.BlockSpec((tm, d), lambda i: (i, 0)),
                  pl.BlockSpec((1, d), lambda i: (0, 0)),
                  pl.BlockSpec((1, 1, d), lambda i: (i // per_b, 0, 0)),
                  pl.BlockSpec((1, 1, d), lambda i: (i // per_b, 0, 0))],
        out_specs=pl.BlockSpec((tm, d), lambda i: (i, 0)),
        out_shape=jax.ShapeDtypeStruct((t, d), out_dtype),
        compiler_params=_params(("arbitrary",)),
    )(x2d, g.reshape(1, d), scale[:, None, :], shift[:, None, :])


def _mm_kernel(a_ref, b_ref, *refs):
    n_side = (len(refs) - 1) // 2
    o_ref = refs[n_side]
    o_ref[...] = jnp.dot(a_ref[...], b_ref[...], preferred_element_type=F32).astype(o_ref.dtype)
    for src, dst in zip(refs[:n_side], refs[n_side + 1:]):
        dst[...] = src[...].astype(dst.dtype)


def _matmul(a, b, out_dtype, side=()):
    m, k = a.shape
    n = b.shape[1]
    tm = _pick_tile(m, (1024, 512, 256, 128))
    tn = _pick_tile(n, (512, 256, 128))
    ni, nj = m // tm, n // tn
    if any(s.shape[0] > ni * nj for s in side):
        prod, _ = _matmul(a, b, out_dtype)
        return prod, [s.astype(BF16) for s in side]

    def side_spec(s):
        units = s.shape[0]
        return pl.BlockSpec((1,) + s.shape[1:],
                            lambda i, j: (jnp.minimum(i * nj + j, units - 1), 0, 0))

    vmem = 48 + 2 * sum(6 * math.prod(s.shape[1:]) for s in side) // MIB
    outs = pl.pallas_call(
        _mm_kernel,
        grid=(ni, nj),
        in_specs=[pl.BlockSpec((tm, k), lambda i, j: (i, 0)),
                  pl.BlockSpec((k, tn), lambda i, j: (0, j))] + [side_spec(s) for s in side],
        out_specs=[pl.BlockSpec((tm, tn), lambda i, j: (i, j))] + [side_spec(s) for s in side],
        out_shape=[jax.ShapeDtypeStruct((m, n), out_dtype)]
                  + [jax.ShapeDtypeStruct(s.shape, BF16) for s in side],
        compiler_params=_params(("arbitrary", "arbitrary"), vmem_mib=min(vmem, 58)),
    )(a, b, *side)
    return outs[0], outs[1:]


def _softplus(z):
    return jnp.maximum(z, 0.0) + jnp.log(1.0 + jnp.exp(-jnp.abs(z)))


def _rwkv_kernel(r_ref, k_ref, v_ref, l_ref, mur_ref, muk_ref, muv_ref, mul_ref,
                 w0_ref, a0_ref, kk_ref, ka_ref, rk_ref, lw_ref, lb_ref,
                 ww_ref, wa_ref, wg_ref, o_ref,
                 s_ref, pr_ref, pk_ref, pv_ref, pl_ref, *, n_batch, dw, da):
    c = RWKV_CHUNK
    nh = RWKV_HEADS_PER_STEP
    hw = nh * HEAD_DIM
    groups = r_ref.shape[-1] // hw
    lp = l_ref.shape[-1]
    chunk = pl.program_id(1)

    @pl.when(chunk == 0)
    def _():
        s_ref[...] = jnp.zeros_like(s_ref)
        pr_ref[...] = jnp.zeros_like(pr_ref)
        pk_ref[...] = jnp.zeros_like(pk_ref)
        pv_ref[...] = jnp.zeros_like(pv_ref)
        pl_ref[...] = jnp.zeros_like(pl_ref)

    row_w = lax.broadcasted_iota(jnp.int32, (c, hw), 0)
    col_w = lax.broadcasted_iota(jnp.int32, (c, hw), 1)
    s_idx = col_w % c
    strict = row_w > s_idx
    incl = row_w >= s_idx
    eye_w = (row_w == s_idx).astype(F32)
    merge_masks = [strict & (row_w // 2 == s_idx // 2)]
    size = 2
    while size < c:
        merge_masks.append(strict & (row_w // (2 * size) == s_idx // (2 * size))
                           & (row_w // size != s_idx // size))
        size *= 2
    bd_row = lax.broadcasted_iota(jnp.int32, (hw, hw), 0) // HEAD_DIM
    bd_col = lax.broadcasted_iota(jnp.int32, (hw, hw), 1) // HEAD_DIM
    bd_mask = bd_row == bd_col
    bd_ones = bd_mask.astype(BF16)
    tri = (lax.broadcasted_iota(jnp.int32, (c, c), 0)
           >= lax.broadcasted_iota(jnp.int32, (c, c), 1)).astype(BF16)
    row_l = lax.broadcasted_iota(jnp.int32, (c, lp), 0)
    col_l = lax.broadcasted_iota(jnp.int32, (c, lp), 1)

    def shift_mix(x, prev, mu, row):
        shifted = jnp.where(row == 0, prev, pltpu.roll(x, 1, axis=0))
        return x + (shifted - x) * mu

    def split2(x):
        hi = x.astype(BF16)
        lo = (x - hi.astype(F32)).astype(BF16)
        return hi, lo

    chains = [(bi, gi) for gi in range(groups) for bi in range(n_batch)]
    nb = range(len(chains))

    def cols(x, gi):
        return x[..., gi * hw:(gi + 1) * hw]

    def vec(ref, ci):
        return cols(ref[...], chains[ci][1])

    def rows_split(x):
        return [x[i * c:(i + 1) * c] for i in range(x.shape[0] // c)]

    def chain_split(x):
        return [cols(x[bi * c:(bi + 1) * c], gi) for bi, gi in chains]

    def seg_sum(xs):
        stacked = jnp.concatenate([x.astype(BF16) for x in xs], axis=0)
        return rows_split(jnp.dot(stacked, bd_ones, preferred_element_type=F32))

    def bd(x):
        return jnp.where(bd_mask, jnp.tile(x, (nh, 1)), 0.0).astype(BF16)

    def mm(a, b):
        return jnp.dot(a.astype(BF16), b, preferred_element_type=F32)

    def mm_nt(a, b):
        return lax.dot_general(a.astype(BF16), b, (((1,), (1,)), ((), ())),
                               preferred_element_type=F32)

    def mixed(x_ref, p_ref, mu_ref):
        return [shift_mix(cols(x_ref[bi], gi).astype(F32), cols(p_ref[bi], gi),
                          cols(mu_ref[...], gi), row_w) for bi, gi in chains]

    xr = mixed(r_ref, pr_ref, mur_ref)
    xk = mixed(k_ref, pk_ref, muk_ref)
    xv = mixed(v_ref, pv_ref, muv_ref)
    xl = [shift_mix(l_ref[bi].astype(F32), pl_ref[bi], mul_ref[...], row_l) for bi in range(n_batch)]
    states = [s_ref[bi, gi] for bi, gi in chains]

    act = jnp.concatenate(
        [jnp.where(col_l < dw, jnp.tanh(x), jnp.where(col_l < dw + da, x, jax.nn.sigmoid(x)))
         for x in xl], axis=0).astype(BF16)
    kw = _round_up(dw, LANES)
    ka = _round_up(dw + da, LANES)
    lora_w = chain_split(jnp.dot(act[:, :kw], ww_ref[:kw, :], preferred_element_type=F32))
    lora_a = chain_split(jnp.dot(act[:, :ka], wa_ref[:ka, :], preferred_element_type=F32))
    gate = chain_split(jnp.dot(act, wg_ref[...], preferred_element_type=F32))

    e = [jnp.exp(-_softplus(-(vec(w0_ref, ci) + lora_w[ci])) - 0.5) for ci in nb]
    lr = [jax.nn.sigmoid(vec(a0_ref, ci) + lora_a[ci]) for ci in nb]
    kk = [xk[ci] * vec(kk_ref, ci) for ci in nb]
    kk_ss = seg_sum([x * x for x in kk])
    kk = [x / jnp.maximum(jnp.sqrt(ss), 1e-12) for x, ss in zip(kk, kk_ss)]
    kmod = [xk[ci] * (1.0 + (lr[ci] - 1.0) * vec(ka_ref, ci)) for ci in nb]

    cum = []
    for ei in e:
        e_hi, e_lo = split2(ei)
        cum2 = jnp.dot(tri, jnp.concatenate([e_hi, e_lo], axis=1), preferred_element_type=F32)
        cum.append(cum2[:, :hw] + cum2[:, hw:])
    dec = [jnp.exp(-x) for x in cum]
    inv = [jnp.exp(x) for x in cum]
    at = [-kk[bi] * jnp.exp(e[bi] - cum[bi]) for bi in nb]
    rt = [xr[bi] * dec[bi] for bi in nb]
    bt = [kk[bi] * lr[bi] * inv[bi] for bi in nb]
    kt = [kmod[bi] * inv[bi] for bi in nb]

    ar = [jnp.concatenate([at[bi], rt[bi]], axis=0).astype(BF16) for bi in nb]
    g = [mm_nt(ar[bi], jnp.concatenate([bd(bt[bi]), bd(kt[bi])], axis=0)) for bi in nb]
    m_ab = [jnp.where(strict, x[:c, :hw], 0.0) for x in g]
    m_ak = [jnp.where(strict, x[:c, hw:], 0.0) for x in g]
    m_rb = [jnp.where(incl, x[c:, :hw], 0.0) for x in g]
    m_rk = [jnp.where(incl, x[c:, hw:], 0.0) for x in g]

    t_inv = [eye_w + jnp.where(merge_masks[0], m, 0.0) for m in m_ab]
    for mask in merge_masks[1:]:
        z = [mm(jnp.where(mask, m_ab[bi], 0.0), bd(t_inv[bi])) for bi in nb]
        t_inv = [t_inv[bi] + mm(t_inv[bi], bd(z[bi])) for bi in nb]

    s_bf = [s.astype(BF16) for s in states]
    bd_v = [bd(x) for x in xv]
    ar_s = [mm_nt(ar[bi], s_bf[bi]) for bi in nb]
    mv = [mm(jnp.concatenate([m_ak[bi], m_rk[bi]], axis=0), bd_v[bi]) for bi in nb]
    u = [mm(t_inv[bi], bd(ar_s[bi][:c] + mv[bi][:c])) for bi in nb]
    y = [ar_s[bi][c:] + mm(m_rb[bi], bd(u[bi])) + mv[bi][c:] for bi in nb]
    upd = [lax.dot_general(jnp.concatenate([u[bi], xv[bi]], axis=0).astype(BF16),
                           jnp.concatenate([bt[bi], kt[bi]], axis=0).astype(BF16),
                           (((0,), (0,)), ((), ())), preferred_element_type=F32) for bi in nb]

    mean = [m * (1.0 / HEAD_DIM) for m in seg_sum(y)]
    dy = [y[bi] - mean[bi] for bi in nb]
    var = [v * (1.0 / HEAD_DIM) for v in seg_sum([d * d for d in dy])]
    rk_sum = seg_sum([xr[ci] * kmod[ci] * vec(rk_ref, ci) for ci in nb])

    for ci, (bi, gi) in enumerate(chains):
        yn = dy[ci] * lax.rsqrt(var[ci] + RWKV_GN_EPS) * vec(lw_ref, ci) + vec(lb_ref, ci)
        o_ref[bi, :, gi * hw:(gi + 1) * hw] = ((yn + rk_sum[ci] * xv[ci]) * gate[ci]).astype(o_ref.dtype)
        s_ref[bi, gi] = (states[ci] + jnp.where(bd_mask, upd[ci], 0.0)) * dec[ci][c - 1:c]
    def last_row(x_ref, bi):
        return x_ref[bi, c - 16:c, :].astype(F32)[15:16]

    for bi in range(n_batch):
        pr_ref[bi] = last_row(r_ref, bi)
        pk_ref[bi] = last_row(k_ref, bi)
        pv_ref[bi] = last_row(v_ref, bi)
        pl_ref[bi] = last_row(l_ref, bi)


def _rwkv(proj, mu, w0, a0, k_k, k_a, r_k, lnx_w, lnx_b, ww, wa, wg, *, d_rwkv, lp, dw, da):
    b, s, _ = proj.shape
    c = RWKV_CHUNK
    hw1 = RWKV_HEADS_PER_STEP * HEAD_DIM
    groups = RWKV_GROUPS_PER_STEP if d_rwkv % (RWKV_GROUPS_PER_STEP * hw1) == 0 else 1
    hw = groups * hw1
    assert c == HEAD_DIM and d_rwkv % hw == 0 and s % c == 0 and (3 * d_rwkv) % lp == 0
    nhg = d_rwkv // hw
    lblk = (3 * d_rwkv) // lp

    def act_spec(off):
        return pl.BlockSpec((b, c, hw), lambda h, t: (0, t, off + h))

    def vec_spec(off):
        return pl.BlockSpec((1, hw), lambda h, t: (0, off + h))

    vec = vec_spec(0)
    lora_spec = pl.BlockSpec((lp, hw), lambda h, t: (0, h))
    kern = functools.partial(_rwkv_kernel, n_batch=b, dw=dw, da=da)
    return pl.pallas_call(
        kern,
        grid=(nhg, s // c),
        in_specs=[act_spec(0), act_spec(nhg), act_spec(2 * nhg),
                  pl.BlockSpec((b, c, lp), lambda h, t: (0, t, lblk)),
                  vec_spec(0), vec_spec(nhg), vec_spec(2 * nhg),
                  pl.BlockSpec((1, lp), lambda h, t: (0, lblk)),
                  vec, vec, vec, vec, vec, vec, vec,
                  lora_spec, lora_spec, lora_spec],
        out_specs=pl.BlockSpec((b, c, hw), lambda h, t: (0, t, h)),
        out_shape=jax.ShapeDtypeStruct((b, s, d_rwkv), BF16),
        scratch_shapes=[pltpu.VMEM((b, groups, hw1, hw1), F32),
                        pltpu.VMEM((b, 1, hw), F32), pltpu.VMEM((b, 1, hw), F32),
                        pltpu.VMEM((b, 1, hw), F32), pltpu.VMEM((b, 1, lp), F32)],
        compiler_params=_params(("arbitrary", "arbitrary")),
    )(proj, proj, proj, proj, mu, mu, mu, mu,
      w0, a0, k_k, k_a, r_k, lnx_w, lnx_b, ww, wa, wg)


def _attn_kernel(sink_ref, slope_ref, q_ref, kp_ref, kc_ref, vp_ref, vc_ref, og_ref, o_ref, *, gqa, qb):
    step = pl.program_id(2)
    pair = pl.program_id(1)
    wd = WINDOW
    qi = lax.broadcasted_iota(jnp.int32, (wd, 2 * wd), 0)
    kj = lax.broadcasted_iota(jnp.int32, (wd, 2 * wd), 1)
    dist = qi + wd - kj
    band = (dist >= 0) & (dist < wd)
    valid = [band & ((kj >= wd) | (step > 0))] + [band] * (qb - 1)
    dist_f = dist.astype(F32)
    scale = HEAD_DIM ** -0.5
    k_all = jnp.concatenate([kp_ref[...], kc_ref[...]], axis=0)
    v_all = jnp.concatenate([vp_ref[...], vc_ref[...]], axis=0)
    heads = range(2 * gqa)
    units = [(sb, hl) for sb in range(qb) for hl in heads]
    nu = range(len(units))

    def kv(x, sb, hl):
        kvh = hl // gqa
        return x[sb * wd:(sb + 2) * wd, kvh * HEAD_DIM:(kvh + 1) * HEAD_DIM]

    slopes = [slope_ref[pair * (2 * gqa) + hl] for _, hl in units]
    sinks = [sink_ref[pair * (2 * gqa) + hl] for _, hl in units]
    s = [lax.dot_general(q_ref[sb * wd:(sb + 1) * wd, hl * HEAD_DIM:(hl + 1) * HEAD_DIM],
                         kv(k_all, sb, hl), (((1,), (1,)), ((), ())), preferred_element_type=F32)
         for sb, hl in units]
    s = [jnp.where(valid[units[u][0]], s[u] * scale - slopes[u] * dist_f, -jnp.inf) for u in nu]
    m = [jnp.maximum(jnp.max(s[u], axis=-1, keepdims=True), sinks[u]) for u in nu]
    p = [jnp.exp(s[u] - m[u]) for u in nu]
    denom = [jnp.sum(p[u], axis=-1, keepdims=True) + jnp.exp(sinks[u] - m[u]) for u in nu]
    o = [jnp.dot(p[u].astype(BF16), kv(v_all, *units[u]), preferred_element_type=F32) / denom[u]
         for u in nu]
    o = [x * lax.rsqrt(jnp.mean(x * x, axis=-1, keepdims=True) + NORM_EPS) for x in o]
    nh = len(heads)
    o_all = jnp.concatenate([jnp.concatenate(o[sb * nh:(sb + 1) * nh], axis=-1) for sb in range(qb)],
                            axis=0) * og_ref[...]
    o_ref[...] = o_all.astype(o_ref.dtype)


def _attention(qkv, sinks, out_g, *, batch, seq, d_attn):
    d_kv = N_KV_HEADS * HEAD_DIM
    n_q_heads = d_attn // HEAD_DIM
    gqa = n_q_heads // N_KV_HEADS
    qw = 2 * gqa * HEAD_DIM
    n_pairs = N_KV_HEADS // 2
    nb = seq // WINDOW
    assert qw % LANES == 0 and d_attn % LANES == 0
    k_off = d_attn // LANES
    v_off = (d_attn + d_kv) // LANES

    qb = 2 if nb % 2 == 0 else 1
    ns = nb // qb

    def cur(off):
        return pl.BlockSpec((qb * WINDOW, LANES), lambda b, p, i, *_: (b * ns + i, off + p))

    def prev(off):
        return pl.BlockSpec((WINDOW, LANES),
                            lambda b, p, i, *_: (b * nb + jnp.maximum(qb * i - 1, 0), off + p))

    slopes = jnp.exp2(-8.0 * jnp.arange(1, n_q_heads + 1, dtype=F32) / n_q_heads)
    grid_spec = pltpu.PrefetchScalarGridSpec(
        num_scalar_prefetch=2,
        grid=(batch, n_pairs, ns),
        in_specs=[pl.BlockSpec((qb * WINDOW, qw), lambda b, p, i, *_: (b * ns + i, p)),
                  prev(k_off), cur(k_off), prev(v_off), cur(v_off),
                  pl.BlockSpec((1, qw), lambda b, p, i, *_: (0, p))],
        out_specs=pl.BlockSpec((qb * WINDOW, qw), lambda b, p, i, *_: (b * ns + i, p)),
    )
    return pl.pallas_call(
        functools.partial(_attn_kernel, gqa=gqa, qb=qb),
        grid_spec=grid_spec,
        out_shape=jax.ShapeDtypeStruct((batch * seq, d_attn), BF16),
        compiler_params=_params(("arbitrary", "arbitrary", "arbitrary")),
    )(sinks, slopes, qkv, qkv, qkv, qkv, qkv, out_g.reshape(1, d_attn))


def _outproj_kernel(yr_ref, ya_ref, w1_ref, w2_ref, x_ref, g_ref, o_ref):
    acc = jnp.dot(yr_ref[...], w1_ref[...], preferred_element_type=F32)
    acc = acc + jnp.dot(ya_ref[...], w2_ref[...], preferred_element_type=F32)
    o_ref[...] = x_ref[...] + g_ref[0] * acc


def _outproj(y_rwkv, y_attn, w_out, x2d, gate, seq):
    t, d = x2d.shape
    k1 = y_rwkv.shape[1]
    k2 = y_attn.shape[1]
    assert k1 == k2
    tm = _pick_tile(seq, (1024, 512, 256, 128))
    tn = _pick_tile(d, (512, 256, 128))
    per_b = seq // tm
    return pl.pallas_call(
        _outproj_kernel,
        grid=(t // tm, d // tn),
        in_specs=[pl.BlockSpec((tm, k1), lambda i, j: (i, 0)),
                  pl.BlockSpec((tm, k2), lambda i, j: (i, 0)),
                  pl.BlockSpec((k1, tn), lambda i, j: (0, j)),
                  pl.BlockSpec((k2, tn), lambda i, j: (1, j)),
                  pl.BlockSpec((tm, tn), lambda i, j: (i, j)),
                  pl.BlockSpec((1, 1, tn), lambda i, j: (i // per_b, 0, j))],
        out_specs=pl.BlockSpec((tm, tn), lambda i, j: (i, j)),
        out_shape=jax.ShapeDtypeStruct((t, d), F32),
        compiler_params=_params(("arbitrary", "arbitrary")),
    )(y_rwkv, y_attn, w_out, w_out, x2d, gate[:, None, :])


U32 = jnp.uint32


def _pack_pairs(x):
    n = x.shape[1] // 2
    lo = lax.bitcast_convert_type(x[:, :n].astype(BF16).astype(F32), U32)
    hi = lax.bitcast_convert_type(x[:, n:].astype(BF16).astype(F32), U32)
    return (lo >> 16) | hi


def _unpack_pairs(w):
    lo = lax.bitcast_convert_type(w << 16, F32)
    hi = lax.bitcast_convert_type(w & jnp.uint32(0xFFFF0000), F32)
    return jnp.concatenate([lo, hi], axis=-1)


def _slab_pitch(ns):
    return ns + 8 if ns % 16 == 0 else _round_up(ns, 8)


def _to_slabs(ref, x):
    rows, d = x.shape
    ns = d // LANES
    pitch = _slab_pitch(ns)
    ref[...] = jnp.zeros_like(ref)
    for s in range(ns):
        ref[pl.ds(s, rows, stride=pitch), :] = x[:, s * LANES:(s + 1) * LANES]


def _from_slabs(ref, ns):
    pitch = _slab_pitch(ns)
    rows = ref.shape[0] // pitch
    return jnp.concatenate([ref[pl.ds(s, rows, stride=pitch), :] for s in range(ns)], axis=-1)


def _norm_route_kernel(x_ref, g_ref, sc_ref, sh_ref, w_ref, b_ref, u_ref, o_ref):
    x = x_ref[...]
    y = x * lax.rsqrt(jnp.mean(x * x, axis=-1, keepdims=True) + NORM_EPS)
    y = y * g_ref[...]
    y = y * (1.0 + sc_ref[0]) + sh_ref[0]
    _to_slabs(u_ref, _pack_pairs(y))
    logits = jnp.dot(y.astype(BF16), w_ref[...], preferred_element_type=F32) + b_ref[...]
    col = lax.broadcasted_iota(jnp.int32, logits.shape, 1).astype(F32)
    ng = float(N_GROUPS)
    epg = float(EXPERTS_PER_GROUP)
    big = 1e9
    is_g = col < ng
    lg = jnp.where(is_g, logits, -jnp.inf)
    mg = jnp.max(lg, axis=-1, keepdims=True)
    gidx = jnp.min(jnp.where(lg == mg, col, big), axis=-1, keepdims=True)
    zg = jnp.sum(jnp.where(is_g, jnp.exp(lg - mg), 0.0), axis=-1, keepdims=True)
    g_w = 1.0 / zg
    lo = ng + gidx * epg
    in_grp = (col >= lo) & (col < lo + epg)
    le = jnp.where(in_grp, logits, -jnp.inf)
    m1 = jnp.max(le, axis=-1, keepdims=True)
    i1 = jnp.min(jnp.where(le == m1, col, big), axis=-1, keepdims=True)
    le2 = jnp.where(col == i1, -jnp.inf, le)
    m2 = jnp.max(le2, axis=-1, keepdims=True)
    i2 = jnp.min(jnp.where(le2 == m2, col, big), axis=-1, keepdims=True)
    e2 = jnp.exp(m2 - m1)
    w1 = 1.0 / (1.0 + e2)
    w2 = e2 / (1.0 + e2)
    out = jnp.where(col == 0.0, i1 - ng,
                    jnp.where(col == 1.0, i2 - ng,
                              jnp.where(col == 2.0, g_w * w1,
                                        jnp.where(col == 3.0, g_w * w2, 0.0))))
    o_ref[...] = out


def _norm_route(x2d, g, scale, shift, w_r, b_r, seq):
    t, d = x2d.shape
    tm = _pick_tile(seq, (256, 128))
    per_b = seq // tm
    ns = d // (2 * LANES)
    return pl.pallas_call(
        _norm_route_kernel,
        grid=(t // tm,),
        in_specs=[pl.BlockSpec((tm, d), lambda i: (i, 0)),
                  pl.BlockSpec((1, d), lambda i: (0, 0)),
                  pl.BlockSpec((1, 1, d), lambda i: (i // per_b, 0, 0)),
                  pl.BlockSpec((1, 1, d), lambda i: (i // per_b, 0, 0)),
                  pl.BlockSpec((d, LANES), lambda i: (0, 0)),
                  pl.BlockSpec((1, LANES), lambda i: (0, 0))],
        out_specs=[pl.BlockSpec((tm * _slab_pitch(ns), LANES), lambda i: (i, 0)),
                   pl.BlockSpec((tm, LANES), lambda i: (i, 0))],
        out_shape=[jax.ShapeDtypeStruct((t * _slab_pitch(ns), LANES), U32),
                   jax.ShapeDtypeStruct((t, LANES), F32)],
        compiler_params=_params(("arbitrary",)),
    )(x2d, g.reshape(1, d), scale[:, None, :], shift[:, None, :], w_r, b_r)


GATHER_GROUP = 8


def _gather_rows(idx_ref, n_rows, src_hbm, dst, sem, col0=0, unroll=False, slab_rows=0):
    pitch = _slab_pitch(slab_rows) if slab_rows else 1
    size = slab_rows if slab_rows else 1

    def issue(r):
        tok = idx_ref[0, 0, col0 + r]
        if slab_rows:
            src = src_hbm.at[pl.ds(pl.multiple_of(tok * pitch, 8), size)]
            dst_r = dst.at[pl.ds(pl.multiple_of(r * pitch, 8), size)]
        else:
            src, dst_r = src_hbm.at[pl.ds(tok, 1)], dst.at[pl.ds(r, 1)]
        pltpu.make_async_copy(src, dst_r, sem).start()

    if unroll is True:
        for r in range(n_rows):
            issue(r)
    elif isinstance(n_rows, int):
        def body(r, carry):
            issue(r)
            return carry
        lax.fori_loop(0, n_rows, body, 0, unroll=max(int(unroll), 1))
    else:
        def group(g, carry):
            for k in range(GATHER_GROUP):
                issue(g * GATHER_GROUP + k)
            return carry
        lax.fori_loop(0, n_rows // GATHER_GROUP, group, 0)


def _wait_rows(n_rows, src_hbm, dst, sem):
    if not isinstance(n_rows, int):
        n_rows = pl.multiple_of(n_rows, 8)
    pltpu.make_async_copy(src_hbm.at[pl.ds(0, n_rows)], dst.at[pl.ds(0, n_rows)], sem).wait()


def _expert_kernel(be_ref, nu_ref, nv_ref, tok_ref, tokn_ref, u_hbm, wg_ref, wu_ref, wd_ref,
                   o_ref, xbuf_a, xbuf_b, sem):
    i = pl.program_id(0)
    last = pl.num_programs(0) - 1
    n_used = nu_ref[0]
    rows, n_words = o_ref.shape
    ns = n_words // LANES
    n_next = jnp.where(i < last, nv_ref[jnp.minimum(i + 1, last)], 0)

    @pl.when(i == 0)
    def _():
        xbuf_a[...] = jnp.zeros_like(xbuf_a)
        xbuf_b[...] = jnp.zeros_like(xbuf_b)
        _gather_rows(tok_ref, nv_ref[0], u_hbm, xbuf_a, sem.at[0], slab_rows=ns)

    def step(cur, cur_sem, nxt, nxt_sem):
        @pl.when(i < n_used)
        def _():
            _wait_rows(nv_ref[i] * ns, u_hbm, cur, cur_sem)
            _gather_rows(tokn_ref, n_next, u_hbm, nxt, nxt_sem, slab_rows=ns)
            xb = _unpack_pairs(_from_slabs(cur, ns)).astype(BF16)
            gate = jnp.dot(xb, wg_ref[0], preferred_element_type=F32)
            up = jnp.dot(xb, wu_ref[0], preferred_element_type=F32)
            h = (gate * jax.nn.sigmoid(gate) * up).astype(BF16)
            o_ref[...] = _pack_pairs(jnp.dot(h, wd_ref[0], preferred_element_type=F32))

        @pl.when(i >= n_used)
        def _():
            o_ref[...] = jnp.zeros_like(o_ref)

    @pl.when(i % 2 == 0)
    def _():
        step(xbuf_a, sem.at[0], xbuf_b, sem.at[1])

    @pl.when(i % 2 == 1)
    def _():
        step(xbuf_b, sem.at[1], xbuf_a, sem.at[0])


def _experts(block_expert, n_used, n_valid, row_tok, u2, wg, wu, wd):
    d = wg.shape[1]
    n_words = d // 2
    pitch = _slab_pitch(n_words // LANES)
    nb = block_expert.shape[0]
    rows = MOE_ROWS
    de = wg.shape[-1]
    tok3 = row_tok.reshape(nb, 1, rows)
    grid_spec = pltpu.PrefetchScalarGridSpec(
        num_scalar_prefetch=3,
        grid=(nb,),
        in_specs=[pl.BlockSpec((1, 1, rows), lambda i, be, nu, nv: (i, 0, 0), memory_space=pltpu.SMEM),
                  pl.BlockSpec((1, 1, rows), lambda i, be, nu, nv: (jnp.minimum(i + 1, nb - 1), 0, 0),
                               memory_space=pltpu.SMEM),
                  pl.BlockSpec(memory_space=pl.ANY),
                  pl.BlockSpec((1, d, de), lambda i, be, nu, nv: (be[i], 0, 0)),
                  pl.BlockSpec((1, d, de), lambda i, be, nu, nv: (be[i], 0, 0)),
                  pl.BlockSpec((1, de, d), lambda i, be, nu, nv: (be[i], 0, 0))],
        out_specs=pl.BlockSpec((rows, n_words), lambda i, be, nu, nv: (i, 0)),
        scratch_shapes=[pltpu.VMEM((rows * pitch, LANES), U32),
                        pltpu.VMEM((rows * pitch, LANES), U32),
                        pltpu.SemaphoreType.DMA((2,))],
    )
    return pl.pallas_call(
        _expert_kernel,
        grid_spec=grid_spec,
        out_shape=jax.ShapeDtypeStruct((nb * rows, n_words), U32),
        compiler_params=_params(("arbitrary",), vmem_mib=58),
    )(block_expert, n_used, n_valid, tok3, tok3, u2, wg, wu, wd)


def _combine_kernel(pos_ref, posn_ref, ys_hbm, r_ref, x_ref, g_ref, ng_ref, o_ref, buf_a, buf_b, sem):
    i = pl.program_id(0)
    last = pl.num_programs(0) - 1
    tc = x_ref.shape[0]

    def gather(idx_ref, buf, s, unroll):
        _gather_rows(idx_ref, tc, ys_hbm, buf.at[0], s, 0, unroll)
        _gather_rows(idx_ref, tc, ys_hbm, buf.at[1], s, tc, unroll)

    def wait(buf, s):
        _wait_rows(tc, ys_hbm, buf.at[0], s)
        _wait_rows(tc, ys_hbm, buf.at[1], s)

    @pl.when(i == 0)
    def _():
        gather(pos_ref, buf_a, sem.at[0], False)

    def step(cur, cur_sem, nxt, nxt_sem):
        wait(cur, cur_sem)
        gather(posn_ref, nxt, nxt_sem, True)
        moe = _unpack_pairs(cur[0]) * r_ref[:, 2:3] + _unpack_pairs(cur[1]) * r_ref[:, 3:4]
        x = x_ref[...] + g_ref[0] * moe
        y = x * lax.rsqrt(jnp.mean(x * x, axis=-1, keepdims=True) + NORM_EPS)
        o_ref[...] = y * ng_ref[...]

        @pl.when(i == last)
        def _():
            wait(nxt, nxt_sem)

    @pl.when(i % 2 == 0)
    def _():
        step(buf_a, sem.at[0], buf_b, sem.at[1])

    @pl.when(i % 2 == 1)
    def _():
        step(buf_b, sem.at[1], buf_a, sem.at[0])


def _combine(pos, route, ys, x1, gate, norm_g, seq):
    t, d = x1.shape
    tc = _pick_tile(seq, (128,))
    n = t // tc
    per_b = seq // tc
    pos3 = pos.reshape(n, tc, 2).transpose(0, 2, 1).reshape(n, 1, 2 * tc)
    return pl.pallas_call(
        _combine_kernel,
        grid=(n,),
        in_specs=[pl.BlockSpec((1, 1, 2 * tc), lambda i: (i, 0, 0), memory_space=pltpu.SMEM),
                  pl.BlockSpec((1, 1, 2 * tc), lambda i: (jnp.minimum(i + 1, n - 1), 0, 0),
                               memory_space=pltpu.SMEM),
                  pl.BlockSpec(memory_space=pl.ANY),
                  pl.BlockSpec((tc, LANES), lambda i: (i, 0)),
                  pl.BlockSpec((tc, d), lambda i: (i, 0)),
                  pl.BlockSpec((1, 1, d), lambda i: (i // per_b, 0, 0)),
                  pl.BlockSpec((1, d), lambda i: (0, 0))],
        out_specs=pl.BlockSpec((tc, d), lambda i: (i, 0)),
        out_shape=jax.ShapeDtypeStruct((t, d), F32),
        scratch_shapes=[pltpu.VMEM((2, tc, d // 2), U32), pltpu.VMEM((2, tc, d // 2), U32),
                        pltpu.SemaphoreType.DMA((2,))],
        compiler_params=_params(("arbitrary",)),
    )(pos3, pos3, ys, route, x1, gate[:, None, :], norm_g.reshape(1, d))


def _dispatch(expert_id, n_tok):
    m = expert_id.shape[0]
    rows = MOE_ROWS
    nb = -(-m // rows) + N_EXPERTS
    onehot = (expert_id[:, None] == jnp.arange(N_EXPERTS, dtype=jnp.int32)[None, :]).astype(jnp.int32)
    csum = jnp.cumsum(onehot, axis=0)
    rank = jnp.take_along_axis(csum, expert_id[:, None], axis=1)[:, 0] - 1
    counts = csum[-1]
    padded = ((counts + rows - 1) // rows) * rows
    pends = jnp.cumsum(padded)
    pstarts = pends - padded
    dest = (pstarts[expert_id] + rank).astype(jnp.int32)
    tok = jnp.arange(m, dtype=jnp.int32) // (m // n_tok)
    row_tok = jnp.zeros((nb * rows,), jnp.int32).at[dest].set(tok)
    n_used = (pends[-1] // rows).astype(jnp.int32)
    all_blk = jnp.arange(nb, dtype=jnp.int32)
    blk = jnp.minimum(all_blk, n_used - 1)
    block_expert = jnp.minimum(jnp.searchsorted(pends, blk * rows, side='right'),
                               N_EXPERTS - 1).astype(jnp.int32)
    seg_end = (pstarts + counts)[block_expert]
    n_valid = jnp.where(all_blk < n_used, jnp.clip(seg_end - all_blk * rows, 0, rows), 0)
    n_valid = ((n_valid + GATHER_GROUP - 1) // GATHER_GROUP) * GATHER_GROUP
    return block_expert, n_used.reshape(1), n_valid.astype(jnp.int32), row_tok, dest


def _layer(x, c, w_cond, b_cond, norm1_g, w_in, rwkv_mu, rwkv_w0, rwkv_w_up, rwkv_a0, rwkv_a_up,
           rwkv_g_up, rwkv_k_k, rwkv_k_a, rwkv_r_k, rwkv_lnx_w, rwkv_lnx_b, attn_sinks, attn_out_g,
           w_out, norm2_g, router_group, router_group_bias, router_expert, router_expert_bias,
           expert_w_gate, expert_w_up, expert_w_down):
    b, s, d = x.shape
    t = b * s
    d_rwkv = rwkv_w0.shape[-1]
    d_attn = attn_out_g.shape[-1]
    dw, da, dg = rwkv_w_up.shape[0], rwkv_a_up.shape[0], rwkv_g_up.shape[0]
    n_lora = dw + da + dg
    lp = _round_up(n_lora, LANES)
    while (3 * d_rwkv) % lp:
        lp += LANES
    n_rwkv_cols = 3 * d_rwkv + n_lora

    c_pad = jnp.zeros((_round_up(b, 8), d), F32).at[:b].set(c)
    mod = _cond(c_pad, w_cond, b_cond.reshape(1, -1))[:b]
    sh1, sc1, g1, sh2, sc2, g2 = jnp.split(mod, 6, axis=-1)

    x2d = x.reshape(t, d)
    u1 = _norm_mod(x2d, norm1_g, sc1, sh1, s, BF16)

    zpad = jnp.zeros((d, lp - n_lora), F32)
    w_r = jnp.concatenate([w_in[:, :n_rwkv_cols], zpad], axis=1).astype(BF16)
    w_a = w_in[:, n_rwkv_cols:].astype(BF16)
    n_exp, _, d_exp = expert_w_gate.shape
    halves = lambda w: w.reshape(2 * n_exp, d // 2, d_exp)
    proj_r, (wg_b, wu_b) = _matmul(u1, w_r, BF16, side=(halves(expert_w_gate), halves(expert_w_up)))
    proj_a, (wd_b,) = _matmul(u1, w_a, BF16, side=(expert_w_down,))
    wg_b = wg_b.reshape(expert_w_gate.shape)
    wu_b = wu_b.reshape(expert_w_up.shape)

    def lora_rows(wmat, off):
        return jnp.zeros((lp, d_rwkv), F32).at[off:off + wmat.shape[0]].set(wmat).astype(BF16)

    mu = jnp.concatenate([rwkv_mu, jnp.zeros((lp - n_lora,), F32)]).reshape(1, -1)
    row = lambda v: v.reshape(1, d_rwkv)
    y_rwkv = _rwkv(proj_r.reshape(b, s, -1), mu, row(rwkv_w0), row(rwkv_a0), row(rwkv_k_k),
                   row(rwkv_k_a), row(rwkv_r_k), row(rwkv_lnx_w), row(rwkv_lnx_b),
                   lora_rows(rwkv_w_up, 0), lora_rows(rwkv_a_up, dw), lora_rows(rwkv_g_up, dw + da),
                   d_rwkv=d_rwkv, lp=lp, dw=dw, da=da).reshape(t, d_rwkv)

    y_attn = _attention(proj_a, attn_sinks, attn_out_g, batch=b, seq=s, d_attn=d_attn)

    x1 = _outproj(y_rwkv, y_attn, w_out.astype(BF16), x2d, g1, s)

    n_r = N_GROUPS + N_EXPERTS
    w_router = jnp.concatenate([router_group, router_expert, jnp.zeros((d, LANES - n_r), F32)],
                               axis=1).astype(BF16)
    b_router = jnp.concatenate([router_group_bias, router_expert_bias,
                                jnp.zeros((LANES - n_r,), F32)]).reshape(1, LANES)
    u2, route = _norm_route(x1, norm2_g, sc2, sh2, w_router, b_router, s)
    expert_id = route[:, :2].astype(jnp.int32).reshape(-1)
    block_expert, n_used, n_valid, row_tok, dest = _dispatch(expert_id, t)
    ys = _experts(block_expert, n_used, n_valid, row_tok, u2, wg_b, wu_b, wd_b)
    return ys, dest.reshape(t, 2), route, x1, g2


def kernel(x, c, w_cond, b_cond, norm1_g, w_in, rwkv_mu, rwkv_w0, rwkv_w_up, rwkv_a0, rwkv_a_up, rwkv_g_up, rwkv_k_k, rwkv_k_a, rwkv_r_k, rwkv_lnx_w, rwkv_lnx_b, attn_sinks, attn_out_g, w_out, norm2_g, router_group, router_group_bias, router_expert, router_expert_bias, expert_w_gate, expert_w_up, expert_w_down, norm_f_g):
    b, s, d = x.shape
    depth = w_cond.shape[0]
    assert depth == 1, "the fused final norm assumes a single layer"
    l = 0
    ys, pos, route, x1, g2 = _layer(
        x, c, w_cond[l], b_cond[l], norm1_g[l], w_in[l], rwkv_mu[l], rwkv_w0[l], rwkv_w_up[l],
        rwkv_a0[l], rwkv_a_up[l], rwkv_g_up[l], rwkv_k_k[l], rwkv_k_a[l], rwkv_r_k[l],
        rwkv_lnx_w[l], rwkv_lnx_b[l], attn_sinks[l], attn_out_g[l], w_out[l], norm2_g[l],
        router_group[l], router_group_bias[l], router_expert[l], router_expert_bias[l],
        expert_w_gate[l], expert_w_up[l], expert_w_down[l])
    out = _combine(pos, route, ys, x1, g2, norm_f_g, s)
    return out.reshape(b, s, d)
```

```python
import functools
import math

import jax
import jax.numpy as jnp
from jax import lax
from jax.experimental import pallas as pl
from jax.experimental.pallas import tpu as pltpu

F32 = jnp.float32
BF16 = jnp.bfloat16

HEAD_DIM = 64
N_KV_HEADS = 8
WINDOW = 128
N_GROUPS = 8
EXPERTS_PER_GROUP = 8
N_EXPERTS = N_GROUPS * EXPERTS_PER_GROUP
RWKV_GN_EPS = 64e-5
NORM_EPS = 1e-6
RWKV_CHUNK = 64
RWKV_HEADS_PER_STEP = 4
RWKV_GROUPS_PER_STEP = 4
MOE_ROWS = 256
LANES = 128
MIB = 1024 * 1024


def _params(sem, vmem_mib=48):
    return pltpu.CompilerParams(dimension_semantics=sem, vmem_limit_bytes=vmem_mib * MIB)


def _round_up(n, m):
    return -(-n // m) * m


def _pick_tile(n, candidates):
    for t in candidates:
        if n % t == 0:
            return t
    raise ValueError(f"no tile in {candidates} divides {n}")


def _cond_kernel(c_ref, w_ref, b_ref, o_ref):
    c = c_ref[...]
    sc = (c * jax.nn.sigmoid(c)).astype(BF16)
    o_ref[...] = jnp.dot(sc, w_ref[...].astype(BF16), preferred_element_type=F32) + b_ref[...]


def _cond(c_pad, w, b):
    rows, d = c_pad.shape
    n = w.shape[1]
    tn = _pick_tile(n, (1024, 512, 256, 128))
    return pl.pallas_call(
        _cond_kernel,
        grid=(n // tn,),
        in_specs=[pl.BlockSpec((rows, d), lambda j: (0, 0)),
                  pl.BlockSpec((d, tn), lambda j: (0, j)),
                  pl.BlockSpec((1, tn), lambda j: (0, j))],
        out_specs=pl.BlockSpec((rows, tn), lambda j: (0, j)),
        out_shape=jax.ShapeDtypeStruct((rows, n), F32),
        compiler_params=_params(("arbitrary",)),
    )(c_pad, w, b)


def _norm_mod_kernel(x_ref, g_ref, sc_ref, sh_ref, o_ref):
    x = x_ref[...]
    y = x * lax.rsqrt(jnp.mean(x * x, axis=-1, keepdims=True) + NORM_EPS)
    y = y * g_ref[...]
    o_ref[...] = (y * (1.0 + sc_ref[0]) + sh_ref[0]).astype(o_ref.dtype)


def _norm_mod(x2d, g, scale, shift, seq, out_dtype):
    t, d = x2d.shape
    tm = _pick_tile(seq, (512, 256, 128))
    per_b = seq // tm
    return pl.pallas_call(
        _norm_mod_kernel,
        grid=(t // tm,),
        in_specs=[pl.BlockSpec((tm, d), lambda i: (i, 0)),
                  pl.BlockSpec((1, d), lambda i: (0, 0)),
                  pl.BlockSpec((1, 1, d), lambda i: (i // per_b, 0, 0)),
                  pl.BlockSpec((1, 1, d), lambda i: (i // per_b, 0, 0))],
        out_specs=pl.BlockSpec((tm, d), lambda i: (i, 0)),
        out_shape=jax.ShapeDtypeStruct((t, d), out_dtype),
        compiler_params=_params(("arbitrary",)),
    )(x2d, g.reshape(1, d), scale[:, None, :], shift[:, None, :])


def _mm_kernel(a_ref, b_ref, *refs):
    n_side = (len(refs) - 1) // 2
    o_ref = refs[n_side]
    o_ref[...] = jnp.dot(a_ref[...], b_ref[...], preferred_element_type=F32).astype(o_ref.dtype)
    for src, dst in zip(refs[:n_side], refs[n_side + 1:]):
        dst[...] = src[...].astype(dst.dtype)


def _matmul(a, b, out_dtype, side=()):
    m, k = a.shape
    n = b.shape[1]
    tm = _pick_tile(m, (1024, 512, 256, 128))
    tn = _pick_tile(n, (512, 256, 128))
    ni, nj = m // tm, n // tn
    if any(s.shape[0] > ni * nj for s in side):
        prod, _ = _matmul(a, b, out_dtype)
        return prod, [s.astype(BF16) for s in side]

    def side_spec(s):
        units = s.shape[0]
        return pl.BlockSpec((1,) + s.shape[1:],
                            lambda i, j: (jnp.minimum(i * nj + j, units - 1), 0, 0))

    vmem = 48 + 2 * sum(6 * math.prod(s.shape[1:]) for s in side) // MIB
    outs = pl.pallas_call(
        _mm_kernel,
        grid=(ni, nj),
        in_specs=[pl.BlockSpec((tm, k), lambda i, j: (i, 0)),
                  pl.BlockSpec((k, tn), lambda i, j: (0, j))] + [side_spec(s) for s in side],
        out_specs=[pl.BlockSpec((tm, tn), lambda i, j: (i, j))] + [side_spec(s) for s in side],
        out_shape=[jax.ShapeDtypeStruct((m, n), out_dtype)]
                  + [jax.ShapeDtypeStruct(s.shape, BF16) for s in side],
        compiler_params=_params(("arbitrary", "arbitrary"), vmem_mib=min(vmem, 58)),
    )(a, b, *side)
    return outs[0], outs[1:]


def _softplus(z):
    return jnp.maximum(z, 0.0) + jnp.log(1.0 + jnp.exp(-jnp.abs(z)))


def _rwkv_kernel(r_ref, k_ref, v_ref, l_ref, mur_ref, muk_ref, muv_ref, mul_ref,
                 w0_ref, a0_ref, kk_ref, ka_ref, rk_ref, lw_ref, lb_ref,
                 ww_ref, wa_ref, wg_ref, o_ref,
                 s_ref, pr_ref, pk_ref, pv_ref, pl_ref, *, n_batch, dw, da):
    c = RWKV_CHUNK
    nh = RWKV_HEADS_PER_STEP
    hw = nh * HEAD_DIM
    groups = r_ref.shape[-1] // hw
    lp = l_ref.shape[-1]
    chunk = pl.program_id(1)

    @pl.when(chunk == 0)
    def _():
        s_ref[...] = jnp.zeros_like(s_ref)
        pr_ref[...] = jnp.zeros_like(pr_ref)
        pk_ref[...] = jnp.zeros_like(pk_ref)
        pv_ref[...] = jnp.zeros_like(pv_ref)
        pl_ref[...] = jnp.zeros_like(pl_ref)

    row_w = lax.broadcasted_iota(jnp.int32, (c, hw), 0)
    col_w = lax.broadcasted_iota(jnp.int32, (c, hw), 1)
    s_idx = col_w % c
    strict = row_w > s_idx
    incl = row_w >= s_idx
    eye_w = (row_w == s_idx).astype(F32)
    merge_masks = [strict & (row_w // 2 == s_idx // 2)]
    size = 2
    while size < c:
        merge_masks.append(strict & (row_w // (2 * size) == s_idx // (2 * size))
                           & (row_w // size != s_idx // size))
        size *= 2
    bd_row = lax.broadcasted_iota(jnp.int32, (hw, hw), 0) // HEAD_DIM
    bd_col = lax.broadcasted_iota(jnp.int32, (hw, hw), 1) // HEAD_DIM
    bd_mask = bd_row == bd_col
    bd_ones = bd_mask.astype(BF16)
    tri = (lax.broadcasted_iota(jnp.int32, (c, c), 0)
           >= lax.broadcasted_iota(jnp.int32, (c, c), 1)).astype(BF16)
    row_l = lax.broadcasted_iota(jnp.int32, (c, lp), 0)
    col_l = lax.broadcasted_iota(jnp.int32, (c, lp), 1)

    def shift_mix(x, prev, mu, row):
        shifted = jnp.where(row == 0, prev, pltpu.roll(x, 1, axis=0))
        return x + (shifted - x) * mu

    def split2(x):
        hi = x.astype(BF16)
        lo = (x - hi.astype(F32)).astype(BF16)
        return hi, lo

    chains = [(bi, gi) for gi in range(groups) for bi in range(n_batch)]
    nb = range(len(chains))

    def cols(x, gi):
        return x[..., gi * hw:(gi + 1) * hw]

    def vec(ref, ci):
        return cols(ref[...], chains[ci][1])

    def rows_split(x):
        return [x[i * c:(i + 1) * c] for i in range(x.shape[0] // c)]

    def chain_split(x):
        return [cols(x[bi * c:(bi + 1) * c], gi) for bi, gi in chains]

    def seg_sum(xs):
        stacked = jnp.concatenate([x.astype(BF16) for x in xs], axis=0)
        return rows_split(jnp.dot(stacked, bd_ones, preferred_element_type=F32))

    def bd(x):
        return jnp.where(bd_mask, jnp.tile(x, (nh, 1)), 0.0).astype(BF16)

    def mm(a, b):
        return jnp.dot(a.astype(BF16), b, preferred_element_type=F32)

    def mm_nt(a, b):
        return lax.dot_general(a.astype(BF16), b, (((1,), (1,)), ((), ())),
                               preferred_element_type=F32)

    def mixed(x_ref, p_ref, mu_ref):
        return [shift_mix(cols(x_ref[bi], gi).astype(F32), cols(p_ref[bi], gi),
                          cols(mu_ref[...], gi), row_w) for bi, gi in chains]

    xr = mixed(r_ref, pr_ref, mur_ref)
    xk = mixed(k_ref, pk_ref, muk_ref)
    xv = mixed(v_ref, pv_ref, muv_ref)
    xl = [shift_mix(l_ref[bi].astype(F32), pl_ref[bi], mul_ref[...], row_l) for bi in range(n_batch)]
    states = [s_ref[bi, gi] for bi, gi in chains]

    act = jnp.concatenate(
        [jnp.where(col_l < dw, jnp.tanh(x), jnp.where(col_l < dw + da, x, jax.nn.sigmoid(x)))
         for x in xl], axis=0).astype(BF16)
    kw = _round_up(dw, LANES)
    ka = _round_up(dw + da, LANES)
    lora_w = chain_split(jnp.dot(act[:, :kw], ww_ref[:kw, :], preferred_element_type=F32))
    lora_a = chain_split(jnp.dot(act[:, :ka], wa_ref[:ka, :], preferred_element_type=F32))
    gate = chain_split(jnp.dot(act, wg_ref[...], preferred_element_type=F32))

    e = [jnp.exp(-_softplus(-(vec(w0_ref, ci) + lora_w[ci])) - 0.5) for ci in nb]
    lr = [jax.nn.sigmoid(vec(a0_ref, ci) + lora_a[ci]) for ci in nb]
    kk = [xk[ci] * vec(kk_ref, ci) for ci in nb]
    kk_ss = seg_sum([x * x for x in kk])
    kk = [x / jnp.maximum(jnp.sqrt(ss), 1e-12) for x, ss in zip(kk, kk_ss)]
    kmod = [xk[ci] * (1.0 + (lr[ci] - 1.0) * vec(ka_ref, ci)) for ci in nb]

    cum = []
    for ei in e:
        e_hi, e_lo = split2(ei)
        cum2 = jnp.dot(tri, jnp.concatenate([e_hi, e_lo], axis=1), preferred_element_type=F32)
        cum.append(cum2[:, :hw] + cum2[:, hw:])
    dec = [jnp.exp(-x) for x in cum]
    inv = [jnp.exp(x) for x in cum]
    at = [-kk[bi] * jnp.exp(e[bi] - cum[bi]) for bi in nb]
    rt = [xr[bi] * dec[bi] for bi in nb]
    bt = [kk[bi] * lr[bi] * inv[bi] for bi in nb]
    kt = [kmod[bi] * inv[bi] for bi in nb]

    ar = [jnp.concatenate([at[bi], rt[bi]], axis=0).astype(BF16) for bi in nb]
    g = [mm_nt(ar[bi], jnp.concatenate([bd(bt[bi]), bd(kt[bi])], axis=0)) for bi in nb]
    m_ab = [jnp.where(strict, x[:c, :hw], 0.0) for x in g]
    m_ak = [jnp.where(strict, x[:c, hw:], 0.0) for x in g]
    m_rb = [jnp.where(incl, x[c:, :hw], 0.0) for x in g]
    m_rk = [jnp.where(incl, x[c:, hw:], 0.0) for x in g]

    t_inv = [eye_w + jnp.where(merge_masks[0], m, 0.0) for m in m_ab]
    for mask in merge_masks[1:]:
        z = [mm(jnp.where(mask, m_ab[bi], 0.0), bd(t_inv[bi])) for bi in nb]
        t_inv = [t_inv[bi] + mm(t_inv[bi], bd(z[bi])) for bi in nb]

    s_bf = [s.astype(BF16) for s in states]
    bd_v = [bd(x) for x in xv]
    ar_s = [mm_nt(ar[bi], s_bf[bi]) for bi in nb]
    mv = [mm(jnp.concatenate([m_ak[bi], m_rk[bi]], axis=0), bd_v[bi]) for bi in nb]
    u = [mm(t_inv[bi], bd(ar_s[bi][:c] + mv[bi][:c])) for bi in nb]
    y = [ar_s[bi][c:] + mm(m_rb[bi], bd(u[bi])) + mv[bi][c:] for bi in nb]
    upd = [lax.dot_general(jnp.concatenate([u[bi], xv[bi]], axis=0).astype(BF16),
                           jnp.concatenate([bt[bi], kt[bi]], axis=0).astype(BF16),
                           (((0,), (0,)), ((), ())), preferred_element_type=F32) for bi in nb]

    mean = [m * (1.0 / HEAD_DIM) for m in seg_sum(y)]
    dy = [y[bi] - mean[bi] for bi in nb]
    var = [v * (1.0 / HEAD_DIM) for v in seg_sum([d * d for d in dy])]
    rk_sum = seg_sum([xr[ci] * kmod[ci] * vec(rk_ref, ci) for ci in nb])

    for ci, (bi, gi) in enumerate(chains):
        yn = dy[ci] * lax.rsqrt(var[ci] + RWKV_GN_EPS) * vec(lw_ref, ci) + vec(lb_ref, ci)
        o_ref[bi, :, gi * hw:(gi + 1) * hw] = ((yn + rk_sum[ci] * xv[ci]) * gate[ci]).astype(o_ref.dtype)
        s_ref[bi, gi] = (states[ci] + jnp.where(bd_mask, upd[ci], 0.0)) * dec[ci][c - 1:c]
    def last_row(x_ref, bi):
        return x_ref[bi, c - 16:c, :].astype(F32)[15:16]

    for bi in range(n_batch):
        pr_ref[bi] = last_row(r_ref, bi)
        pk_ref[bi] = last_row(k_ref, bi)
        pv_ref[bi] = last_row(v_ref, bi)
        pl_ref[bi] = last_row(l_ref, bi)


N_RWKV_INPUTS = 18
N_RWKV_SCRATCH = 5


def _rwkv_side_kernel(*refs, n_side, **kw):
    ins = refs[:N_RWKV_INPUTS]
    side_in = refs[N_RWKV_INPUTS:N_RWKV_INPUTS + n_side]
    o_ref = refs[N_RWKV_INPUTS + n_side]
    side_out = refs[N_RWKV_INPUTS + n_side + 1:N_RWKV_INPUTS + 2 * n_side + 1]
    scratch = refs[N_RWKV_INPUTS + 2 * n_side + 1:]
    assert len(scratch) == N_RWKV_SCRATCH
    _rwkv_kernel(*ins, o_ref, *scratch, **kw)
    for src, dst in zip(side_in, side_out):
        dst[...] = src[...].astype(dst.dtype)


def _rwkv(proj, mu, w0, a0, k_k, k_a, r_k, lnx_w, lnx_b, ww, wa, wg, *, d_rwkv, lp, dw, da, side=()):
    b, s, _ = proj.shape
    c = RWKV_CHUNK
    hw1 = RWKV_HEADS_PER_STEP * HEAD_DIM
    groups = RWKV_GROUPS_PER_STEP if d_rwkv % (RWKV_GROUPS_PER_STEP * hw1) == 0 else 1
    hw = groups * hw1
    assert c == HEAD_DIM and d_rwkv % hw == 0 and s % c == 0 and (3 * d_rwkv) % lp == 0
    nhg = d_rwkv // hw
    lblk = (3 * d_rwkv) // lp

    def act_spec(off):
        return pl.BlockSpec((b, c, hw), lambda h, t: (0, t, off + h))

    def vec_spec(off):
        return pl.BlockSpec((1, hw), lambda h, t: (0, off + h))

    vec = vec_spec(0)
    lora_spec = pl.BlockSpec((lp, hw), lambda h, t: (0, h))
    nt = s // c
    if any(u.shape[0] > nhg * nt for u in side):
        y, _ = _rwkv(proj, mu, w0, a0, k_k, k_a, r_k, lnx_w, lnx_b, ww, wa, wg,
                     d_rwkv=d_rwkv, lp=lp, dw=dw, da=da)
        return y, [u.astype(BF16) for u in side]

    def side_spec(u):
        units = u.shape[0]
        return pl.BlockSpec((1,) + u.shape[1:],
                            lambda h, t: (jnp.minimum(h * nt + t, units - 1), 0, 0))

    in_specs = [act_spec(0), act_spec(nhg), act_spec(2 * nhg),
                pl.BlockSpec((b, c, lp), lambda h, t: (0, t, lblk)),
                vec_spec(0), vec_spec(nhg), vec_spec(2 * nhg),
                pl.BlockSpec((1, lp), lambda h, t: (0, lblk)),
                vec, vec, vec, vec, vec, vec, vec,
                lora_spec, lora_spec, lora_spec]
    assert len(in_specs) == N_RWKV_INPUTS
    side_mib = 2 * sum(6 * math.prod(u.shape[1:]) for u in side) // MIB
    kern = functools.partial(_rwkv_side_kernel, n_side=len(side), n_batch=b, dw=dw, da=da)
    outs = pl.pallas_call(
        kern,
        grid=(nhg, nt),
        in_specs=in_specs + [side_spec(u) for u in side],
        out_specs=[pl.BlockSpec((b, c, hw), lambda h, t: (0, t, h))] + [side_spec(u) for u in side],
        out_shape=[jax.ShapeDtypeStruct((b, s, d_rwkv), BF16)]
                  + [jax.ShapeDtypeStruct(u.shape, BF16) for u in side],
        scratch_shapes=[pltpu.VMEM((b, groups, hw1, hw1), F32),
                        pltpu.VMEM((b, 1, hw), F32), pltpu.VMEM((b, 1, hw), F32),
                        pltpu.VMEM((b, 1, hw), F32), pltpu.VMEM((b, 1, lp), F32)],
        compiler_params=_params(("arbitrary", "arbitrary"), vmem_mib=min(32 + side_mib, 58)),
    )(proj, proj, proj, proj, mu, mu, mu, mu,
      w0, a0, k_k, k_a, r_k, lnx_w, lnx_b, ww, wa, wg, *side)
    return outs[0], outs[1:]


def _attn_kernel(sink_ref, slope_ref, q_ref, kp_ref, kc_ref, vp_ref, vc_ref, og_ref, *rest, gqa, qb):
    o_ref = rest[len(rest) // 2]
    if len(rest) == 3:
        rest[2][...] = rest[0][...].astype(rest[2].dtype)
    step = pl.program_id(2)
    pair = pl.program_id(1)
    wd = WINDOW
    qi = lax.broadcasted_iota(jnp.int32, (wd, 2 * wd), 0)
    kj = lax.broadcasted_iota(jnp.int32, (wd, 2 * wd), 1)
    dist = qi + wd - kj
    band = (dist >= 0) & (dist < wd)
    valid = [band & ((kj >= wd) | (step > 0))] + [band] * (qb - 1)
    dist_f = dist.astype(F32)
    scale = HEAD_DIM ** -0.5
    k_all = jnp.concatenate([kp_ref[...], kc_ref[...]], axis=0)
    v_all = jnp.concatenate([vp_ref[...], vc_ref[...]], axis=0)
    heads = range(2 * gqa)
    units = [(sb, hl) for sb in range(qb) for hl in heads]
    nu = range(len(units))

    def kv(x, sb, hl):
        kvh = hl // gqa
        return x[sb * wd:(sb + 2) * wd, kvh * HEAD_DIM:(kvh + 1) * HEAD_DIM]

    slopes = [slope_ref[pair * (2 * gqa) + hl] for _, hl in units]
    sinks = [sink_ref[pair * (2 * gqa) + hl] for _, hl in units]
    s = [lax.dot_general(q_ref[sb * wd:(sb + 1) * wd, hl * HEAD_DIM:(hl + 1) * HEAD_DIM],
                         kv(k_all, sb, hl), (((1,), (1,)), ((), ())), preferred_element_type=F32)
         for sb, hl in units]
    s = [jnp.where(valid[units[u][0]], s[u] * scale - slopes[u] * dist_f, -jnp.inf) for u in nu]
    m = [jnp.maximum(jnp.max(s[u], axis=-1, keepdims=True), sinks[u]) for u in nu]
    p = [jnp.exp(s[u] - m[u]) for u in nu]
    denom = [jnp.sum(p[u], axis=-1, keepdims=True) + jnp.exp(sinks[u] - m[u]) for u in nu]
    o = [jnp.dot(p[u].astype(BF16), kv(v_all, *units[u]), preferred_element_type=F32) / denom[u]
         for u in nu]
    o = [x * lax.rsqrt(jnp.mean(x * x, axis=-1, keepdims=True) + NORM_EPS) for x in o]
    nh = len(heads)
    o_all = jnp.concatenate([jnp.concatenate(o[sb * nh:(sb + 1) * nh], axis=-1) for sb in range(qb)],
                            axis=0) * og_ref[...]
    o_ref[...] = o_all.astype(o_ref.dtype)


def _attention(qkv, sinks, out_g, *, batch, seq, d_attn, side=None):
    d_kv = N_KV_HEADS * HEAD_DIM
    n_q_heads = d_attn // HEAD_DIM
    gqa = n_q_heads // N_KV_HEADS
    qw = 2 * gqa * HEAD_DIM
    n_pairs = N_KV_HEADS // 2
    nb = seq // WINDOW
    assert qw % LANES == 0 and d_attn % LANES == 0
    k_off = d_attn // LANES
    v_off = (d_attn + d_kv) // LANES

    qb = 2 if nb % 2 == 0 else 1
    ns = nb // qb

    def cur(off):
        return pl.BlockSpec((qb * WINDOW, LANES), lambda b, p, i, *_: (b * ns + i, off + p))

    def prev(off):
        return pl.BlockSpec((WINDOW, LANES),
                            lambda b, p, i, *_: (b * nb + jnp.maximum(qb * i - 1, 0), off + p))

    slopes = jnp.exp2(-8.0 * jnp.arange(1, n_q_heads + 1, dtype=F32) / n_q_heads)
    if side is not None and side.shape[0] > batch * n_pairs * ns:
        return _attention(qkv, sinks, out_g, batch=batch, seq=seq, d_attn=d_attn), side.astype(BF16)
    sides = [] if side is None else [side]

    def side_spec(u):
        units = u.shape[0]
        return pl.BlockSpec((1,) + u.shape[1:], lambda b, p, i, *_: (
            jnp.minimum((b * n_pairs + p) * ns + i, units - 1), 0, 0))

    grid_spec = pltpu.PrefetchScalarGridSpec(
        num_scalar_prefetch=2,
        grid=(batch, n_pairs, ns),
        in_specs=[pl.BlockSpec((qb * WINDOW, qw), lambda b, p, i, *_: (b * ns + i, p)),
                  prev(k_off), cur(k_off), prev(v_off), cur(v_off),
                  pl.BlockSpec((1, qw), lambda b, p, i, *_: (0, p))] + [side_spec(u) for u in sides],
        out_specs=[pl.BlockSpec((qb * WINDOW, qw), lambda b, p, i, *_: (b * ns + i, p))]
                  + [side_spec(u) for u in sides],
    )
    outs = pl.pallas_call(
        functools.partial(_attn_kernel, gqa=gqa, qb=qb),
        grid_spec=grid_spec,
        out_shape=[jax.ShapeDtypeStruct((batch * seq, d_attn), BF16)]
                  + [jax.ShapeDtypeStruct(u.shape, BF16) for u in sides],
        compiler_params=_params(("arbitrary", "arbitrary", "arbitrary")),
    )(sinks, slopes, qkv, qkv, qkv, qkv, qkv, out_g.reshape(1, d_attn), *sides)
    return outs[0] if side is None else (outs[0], outs[1])


def _outproj_kernel(yr_ref, ya_ref, w1_ref, w2_ref, x_ref, g_ref, o_ref):
    acc = jnp.dot(yr_ref[...], w1_ref[...], preferred_element_type=F32)
    acc = acc + jnp.dot(ya_ref[...], w2_ref[...], preferred_element_type=F32)
    o_ref[...] = x_ref[...] + g_ref[0] * acc


def _outproj(y_rwkv, y_attn, w_out, x2d, gate, seq):
    t, d = x2d.shape
    k1 = y_rwkv.shape[1]
    k2 = y_attn.shape[1]
    assert k1 == k2
    tm = _pick_tile(seq, (1024, 512, 256, 128))
    tn = _pick_tile(d, (512, 256, 128))
    per_b = seq // tm
    return pl.pallas_call(
        _outproj_kernel,
        grid=(t // tm, d // tn),
        in_specs=[pl.BlockSpec((tm, k1), lambda i, j: (i, 0)),
                  pl.BlockSpec((tm, k2), lambda i, j: (i, 0)),
                  pl.BlockSpec((k1, tn), lambda i, j: (0, j)),
                  pl.BlockSpec((k2, tn), lambda i, j: (1, j)),
                  pl.BlockSpec((tm, tn), lambda i, j: (i, j)),
                  pl.BlockSpec((1, 1, tn), lambda i, j: (i // per_b, 0, j))],
        out_specs=pl.BlockSpec((tm, tn), lambda i, j: (i, j)),
        out_shape=jax.ShapeDtypeStruct((t, d), F32),
        compiler_params=_params(("arbitrary", "arbitrary")),
    )(y_rwkv, y_attn, w_out, w_out, x2d, gate[:, None, :])


U32 = jnp.uint32


def _pack_pairs(x):
    n = x.shape[1] // 2
    lo = lax.bitcast_convert_type(x[:, :n].astype(BF16).astype(F32), U32)
    hi = lax.bitcast_convert_type(x[:, n:].astype(BF16).astype(F32), U32)
    return (lo >> 16) | hi


def _unpack_pairs(w):
    lo = lax.bitcast_convert_type(w << 16, F32)
    hi = lax.bitcast_convert_type(w & jnp.uint32(0xFFFF0000), F32)
    return jnp.concatenate([lo, hi], axis=-1)


def _slab_pitch(ns):
    return ns + 8 if ns % 16 == 0 else _round_up(ns, 8)


def _to_slabs(ref, x):
    rows, d = x.shape
    ns = d // LANES
    pitch = _slab_pitch(ns)
    ref[...] = jnp.zeros_like(ref)
    for s in range(ns):
        ref[pl.ds(s, rows, stride=pitch), :] = x[:, s * LANES:(s + 1) * LANES]


def _from_slabs(ref, ns):
    pitch = _slab_pitch(ns)
    rows = ref.shape[0] // pitch
    return jnp.concatenate([ref[pl.ds(s, rows, stride=pitch), :] for s in range(ns)], axis=-1)


def _norm_route_kernel(x_ref, g_ref, sc_ref, sh_ref, w_ref, b_ref, u_ref, o_ref):
    x = x_ref[...]
    y = x * lax.rsqrt(jnp.mean(x * x, axis=-1, keepdims=True) + NORM_EPS)
    y = y * g_ref[...]
    y = y * (1.0 + sc_ref[0]) + sh_ref[0]
    _to_slabs(u_ref, _pack_pairs(y))
    logits = jnp.dot(y.astype(BF16), w_ref[...], preferred_element_type=F32) + b_ref[...]
    col = lax.broadcasted_iota(jnp.int32, logits.shape, 1).astype(F32)
    ng = float(N_GROUPS)
    epg = float(EXPERTS_PER_GROUP)
    big = 1e9
    is_g = col < ng
    lg = jnp.where(is_g, logits, -jnp.inf)
    mg = jnp.max(lg, axis=-1, keepdims=True)
    gidx = jnp.min(jnp.where(lg == mg, col, big), axis=-1, keepdims=True)
    zg = jnp.sum(jnp.where(is_g, jnp.exp(lg - mg), 0.0), axis=-1, keepdims=True)
    g_w = 1.0 / zg
    lo = ng + gidx * epg
    in_grp = (col >= lo) & (col < lo + epg)
    le = jnp.where(in_grp, logits, -jnp.inf)
    m1 = jnp.max(le, axis=-1, keepdims=True)
    i1 = jnp.min(jnp.where(le == m1, col, big), axis=-1, keepdims=True)
    le2 = jnp.where(col == i1, -jnp.inf, le)
    m2 = jnp.max(le2, axis=-1, keepdims=True)
    i2 = jnp.min(jnp.where(le2 == m2, col, big), axis=-1, keepdims=True)
    e2 = jnp.exp(m2 - m1)
    w1 = 1.0 / (1.0 + e2)
    w2 = e2 / (1.0 + e2)
    out = jnp.where(col == 0.0, i1 - ng,
                    jnp.where(col == 1.0, i2 - ng,
                              jnp.where(col == 2.0, g_w * w1,
                                        jnp.where(col == 3.0, g_w * w2, 0.0))))
    o_ref[...] = out


def _norm_route(x2d, g, scale, shift, w_r, b_r, seq):
    t, d = x2d.shape
    tm = _pick_tile(seq, (256, 128))
    per_b = seq // tm
    ns = d // (2 * LANES)
    return pl.pallas_call(
        _norm_route_kernel,
        grid=(t // tm,),
        in_specs=[pl.BlockSpec((tm, d), lambda i: (i, 0)),
                  pl.BlockSpec((1, d), lambda i: (0, 0)),
                  pl.BlockSpec((1, 1, d), lambda i: (i // per_b, 0, 0)),
                  pl.BlockSpec((1, 1, d), lambda i: (i // per_b, 0, 0)),
                  pl.BlockSpec((d, LANES), lambda i: (0, 0)),
                  pl.BlockSpec((1, LANES), lambda i: (0, 0))],
        out_specs=[pl.BlockSpec((tm * _slab_pitch(ns), LANES), lambda i: (i, 0)),
                   pl.BlockSpec((tm, LANES), lambda i: (i, 0))],
        out_shape=[jax.ShapeDtypeStruct((t * _slab_pitch(ns), LANES), U32),
                   jax.ShapeDtypeStruct((t, LANES), F32)],
        compiler_params=_params(("arbitrary",)),
    )(x2d, g.reshape(1, d), scale[:, None, :], shift[:, None, :], w_r, b_r)


GATHER_GROUP = 8


def _gather_rows(idx_ref, n_rows, src_hbm, dst, sem, col0=0, unroll=False, slab_rows=0):
    pitch = _slab_pitch(slab_rows) if slab_rows else 1
    size = slab_rows if slab_rows else 1

    def issue(r):
        tok = idx_ref[0, 0, col0 + r]
        if slab_rows:
            src = src_hbm.at[pl.ds(pl.multiple_of(tok * pitch, 8), size)]
            dst_r = dst.at[pl.ds(pl.multiple_of(r * pitch, 8), size)]
        else:
            src, dst_r = src_hbm.at[pl.ds(tok, 1)], dst.at[pl.ds(r, 1)]
        pltpu.make_async_copy(src, dst_r, sem).start()

    if unroll is True:
        for r in range(n_rows):
            issue(r)
    elif isinstance(n_rows, int):
        def body(r, carry):
            issue(r)
            return carry
        lax.fori_loop(0, n_rows, body, 0, unroll=max(int(unroll), 1))
    else:
        def group(g, carry):
            for k in range(GATHER_GROUP):
                issue(g * GATHER_GROUP + k)
            return carry
        lax.fori_loop(0, n_rows // GATHER_GROUP, group, 0)


def _wait_rows(n_rows, src_hbm, dst, sem):
    if not isinstance(n_rows, int):
        n_rows = pl.multiple_of(n_rows, 8)
    pltpu.make_async_copy(src_hbm.at[pl.ds(0, n_rows)], dst.at[pl.ds(0, n_rows)], sem).wait()


def _expert_kernel(be_ref, nu_ref, nv_ref, tok_ref, tokn_ref, u_hbm, wg_ref, wu_ref, wd_ref,
                   o_ref, xbuf_a, xbuf_b, sem):
    i = pl.program_id(0)
    last = pl.num_programs(0) - 1
    n_used = nu_ref[0]
    rows, n_words = o_ref.shape
    ns = n_words // LANES
    n_next = jnp.where(i < last, nv_ref[jnp.minimum(i + 1, last)], 0)

    @pl.when(i == 0)
    def _():
        xbuf_a[...] = jnp.zeros_like(xbuf_a)
        xbuf_b[...] = jnp.zeros_like(xbuf_b)
        _gather_rows(tok_ref, nv_ref[0], u_hbm, xbuf_a, sem.at[0], slab_rows=ns)

    def step(cur, cur_sem, nxt, nxt_sem):
        @pl.when(i < n_used)
        def _():
            _wait_rows(nv_ref[i] * ns, u_hbm, cur, cur_sem)
            _gather_rows(tokn_ref, n_next, u_hbm, nxt, nxt_sem, slab_rows=ns)
            xb = _unpack_pairs(_from_slabs(cur, ns)).astype(BF16)
            gate = jnp.dot(xb, wg_ref[0], preferred_element_type=F32)
            up = jnp.dot(xb, wu_ref[0], preferred_element_type=F32)
            h = (gate * jax.nn.sigmoid(gate) * up).astype(BF16)
            o_ref[...] = _pack_pairs(jnp.dot(h, wd_ref[0], preferred_element_type=F32))

        @pl.when(i >= n_used)
        def _():
            o_ref[...] = jnp.zeros_like(o_ref)

    @pl.when(i % 2 == 0)
    def _():
        step(xbuf_a, sem.at[0], xbuf_b, sem.at[1])

    @pl.when(i % 2 == 1)
    def _():
        step(xbuf_b, sem.at[1], xbuf_a, sem.at[0])


def _experts(block_expert, n_used, n_valid, row_tok, u2, wg, wu, wd):
    d = wg.shape[1]
    n_words = d // 2
    pitch = _slab_pitch(n_words // LANES)
    nb = block_expert.shape[0]
    rows = MOE_ROWS
    de = wg.shape[-1]
    tok3 = row_tok.reshape(nb, 1, rows)
    grid_spec = pltpu.PrefetchScalarGridSpec(
        num_scalar_prefetch=3,
        grid=(nb,),
        in_specs=[pl.BlockSpec((1, 1, rows), lambda i, be, nu, nv: (i, 0, 0), memory_space=pltpu.SMEM),
                  pl.BlockSpec((1, 1, rows), lambda i, be, nu, nv: (jnp.minimum(i + 1, nb - 1), 0, 0),
                               memory_space=pltpu.SMEM),
                  pl.BlockSpec(memory_space=pl.ANY),
                  pl.BlockSpec((1, d, de), lambda i, be, nu, nv: (be[i], 0, 0)),
                  pl.BlockSpec((1, d, de), lambda i, be, nu, nv: (be[i], 0, 0)),
                  pl.BlockSpec((1, de, d), lambda i, be, nu, nv: (be[i], 0, 0))],
        out_specs=pl.BlockSpec((rows, n_words), lambda i, be, nu, nv: (i, 0)),
        scratch_shapes=[pltpu.VMEM((rows * pitch, LANES), U32),
                        pltpu.VMEM((rows * pitch, LANES), U32),
                        pltpu.SemaphoreType.DMA((2,))],
    )
    return pl.pallas_call(
        _expert_kernel,
        grid_spec=grid_spec,
        out_shape=jax.ShapeDtypeStruct((nb * rows, n_words), U32),
        compiler_params=_params(("arbitrary",), vmem_mib=58),
    )(block_expert, n_used, n_valid, tok3, tok3, u2, wg, wu, wd)


def _combine_kernel(pos_ref, posn_ref, ys_hbm, r_ref, x_ref, g_ref, ng_ref, o_ref, buf_a, buf_b, sem):
    i = pl.program_id(0)
    last = pl.num_programs(0) - 1
    tc = x_ref.shape[0]

    def gather(idx_ref, buf, s, unroll):
        _gather_rows(idx_ref, tc, ys_hbm, buf.at[0], s, 0, unroll)
        _gather_rows(idx_ref, tc, ys_hbm, buf.at[1], s, tc, unroll)

    def wait(buf, s):
        _wait_rows(tc, ys_hbm, buf.at[0], s)
        _wait_rows(tc, ys_hbm, buf.at[1], s)

    @pl.when(i == 0)
    def _():
        gather(pos_ref, buf_a, sem.at[0], False)

    def step(cur, cur_sem, nxt, nxt_sem):
        wait(cur, cur_sem)
        gather(posn_ref, nxt, nxt_sem, True)
        moe = _unpack_pairs(cur[0]) * r_ref[:, 2:3] + _unpack_pairs(cur[1]) * r_ref[:, 3:4]
        x = x_ref[...] + g_ref[0] * moe
        y = x * lax.rsqrt(jnp.mean(x * x, axis=-1, keepdims=True) + NORM_EPS)
        o_ref[...] = y * ng_ref[...]

        @pl.when(i == last)
        def _():
            wait(nxt, nxt_sem)

    @pl.when(i % 2 == 0)
    def _():
        step(buf_a, sem.at[0], buf_b, sem.at[1])

    @pl.when(i % 2 == 1)
    def _():
        step(buf_b, sem.at[1], buf_a, sem.at[0])


def _combine(pos, route, ys, x1, gate, norm_g, seq):
    t, d = x1.shape
    tc = _pick_tile(seq, (128,))
    n = t // tc
    per_b = seq // tc
    pos3 = pos.reshape(n, tc, 2).transpose(0, 2, 1).reshape(n, 1, 2 * tc)
    return pl.pallas_call(
        _combine_kernel,
        grid=(n,),
        in_specs=[pl.BlockSpec((1, 1, 2 * tc), lambda i: (i, 0, 0), memory_space=pltpu.SMEM),
                  pl.BlockSpec((1, 1, 2 * tc), lambda i: (jnp.minimum(i + 1, n - 1), 0, 0),
                               memory_space=pltpu.SMEM),
                  pl.BlockSpec(memory_space=pl.ANY),
                  pl.BlockSpec((tc, LANES), lambda i: (i, 0)),
                  pl.BlockSpec((tc, d), lambda i: (i, 0)),
                  pl.BlockSpec((1, 1, d), lambda i: (i // per_b, 0, 0)),
                  pl.BlockSpec((1, d), lambda i: (0, 0))],
        out_specs=pl.BlockSpec((tc, d), lambda i: (i, 0)),
        out_shape=jax.ShapeDtypeStruct((t, d), F32),
        scratch_shapes=[pltpu.VMEM((2, tc, d // 2), U32), pltpu.VMEM((2, tc, d // 2), U32),
                        pltpu.SemaphoreType.DMA((2,))],
        compiler_params=_params(("arbitrary",)),
    )(pos3, pos3, ys, route, x1, gate[:, None, :], norm_g.reshape(1, d))


def _dispatch(expert_id, n_tok):
    m = expert_id.shape[0]
    rows = MOE_ROWS
    nb = -(-m // rows) + N_EXPERTS
    onehot = (expert_id[:, None] == jnp.arange(N_EXPERTS, dtype=jnp.int32)[None, :]).astype(jnp.int32)
    csum = jnp.cumsum(onehot, axis=0)
    rank = jnp.take_along_axis(csum, expert_id[:, None], axis=1)[:, 0] - 1
    counts = csum[-1]
    padded = ((counts + rows - 1) // rows) * rows
    pends = jnp.cumsum(padded)
    pstarts = pends - padded
    dest = (pstarts[expert_id] + rank).astype(jnp.int32)
    tok = jnp.arange(m, dtype=jnp.int32) // (m // n_tok)
    row_tok = jnp.zeros((nb * rows,), jnp.int32).at[dest].set(tok)
    n_used = (pends[-1] // rows).astype(jnp.int32)
    all_blk = jnp.arange(nb, dtype=jnp.int32)
    blk = jnp.minimum(all_blk, n_used - 1)
    block_expert = jnp.minimum(jnp.searchsorted(pends, blk * rows, side='right'),
                               N_EXPERTS - 1).astype(jnp.int32)
    seg_end = (pstarts + counts)[block_expert]
    n_valid = jnp.where(all_blk < n_used, jnp.clip(seg_end - all_blk * rows, 0, rows), 0)
    n_valid = ((n_valid + GATHER_GROUP - 1) // GATHER_GROUP) * GATHER_GROUP
    return block_expert, n_used.reshape(1), n_valid.astype(jnp.int32), row_tok, dest


def _layer(x, c, w_cond, b_cond, norm1_g, w_in, rwkv_mu, rwkv_w0, rwkv_w_up, rwkv_a0, rwkv_a_up,
           rwkv_g_up, rwkv_k_k, rwkv_k_a, rwkv_r_k, rwkv_lnx_w, rwkv_lnx_b, attn_sinks, attn_out_g,
           w_out, norm2_g, router_group, router_group_bias, router_expert, router_expert_bias,
           expert_w_gate, expert_w_up, expert_w_down):
    b, s, d = x.shape
    t = b * s
    d_rwkv = rwkv_w0.shape[-1]
    d_attn = attn_out_g.shape[-1]
    dw, da, dg = rwkv_w_up.shape[0], rwkv_a_up.shape[0], rwkv_g_up.shape[0]
    n_lora = dw + da + dg
    lp = _round_up(n_lora, LANES)
    while (3 * d_rwkv) % lp:
        lp += LANES
    n_rwkv_cols = 3 * d_rwkv + n_lora

    c_pad = jnp.zeros((_round_up(b, 8), d), F32).at[:b].set(c)
    mod = _cond(c_pad, w_cond, b_cond.reshape(1, -1))[:b]
    sh1, sc1, g1, sh2, sc2, g2 = jnp.split(mod, 6, axis=-1)

    x2d = x.reshape(t, d)
    u1 = _norm_mod(x2d, norm1_g, sc1, sh1, s, BF16)

    zpad = jnp.zeros((d, lp - n_lora), F32)
    w_r = jnp.concatenate([w_in[:, :n_rwkv_cols], zpad], axis=1).astype(BF16)
    w_a = w_in[:, n_rwkv_cols:].astype(BF16)
    proj_r, _ = _matmul(u1, w_r, BF16)
    proj_a, _ = _matmul(u1, w_a, BF16)

    def lora_rows(wmat, off):
        return jnp.zeros((lp, d_rwkv), F32).at[off:off + wmat.shape[0]].set(wmat).astype(BF16)

    mu = jnp.concatenate([rwkv_mu, jnp.zeros((lp - n_lora,), F32)]).reshape(1, -1)
    row = lambda v: v.reshape(1, d_rwkv)
    n_exp = expert_w_gate.shape[0]
    halves = lambda w: w.reshape(2 * n_exp, w.shape[1] // 2, w.shape[2])
    y_rwkv, (wg_b, wu_b) = _rwkv(
        proj_r.reshape(b, s, -1), mu, row(rwkv_w0), row(rwkv_a0), row(rwkv_k_k),
        row(rwkv_k_a), row(rwkv_r_k), row(rwkv_lnx_w), row(rwkv_lnx_b),
        lora_rows(rwkv_w_up, 0), lora_rows(rwkv_a_up, dw), lora_rows(rwkv_g_up, dw + da),
        d_rwkv=d_rwkv, lp=lp, dw=dw, da=da, side=(halves(expert_w_gate), halves(expert_w_up)))
    y_rwkv = y_rwkv.reshape(t, d_rwkv)
    wg_b = wg_b.reshape(expert_w_gate.shape)
    wu_b = wu_b.reshape(expert_w_up.shape)

    y_attn, wd_b = _attention(proj_a, attn_sinks, attn_out_g, batch=b, seq=s, d_attn=d_attn,
                              side=halves(expert_w_down))
    wd_b = wd_b.reshape(expert_w_down.shape)

    x1 = _outproj(y_rwkv, y_attn, w_out.astype(BF16), x2d, g1, s)

    n_r = N_GROUPS + N_EXPERTS
    w_router = jnp.concatenate([router_group, router_expert, jnp.zeros((d, LANES - n_r), F32)],
                               axis=1).astype(BF16)
    b_router = jnp.concatenate([router_group_bias, router_expert_bias,
                                jnp.zeros((LANES - n_r,), F32)]).reshape(1, LANES)
    u2, route = _norm_route(x1, norm2_g, sc2, sh2, w_router, b_router, s)
    expert_id = route[:, :2].astype(jnp.int32).reshape(-1)
    block_expert, n_used, n_valid, row_tok, dest = _dispatch(expert_id, t)
    ys = _experts(block_expert, n_used, n_valid, row_tok, u2, wg_b, wu_b, wd_b)
    return ys, dest.reshape(t, 2), route, x1, g2


def kernel(x, c, w_cond, b_cond, norm1_g, w_in, rwkv_mu, rwkv_w0, rwkv_w_up, rwkv_a0, rwkv_a_up, rwkv_g_up, rwkv_k_k, rwkv_k_a, rwkv_r_k, rwkv_lnx_w, rwkv_lnx_b, attn_sinks, attn_out_g, w_out, norm2_g, router_group, router_group_bias, router_expert, router_expert_bias, expert_w_gate, expert_w_up, expert_w_down, norm_f_g):
    b, s, d = x.shape
    depth = w_cond.shape[0]
    assert depth == 1, "the fused final norm assumes a single layer"
    l = 0
    ys, pos, route, x1, g2 = _layer(
        x, c, w_cond[l], b_cond[l], norm1_g[l], w_in[l], rwkv_mu[l], rwkv_w0[l], rwkv_w_up[l],
        rwkv_a0[l], rwkv_a_up[l], rwkv_g_up[l], rwkv_k_k[l], rwkv_k_a[l], rwkv_r_k[l],
        rwkv_lnx_w[l], rwkv_lnx_b[l], attn_sinks[l], attn_out_g[l], w_out[l], norm2_g[l],
        router_group[l], router_group_bias[l], router_expert[l], router_expert_bias[l],
        expert_w_gate[l], expert_w_up[l], expert_w_down[l])
    out = _combine(pos, route, ys, x1, g2, norm_f_g, s)
    return out.reshape(b, s, d)
```

```python
import functools
import math

import jax
import jax.numpy as jnp
from jax import lax
from jax.experimental import pallas as pl
from jax.experimental.pallas import tpu as pltpu

F32 = jnp.float32
BF16 = jnp.bfloat16

HEAD_DIM = 64
N_KV_HEADS = 8
WINDOW = 128
N_GROUPS = 8
EXPERTS_PER_GROUP = 8
N_EXPERTS = N_GROUPS * EXPERTS_PER_GROUP
RWKV_GN_EPS = 64e-5
NORM_EPS = 1e-6
RWKV_CHUNK = 64
RWKV_HEADS_PER_STEP = 4
RWKV_GROUPS_PER_STEP = 4
MOE_ROWS = 256
LANES = 128
MIB = 1024 * 1024


def _params(sem, vmem_mib=48):
    return pltpu.CompilerParams(dimension_semantics=sem, vmem_limit_bytes=vmem_mib * MIB)


def _round_up(n, m):
    return -(-n // m) * m


def _pick_tile(n, candidates):
    for t in candidates:
        if n % t == 0:
            return t
    raise ValueError(f"no tile in {candidates} divides {n}")


def _cond_kernel(c_ref, w_ref, b_ref, o_ref):
    c = c_ref[...]
    sc = (c * jax.nn.sigmoid(c)).astype(BF16)
    o_ref[...] = jnp.dot(sc, w_ref[...].astype(BF16), preferred_element_type=F32) + b_ref[...]


def _cond(c_pad, w, b):
    rows, d = c_pad.shape
    n = w.shape[1]
    tn = _pick_tile(n, (1024, 512, 256, 128))
    return pl.pallas_call(
        _cond_kernel,
        grid=(n // tn,),
        in_specs=[pl.BlockSpec((rows, d), lambda j: (0, 0)),
                  pl.BlockSpec((d, tn), lambda j: (0, j)),
                  pl.BlockSpec((1, tn), lambda j: (0, j))],
        out_specs=pl.BlockSpec((rows, tn), lambda j: (0, j)),
        out_shape=jax.ShapeDtypeStruct((rows, n), F32),
        compiler_params=_params(("arbitrary",)),
    )(c_pad, w, b)


def _norm_mod_kernel(x_ref, g_ref, sc_ref, sh_ref, o_ref):
    x = x_ref[...]
    y = x * lax.rsqrt(jnp.mean(x * x, axis=-1, keepdims=True) + NORM_EPS)
    y = y * g_ref[...]
    o_ref[...] = (y * (1.0 + sc_ref[0]) + sh_ref[0]).astype(o_ref.dtype)


def _norm_mod(x2d, g, scale, shift, seq, out_dtype):
    t, d = x2d.shape
    tm = _pick_tile(seq, (512, 256, 128))
    per_b = seq // tm
    return pl.pallas_call(
        _norm_mod_kernel,
        grid=(t // tm,),
        in_specs=[pl.BlockSpec((tm, d), lambda i: (i, 0)),
                  pl.BlockSpec((1, d), lambda i: (0, 0)),
                  pl.BlockSpec((1, 1, d), lambda i: (i // per_b, 0, 0)),
                  pl.BlockSpec((1, 1, d), lambda i: (i // per_b, 0, 0))],
        out_specs=pl.BlockSpec((tm, d), lambda i: (i, 0)),
        out_shape=jax.ShapeDtypeStruct((t, d), out_dtype),
        compiler_params=_params(("arbitrary",)),
    )(x2d, g.reshape(1, d), scale[:, None, :], shift[:, None, :])


def _mm_kernel(a_ref, b_ref, *refs):
    n_side = (len(refs) - 1) // 2
    o_ref = refs[n_side]
    o_ref[...] = jnp.dot(a_ref[...], b_ref[...], preferred_element_type=F32).astype(o_ref.dtype)
    for src, dst in zip(refs[:n_side], refs[n_side + 1:]):
        dst[...] = src[...].astype(dst.dtype)


def _matmul(a, b, out_dtype, side=()):
    m, k = a.shape
    n = b.shape[1]
    tm = _pick_tile(m, (1024, 512, 256, 128))
    tn = _pick_tile(n, (512, 256, 128))
    ni, nj = m // tm, n // tn
    if any(s.shape[0] > ni * nj for s in side):
        prod, _ = _matmul(a, b, out_dtype)
        return prod, [s.astype(BF16) for s in side]

    def side_spec(s):
        units = s.shape[0]
        return pl.BlockSpec((1,) + s.shape[1:],
                            lambda i, j: (jnp.minimum(i * nj + j, units - 1), 0, 0))

    vmem = 48 + 2 * sum(6 * math.prod(s.shape[1:]) for s in side) // MIB
    outs = pl.pallas_call(
        _mm_kernel,
        grid=(ni, nj),
        in_specs=[pl.BlockSpec((tm, k), lambda i, j: (i, 0)),
                  pl.BlockSpec((k, tn), lambda i, j: (0, j))] + [side_spec(s) for s in side],
        out_specs=[pl.BlockSpec((tm, tn), lambda i, j: (i, j))] + [side_spec(s) for s in side],
        out_shape=[jax.ShapeDtypeStruct((m, n), out_dtype)]
                  + [jax.ShapeDtypeStruct(s.shape, BF16) for s in side],
        compiler_params=_params(("arbitrary", "arbitrary"), vmem_mib=min(vmem, 58)),
    )(a, b, *side)
    return outs[0], outs[1:]


def _softplus(z):
    return jnp.maximum(z, 0.0) + jnp.log(1.0 + jnp.exp(-jnp.abs(z)))


def _rwkv_kernel(r_ref, k_ref, v_ref, l_ref, mur_ref, muk_ref, muv_ref, mul_ref,
                 w0_ref, a0_ref, kk_ref, ka_ref, rk_ref, lw_ref, lb_ref,
                 ww_ref, wa_ref, wg_ref, o_ref,
                 s_ref, pr_ref, pk_ref, pv_ref, pl_ref, *, n_batch, dw, da):
    c = RWKV_CHUNK
    nh = RWKV_HEADS_PER_STEP
    hw = nh * HEAD_DIM
    groups = r_ref.shape[-1] // hw
    lp = l_ref.shape[-1]
    chunk = pl.program_id(1)

    @pl.when(chunk == 0)
    def _():
        s_ref[...] = jnp.zeros_like(s_ref)
        pr_ref[...] = jnp.zeros_like(pr_ref)
        pk_ref[...] = jnp.zeros_like(pk_ref)
        pv_ref[...] = jnp.zeros_like(pv_ref)
        pl_ref[...] = jnp.zeros_like(pl_ref)

    row_w = lax.broadcasted_iota(jnp.int32, (c, hw), 0)
    col_w = lax.broadcasted_iota(jnp.int32, (c, hw), 1)
    s_idx = col_w % c
    strict = row_w > s_idx
    incl = row_w >= s_idx
    eye_w = (row_w == s_idx).astype(F32)
    merge_masks = [strict & (row_w // 2 == s_idx // 2)]
    size = 2
    while size < c:
        merge_masks.append(strict & (row_w // (2 * size) == s_idx // (2 * size))
                           & (row_w // size != s_idx // size))
        size *= 2
    bd_row = lax.broadcasted_iota(jnp.int32, (hw, hw), 0) // HEAD_DIM
    bd_col = lax.broadcasted_iota(jnp.int32, (hw, hw), 1) // HEAD_DIM
    bd_mask = bd_row == bd_col
    bd_ones = bd_mask.astype(BF16)
    tri = (lax.broadcasted_iota(jnp.int32, (c, c), 0)
           >= lax.broadcasted_iota(jnp.int32, (c, c), 1)).astype(BF16)
    row_l = lax.broadcasted_iota(jnp.int32, (c, lp), 0)
    col_l = lax.broadcasted_iota(jnp.int32, (c, lp), 1)

    def shift_mix(x, prev, mu, row):
        shifted = jnp.where(row == 0, prev, pltpu.roll(x, 1, axis=0))
        return x + (shifted - x) * mu

    def split2(x):
        hi = x.astype(BF16)
        lo = (x - hi.astype(F32)).astype(BF16)
        return hi, lo

    chains = [(bi, gi) for gi in range(groups) for bi in range(n_batch)]
    nb = range(len(chains))

    def cols(x, gi):
        return x[..., gi * hw:(gi + 1) * hw]

    def vec(ref, ci):
        return cols(ref[...], chains[ci][1])

    def rows_split(x):
        return [x[i * c:(i + 1) * c] for i in range(x.shape[0] // c)]

    def chain_split(x):
        return [cols(x[bi * c:(bi + 1) * c], gi) for bi, gi in chains]

    def seg_sum(xs):
        stacked = jnp.concatenate([x.astype(BF16) for x in xs], axis=0)
        return rows_split(jnp.dot(stacked, bd_ones, preferred_element_type=F32))

    def bd(x):
        return jnp.where(bd_mask, jnp.tile(x, (nh, 1)), 0.0).astype(BF16)

    def mm(a, b):
        return jnp.dot(a.astype(BF16), b, preferred_element_type=F32)

    def mm_nt(a, b):
        return lax.dot_general(a.astype(BF16), b, (((1,), (1,)), ((), ())),
                               preferred_element_type=F32)

    def mixed(x_ref, p_ref, mu_ref):
        return [shift_mix(cols(x_ref[bi], gi).astype(F32), cols(p_ref[bi], gi),
                          cols(mu_ref[...], gi), row_w) for bi, gi in chains]

    xr = mixed(r_ref, pr_ref, mur_ref)
    xk = mixed(k_ref, pk_ref, muk_ref)
    xv = mixed(v_ref, pv_ref, muv_ref)
    xl = [shift_mix(l_ref[bi].astype(F32), pl_ref[bi], mul_ref[...], row_l) for bi in range(n_batch)]
    states = [s_ref[bi, gi] for bi, gi in chains]

    act = jnp.concatenate(
        [jnp.where(col_l < dw, jnp.tanh(x), jnp.where(col_l < dw + da, x, jax.nn.sigmoid(x)))
         for x in xl], axis=0).astype(BF16)
    kw = _round_up(dw, LANES)
    ka = _round_up(dw + da, LANES)
    lora_w = chain_split(jnp.dot(act[:, :kw], ww_ref[:kw, :], preferred_element_type=F32))
    lora_a = chain_split(jnp.dot(act[:, :ka], wa_ref[:ka, :], preferred_element_type=F32))
    gate = chain_split(jnp.dot(act, wg_ref[...], preferred_element_type=F32))

    e = [jnp.exp(-_softplus(-(vec(w0_ref, ci) + lora_w[ci])) - 0.5) for ci in nb]
    lr = [jax.nn.sigmoid(vec(a0_ref, ci) + lora_a[ci]) for ci in nb]
    kk = [xk[ci] * vec(kk_ref, ci) for ci in nb]
    kk_ss = seg_sum([x * x for x in kk])
    kk = [x / jnp.maximum(jnp.sqrt(ss), 1e-12) for x, ss in zip(kk, kk_ss)]
    kmod = [xk[ci] * (1.0 + (lr[ci] - 1.0) * vec(ka_ref, ci)) for ci in nb]

    cum = []
    for ei in e:
        e_hi, e_lo = split2(ei)
        cum2 = jnp.dot(tri, jnp.concatenate([e_hi, e_lo], axis=1), preferred_element_type=F32)
        cum.append(cum2[:, :hw] + cum2[:, hw:])
    dec = [jnp.exp(-x) for x in cum]
    inv = [jnp.exp(x) for x in cum]
    at = [-kk[bi] * jnp.exp(e[bi] - cum[bi]) for bi in nb]
    rt = [xr[bi] * dec[bi] for bi in nb]
    bt = [kk[bi] * lr[bi] * inv[bi] for bi in nb]
    kt = [kmod[bi] * inv[bi] for bi in nb]

    ar = [jnp.concatenate([at[bi], rt[bi]], axis=0).astype(BF16) for bi in nb]
    g = [mm_nt(ar[bi], jnp.concatenate([bd(bt[bi]), bd(kt[bi])], axis=0)) for bi in nb]
    m_ab = [jnp.where(strict, x[:c, :hw], 0.0) for x in g]
    m_ak = [jnp.where(strict, x[:c, hw:], 0.0) for x in g]
    m_rb = [jnp.where(incl, x[c:, :hw], 0.0) for x in g]
    m_rk = [jnp.where(incl, x[c:, hw:], 0.0) for x in g]

    t_inv = [eye_w + jnp.where(merge_masks[0], m, 0.0) for m in m_ab]
    for mask in merge_masks[1:]:
        z = [mm(jnp.where(mask, m_ab[bi], 0.0), bd(t_inv[bi])) for bi in nb]
        t_inv = [t_inv[bi] + mm(t_inv[bi], bd(z[bi])) for bi in nb]

    s_bf = [s.astype(BF16) for s in states]
    bd_v = [bd(x) for x in xv]
    ar_s = [mm_nt(ar[bi], s_bf[bi]) for bi in nb]
    mv = [mm(jnp.concatenate([m_ak[bi], m_rk[bi]], axis=0), bd_v[bi]) for bi in nb]
    u = [mm(t_inv[bi], bd(ar_s[bi][:c] + mv[bi][:c])) for bi in nb]
    y = [ar_s[bi][c:] + mm(m_rb[bi], bd(u[bi])) + mv[bi][c:] for bi in nb]
    upd = [lax.dot_general(jnp.concatenate([u[bi], xv[bi]], axis=0).astype(BF16),
                           jnp.concatenate([bt[bi], kt[bi]], axis=0).astype(BF16),
                           (((0,), (0,)), ((), ())), preferred_element_type=F32) for bi in nb]

    mean = [m * (1.0 / HEAD_DIM) for m in seg_sum(y)]
    dy = [y[bi] - mean[bi] for bi in nb]
    var = [v * (1.0 / HEAD_DIM) for v in seg_sum([d * d for d in dy])]
    rk_sum = seg_sum([xr[ci] * kmod[ci] * vec(rk_ref, ci) for ci in nb])

    for ci, (bi, gi) in enumerate(chains):
        yn = dy[ci] * lax.rsqrt(var[ci] + RWKV_GN_EPS) * vec(lw_ref, ci) + vec(lb_ref, ci)
        o_ref[bi, :, gi * hw:(gi + 1) * hw] = ((yn + rk_sum[ci] * xv[ci]) * gate[ci]).astype(o_ref.dtype)
        s_ref[bi, gi] = (states[ci] + jnp.where(bd_mask, upd[ci], 0.0)) * dec[ci][c - 1:c]
    def last_row(x_ref, bi):
        return x_ref[bi, c - 16:c, :].astype(F32)[15:16]

    for bi in range(n_batch):
        pr_ref[bi] = last_row(r_ref, bi)
        pk_ref[bi] = last_row(k_ref, bi)
        pv_ref[bi] = last_row(v_ref, bi)
        pl_ref[bi] = last_row(l_ref, bi)


N_RWKV_INPUTS = 18
N_RWKV_SCRATCH = 5


def _rwkv_side_kernel(*refs, n_side, **kw):
    ins = refs[:N_RWKV_INPUTS]
    side_in = refs[N_RWKV_INPUTS:N_RWKV_INPUTS + n_side]
    o_ref = refs[N_RWKV_INPUTS + n_side]
    side_out = refs[N_RWKV_INPUTS + n_side + 1:N_RWKV_INPUTS + 2 * n_side + 1]
    scratch = refs[N_RWKV_INPUTS + 2 * n_side + 1:]
    assert len(scratch) == N_RWKV_SCRATCH
    _rwkv_kernel(*ins, o_ref, *scratch, **kw)
    for src, dst in zip(side_in, side_out):
        dst[...] = src[...].astype(dst.dtype)


def _rwkv(proj, mu, w0, a0, k_k, k_a, r_k, lnx_w, lnx_b, ww, wa, wg, *, d_rwkv, lp, dw, da, side=()):
    b, s, _ = proj.shape
    c = RWKV_CHUNK
    hw1 = RWKV_HEADS_PER_STEP * HEAD_DIM
    groups = RWKV_GROUPS_PER_STEP if d_rwkv % (RWKV_GROUPS_PER_STEP * hw1) == 0 else 1
    hw = groups * hw1
    assert c == HEAD_DIM and d_rwkv % hw == 0 and s % c == 0 and (3 * d_rwkv) % lp == 0
    nhg = d_rwkv // hw
    lblk = (3 * d_rwkv) // lp

    def act_spec(off):
        return pl.BlockSpec((b, c, hw), lambda h, t: (0, t, off + h))

    def vec_spec(off):
        return pl.BlockSpec((1, hw), lambda h, t: (0, off + h))

    vec = vec_spec(0)
    lora_spec = pl.BlockSpec((lp, hw), lambda h, t: (0, h))
    nt = s // c
    if any(u.shape[0] > nhg * nt for u in side):
        y, _ = _rwkv(proj, mu, w0, a0, k_k, k_a, r_k, lnx_w, lnx_b, ww, wa, wg,
                     d_rwkv=d_rwkv, lp=lp, dw=dw, da=da)
        return y, [u.astype(BF16) for u in side]

    def side_spec(u):
        units = u.shape[0]
        return pl.BlockSpec((1,) + u.shape[1:],
                            lambda h, t: (jnp.minimum(h * nt + t, units - 1), 0, 0))

    in_specs = [act_spec(0), act_spec(nhg), act_spec(2 * nhg),
                pl.BlockSpec((b, c, lp), lambda h, t: (0, t, lblk)),
                vec_spec(0), vec_spec(nhg), vec_spec(2 * nhg),
                pl.BlockSpec((1, lp), lambda h, t: (0, lblk)),
                vec, vec, vec, vec, vec, vec, vec,
                lora_spec, lora_spec, lora_spec]
    assert len(in_specs) == N_RWKV_INPUTS
    side_mib = 2 * sum(6 * math.prod(u.shape[1:]) for u in side) // MIB
    kern = functools.partial(_rwkv_side_kernel, n_side=len(side), n_batch=b, dw=dw, da=da)
    outs = pl.pallas_call(
        kern,
        grid=(nhg, nt),
        in_specs=in_specs + [side_spec(u) for u in side],
        out_specs=[pl.BlockSpec((b, c, hw), lambda h, t: (0, t, h))] + [side_spec(u) for u in side],
        out_shape=[jax.ShapeDtypeStruct((b, s, d_rwkv), BF16)]
                  + [jax.ShapeDtypeStruct(u.shape, BF16) for u in side],
        scratch_shapes=[pltpu.VMEM((b, groups, hw1, hw1), F32),
                        pltpu.VMEM((b, 1, hw), F32), pltpu.VMEM((b, 1, hw), F32),
                        pltpu.VMEM((b, 1, hw), F32), pltpu.VMEM((b, 1, lp), F32)],
        compiler_params=_params(("arbitrary", "arbitrary"), vmem_mib=min(32 + side_mib, 58)),
    )(proj, proj, proj, proj, mu, mu, mu, mu,
      w0, a0, k_k, k_a, r_k, lnx_w, lnx_b, ww, wa, wg, *side)
    return outs[0], outs[1:]


def _attn_kernel(sink_ref, slope_ref, q_ref, kp_ref, kc_ref, vp_ref, vc_ref, og_ref, *rest, gqa, qb):
    o_ref = rest[len(rest) // 2]
    if len(rest) == 3:
        rest[2][...] = rest[0][...].astype(rest[2].dtype)
    step = pl.program_id(2)
    pair = pl.program_id(1)
    wd = WINDOW
    qi = lax.broadcasted_iota(jnp.int32, (wd, 2 * wd), 0)
    kj = lax.broadcasted_iota(jnp.int32, (wd, 2 * wd), 1)
    dist = qi + wd - kj
    band = (dist >= 0) & (dist < wd)
    valid = [band & ((kj >= wd) | (step > 0))] + [band] * (qb - 1)
    dist_f = dist.astype(F32)
    scale = HEAD_DIM ** -0.5
    k_all = jnp.concatenate([kp_ref[...], kc_ref[...]], axis=0)
    v_all = jnp.concatenate([vp_ref[...], vc_ref[...]], axis=0)
    heads = range(2 * gqa)
    units = [(sb, hl) for sb in range(qb) for hl in heads]
    nu = range(len(units))

    def kv(x, sb, hl):
        kvh = hl // gqa
        return x[sb * wd:(sb + 2) * wd, kvh * HEAD_DIM:(kvh + 1) * HEAD_DIM]

    slopes = [slope_ref[pair * (2 * gqa) + hl] for _, hl in units]
    sinks = [sink_ref[pair * (2 * gqa) + hl] for _, hl in units]
    s = [lax.dot_general(q_ref[sb * wd:(sb + 1) * wd, hl * HEAD_DIM:(hl + 1) * HEAD_DIM],
                         kv(k_all, sb, hl), (((1,), (1,)), ((), ())), preferred_element_type=F32)
         for sb, hl in units]
    s = [jnp.where(valid[units[u][0]], s[u] * scale - slopes[u] * dist_f, -jnp.inf) for u in nu]
    m = [jnp.maximum(jnp.max(s[u], axis=-1, keepdims=True), sinks[u]) for u in nu]
    p = [jnp.exp(s[u] - m[u]) for u in nu]
    denom = [jnp.sum(p[u], axis=-1, keepdims=True) + jnp.exp(sinks[u] - m[u]) for u in nu]
    o = [jnp.dot(p[u].astype(BF16), kv(v_all, *units[u]), preferred_element_type=F32) / denom[u]
         for u in nu]
    o = [x * lax.rsqrt(jnp.mean(x * x, axis=-1, keepdims=True) + NORM_EPS) for x in o]
    nh = len(heads)
    o_all = jnp.concatenate([jnp.concatenate(o[sb * nh:(sb + 1) * nh], axis=-1) for sb in range(qb)],
                            axis=0) * og_ref[...]
    o_ref[...] = o_all.astype(o_ref.dtype)


def _attention(qkv, sinks, out_g, *, batch, seq, d_attn, side=None):
    d_kv = N_KV_HEADS * HEAD_DIM
    n_q_heads = d_attn // HEAD_DIM
    gqa = n_q_heads // N_KV_HEADS
    qw = 2 * gqa * HEAD_DIM
    n_pairs = N_KV_HEADS // 2
    nb = seq // WINDOW
    assert qw % LANES == 0 and d_attn % LANES == 0
    k_off = d_attn // LANES
    v_off = (d_attn + d_kv) // LANES

    qb = 2 if nb % 2 == 0 else 1
    ns = nb // qb

    def cur(off):
        return pl.BlockSpec((qb * WINDOW, LANES), lambda b, p, i, *_: (b * ns + i, off + p))

    def prev(off):
        return pl.BlockSpec((WINDOW, LANES),
                            lambda b, p, i, *_: (b * nb + jnp.maximum(qb * i - 1, 0), off + p))

    slopes = jnp.exp2(-8.0 * jnp.arange(1, n_q_heads + 1, dtype=F32) / n_q_heads)
    if side is not None and side.shape[0] > batch * n_pairs * ns:
        return _attention(qkv, sinks, out_g, batch=batch, seq=seq, d_attn=d_attn), side.astype(BF16)
    sides = [] if side is None else [side]

    def side_spec(u):
        units = u.shape[0]
        return pl.BlockSpec((1,) + u.shape[1:], lambda b, p, i, *_: (
            jnp.minimum((b * n_pairs + p) * ns + i, units - 1), 0, 0))

    grid_spec = pltpu.PrefetchScalarGridSpec(
        num_scalar_prefetch=2,
        grid=(batch, n_pairs, ns),
        in_specs=[pl.BlockSpec((qb * WINDOW, qw), lambda b, p, i, *_: (b * ns + i, p)),
                  prev(k_off), cur(k_off), prev(v_off), cur(v_off),
                  pl.BlockSpec((1, qw), lambda b, p, i, *_: (0, p))] + [side_spec(u) for u in sides],
        out_specs=[pl.BlockSpec((qb * WINDOW, qw), lambda b, p, i, *_: (b * ns + i, p))]
                  + [side_spec(u) for u in sides],
    )
    outs = pl.pallas_call(
        functools.partial(_attn_kernel, gqa=gqa, qb=qb),
        grid_spec=grid_spec,
        out_shape=[jax.ShapeDtypeStruct((batch * seq, d_attn), BF16)]
                  + [jax.ShapeDtypeStruct(u.shape, BF16) for u in sides],
        compiler_params=_params(("arbitrary", "arbitrary", "arbitrary")),
    )(sinks, slopes, qkv, qkv, qkv, qkv, qkv, out_g.reshape(1, d_attn), *sides)
    return outs[0] if side is None else (outs[0], outs[1])


def _outproj_kernel(yr_ref, ya_ref, w1_ref, w2_ref, x_ref, g_ref, o_ref):
    acc = jnp.dot(yr_ref[...], w1_ref[...], preferred_element_type=F32)
    acc = acc + jnp.dot(ya_ref[...], w2_ref[...], preferred_element_type=F32)
    o_ref[...] = x_ref[...] + g_ref[0] * acc


def _outproj(y_rwkv, y_attn, w_out, x2d, gate, seq):
    t, d = x2d.shape
    k1 = y_rwkv.shape[1]
    k2 = y_attn.shape[1]
    assert k1 == k2
    tm = _pick_tile(seq, (1024, 512, 256, 128))
    tn = _pick_tile(d, (512, 256, 128))
    per_b = seq // tm
    return pl.pallas_call(
        _outproj_kernel,
        grid=(t // tm, d // tn),
        in_specs=[pl.BlockSpec((tm, k1), lambda i, j: (i, 0)),
                  pl.BlockSpec((tm, k2), lambda i, j: (i, 0)),
                  pl.BlockSpec((k1, tn), lambda i, j: (0, j)),
                  pl.BlockSpec((k2, tn), lambda i, j: (1, j)),
                  pl.BlockSpec((tm, tn), lambda i, j: (i, j)),
                  pl.BlockSpec((1, 1, tn), lambda i, j: (i // per_b, 0, j))],
        out_specs=pl.BlockSpec((tm, tn), lambda i, j: (i, j)),
        out_shape=jax.ShapeDtypeStruct((t, d), F32),
        compiler_params=_params(("arbitrary", "arbitrary")),
    )(y_rwkv, y_attn, w_out, w_out, x2d, gate[:, None, :])


U32 = jnp.uint32


def _pack_pairs(x):
    n = x.shape[1] // 2
    lo = lax.bitcast_convert_type(x[:, :n].astype(BF16).astype(F32), U32)
    hi = lax.bitcast_convert_type(x[:, n:].astype(BF16).astype(F32), U32)
    return (lo >> 16) | hi


def _unpack_pairs(w):
    lo = lax.bitcast_convert_type(w << 16, F32)
    hi = lax.bitcast_convert_type(w & jnp.uint32(0xFFFF0000), F32)
    return jnp.concatenate([lo, hi], axis=-1)


def _slab_pitch(ns):
    return ns + 8 if ns % 16 == 0 else _round_up(ns, 8)


def _to_slabs(ref, x):
    rows, d = x.shape
    ns = d // LANES
    pitch = _slab_pitch(ns)
    ref[...] = jnp.zeros_like(ref)
    for s in range(ns):
        ref[pl.ds(s, rows, stride=pitch), :] = x[:, s * LANES:(s + 1) * LANES]


def _from_slabs(ref, ns):
    pitch = _slab_pitch(ns)
    rows = ref.shape[0] // pitch
    return jnp.concatenate([ref[pl.ds(s, rows, stride=pitch), :] for s in range(ns)], axis=-1)


def _norm_route_kernel(x_ref, g_ref, sc_ref, sh_ref, w_ref, b_ref, u_ref, o_ref):
    x = x_ref[...]
    y = x * lax.rsqrt(jnp.mean(x * x, axis=-1, keepdims=True) + NORM_EPS)
    y = y * g_ref[...]
    y = y * (1.0 + sc_ref[0]) + sh_ref[0]
    _to_slabs(u_ref, _pack_pairs(y))
    logits = jnp.dot(y.astype(BF16), w_ref[...], preferred_element_type=F32) + b_ref[...]
    col = lax.broadcasted_iota(jnp.int32, logits.shape, 1).astype(F32)
    ng = float(N_GROUPS)
    epg = float(EXPERTS_PER_GROUP)
    big = 1e9
    is_g = col < ng
    lg = jnp.where(is_g, logits, -jnp.inf)
    mg = jnp.max(lg, axis=-1, keepdims=True)
    gidx = jnp.min(jnp.where(lg == mg, col, big), axis=-1, keepdims=True)
    zg = jnp.sum(jnp.where(is_g, jnp.exp(lg - mg), 0.0), axis=-1, keepdims=True)
    g_w = 1.0 / zg
    lo = ng + gidx * epg
    in_grp = (col >= lo) & (col < lo + epg)
    le = jnp.where(in_grp, logits, -jnp.inf)
    m1 = jnp.max(le, axis=-1, keepdims=True)
    i1 = jnp.min(jnp.where(le == m1, col, big), axis=-1, keepdims=True)
    le2 = jnp.where(col == i1, -jnp.inf, le)
    m2 = jnp.max(le2, axis=-1, keepdims=True)
    i2 = jnp.min(jnp.where(le2 == m2, col, big), axis=-1, keepdims=True)
    e2 = jnp.exp(m2 - m1)
    w1 = 1.0 / (1.0 + e2)
    w2 = e2 / (1.0 + e2)
    out = jnp.where(col == 0.0, i1 - ng,
                    jnp.where(col == 1.0, i2 - ng,
                              jnp.where(col == 2.0, g_w * w1,
                                        jnp.where(col == 3.0, g_w * w2, 0.0))))
    o_ref[...] = out


def _norm_route(x2d, g, scale, shift, w_r, b_r, seq):
    t, d = x2d.shape
    tm = _pick_tile(seq, (256, 128))
    per_b = seq // tm
    ns = d // (2 * LANES)
    return pl.pallas_call(
        _norm_route_kernel,
        grid=(t // tm,),
        in_specs=[pl.BlockSpec((tm, d), lambda i: (i, 0)),
                  pl.BlockSpec((1, d), lambda i: (0, 0)),
                  pl.BlockSpec((1, 1, d), lambda i: (i // per_b, 0, 0)),
                  pl.BlockSpec((1, 1, d), lambda i: (i // per_b, 0, 0)),
                  pl.BlockSpec((d, LANES), lambda i: (0, 0)),
                  pl.BlockSpec((1, LANES), lambda i: (0, 0))],
        out_specs=[pl.BlockSpec((tm * _slab_pitch(ns), LANES), lambda i: (i, 0)),
                   pl.BlockSpec((tm, LANES), lambda i: (i, 0))],
        out_shape=[jax.ShapeDtypeStruct((t * _slab_pitch(ns), LANES), U32),
                   jax.ShapeDtypeStruct((t, LANES), F32)],
        compiler_params=_params(("arbitrary",)),
    )(x2d, g.reshape(1, d), scale[:, None, :], shift[:, None, :], w_r, b_r)


GATHER_GROUP = 8


def _gather_rows(idx_ref, n_rows, src_hbm, dst, sem, col0=0, unroll=False, slab_rows=0):
    pitch = _slab_pitch(slab_rows) if slab_rows else 1
    size = slab_rows if slab_rows else 1

    def issue(r):
        tok = idx_ref[0, 0, col0 + r]
        if slab_rows:
            src = src_hbm.at[pl.ds(pl.multiple_of(tok * pitch, 8), size)]
            dst_r = dst.at[pl.ds(pl.multiple_of(r * pitch, 8), size)]
        else:
            src, dst_r = src_hbm.at[pl.ds(tok, 1)], dst.at[pl.ds(r, 1)]
        pltpu.make_async_copy(src, dst_r, sem).start()

    if unroll is True:
        for r in range(n_rows):
            issue(r)
    elif isinstance(n_rows, int):
        def body(r, carry):
            issue(r)
            return carry
        lax.fori_loop(0, n_rows, body, 0, unroll=max(int(unroll), 1))
    else:
        def group(g, carry):
            for k in range(GATHER_GROUP):
                issue(g * GATHER_GROUP + k)
            return carry
        lax.fori_loop(0, n_rows // GATHER_GROUP, group, 0)


def _wait_rows(n_rows, src_hbm, dst, sem):
    if not isinstance(n_rows, int):
        n_rows = pl.multiple_of(n_rows, 8)
    pltpu.make_async_copy(src_hbm.at[pl.ds(0, n_rows)], dst.at[pl.ds(0, n_rows)], sem).wait()


def _expert_kernel(be_ref, nu_ref, nv_ref, tok_ref, tokn_ref, u_hbm, wg_ref, wu_ref, wd_ref,
                   o_ref, xbuf_a, xbuf_b, sem):
    i = pl.program_id(0)
    last = pl.num_programs(0) - 1
    n_used = nu_ref[0]
    rows, n_words = o_ref.shape
    ns = n_words // LANES
    n_next = jnp.where(i < last, nv_ref[jnp.minimum(i + 1, last)], 0)

    @pl.when(i == 0)
    def _():
        xbuf_a[...] = jnp.zeros_like(xbuf_a)
        xbuf_b[...] = jnp.zeros_like(xbuf_b)
        _gather_rows(tok_ref, nv_ref[0], u_hbm, xbuf_a, sem.at[0], slab_rows=ns)

    def step(cur, cur_sem, nxt, nxt_sem):
        @pl.when(i < n_used)
        def _():
            _wait_rows(nv_ref[i] * ns, u_hbm, cur, cur_sem)
            _gather_rows(tokn_ref, n_next, u_hbm, nxt, nxt_sem, slab_rows=ns)
            xb = _unpack_pairs(_from_slabs(cur, ns)).astype(BF16)
            gate = jnp.dot(xb, wg_ref[0], preferred_element_type=F32)
            up = jnp.dot(xb, wu_ref[0], preferred_element_type=F32)
            h = (gate * jax.nn.sigmoid(gate) * up).astype(BF16)
            o_ref[...] = _pack_pairs(jnp.dot(h, wd_ref[0], preferred_element_type=F32))

        @pl.when(i >= n_used)
        def _():
            o_ref[...] = jnp.zeros_like(o_ref)

    @pl.when(i % 2 == 0)
    def _():
        step(xbuf_a, sem.at[0], xbuf_b, sem.at[1])

    @pl.when(i % 2 == 1)
    def _():
        step(xbuf_b, sem.at[1], xbuf_a, sem.at[0])


def _experts(block_expert, n_used, n_valid, row_tok, u2, wg, wu, wd):
    d = wg.shape[1]
    n_words = d // 2
    pitch = _slab_pitch(n_words // LANES)
    nb = block_expert.shape[0]
    rows = MOE_ROWS
    de = wg.shape[-1]
    tok3 = row_tok.reshape(nb, 1, rows)
    grid_spec = pltpu.PrefetchScalarGridSpec(
        num_scalar_prefetch=3,
        grid=(nb,),
        in_specs=[pl.BlockSpec((1, 1, rows), lambda i, be, nu, nv: (i, 0, 0), memory_space=pltpu.SMEM),
                  pl.BlockSpec((1, 1, rows), lambda i, be, nu, nv: (jnp.minimum(i + 1, nb - 1), 0, 0),
                               memory_space=pltpu.SMEM),
                  pl.BlockSpec(memory_space=pl.ANY),
                  pl.BlockSpec((1, d, de), lambda i, be, nu, nv: (be[i], 0, 0)),
                  pl.BlockSpec((1, d, de), lambda i, be, nu, nv: (be[i], 0, 0)),
                  pl.BlockSpec((1, de, d), lambda i, be, nu, nv: (be[i], 0, 0))],
        out_specs=pl.BlockSpec((rows, n_words), lambda i, be, nu, nv: (i, 0)),
        scratch_shapes=[pltpu.VMEM((rows * pitch, LANES), U32),
                        pltpu.VMEM((rows * pitch, LANES), U32),
                        pltpu.SemaphoreType.DMA((2,))],
    )
    return pl.pallas_call(
        _expert_kernel,
        grid_spec=grid_spec,
        out_shape=jax.ShapeDtypeStruct((nb * rows, n_words), U32),
        compiler_params=_params(("arbitrary",), vmem_mib=58),
    )(block_expert, n_used, n_valid, tok3, tok3, u2, wg, wu, wd)


def _combine_kernel(pos_ref, posn_ref, ys_hbm, r_ref, x_ref, g_ref, ng_ref, o_ref, buf_a, buf_b, sem):
    i = pl.program_id(0)
    last = pl.num_programs(0) - 1
    tc = x_ref.shape[0]

    def gather(idx_ref, buf, s, unroll):
        _gather_rows(idx_ref, tc, ys_hbm, buf.at[0], s, 0, unroll)
        _gather_rows(idx_ref, tc, ys_hbm, buf.at[1], s, tc, unroll)

    def wait(buf, s):
        _wait_rows(tc, ys_hbm, buf.at[0], s)
        _wait_rows(tc, ys_hbm, buf.at[1], s)

    @pl.when(i == 0)
    def _():
        gather(pos_ref, buf_a, sem.at[0], False)

    def step(cur, cur_sem, nxt, nxt_sem):
        wait(cur, cur_sem)
        gather(posn_ref, nxt, nxt_sem, True)
        moe = _unpack_pairs(cur[0]) * r_ref[:, 2:3] + _unpack_pairs(cur[1]) * r_ref[:, 3:4]
        x = x_ref[...] + g_ref[0] * moe
        y = x * lax.rsqrt(jnp.mean(x * x, axis=-1, keepdims=True) + NORM_EPS)
        o_ref[...] = y * ng_ref[...]

        @pl.when(i == last)
        def _():
            wait(nxt, nxt_sem)

    @pl.when(i % 2 == 0)
    def _():
        step(buf_a, sem.at[0], buf_b, sem.at[1])

    @pl.when(i % 2 == 1)
    def _():
        step(buf_b, sem.at[1], buf_a, sem.at[0])


def _combine(pos, route, ys, x1, gate, norm_g, seq):
    t, d = x1.shape
    tc = _pick_tile(seq, (128,))
    n = t // tc
    per_b = seq // tc
    pos3 = pos.reshape(n, tc, 2).transpose(0, 2, 1).reshape(n, 1, 2 * tc)
    return pl.pallas_call(
        _combine_kernel,
        grid=(n,),
        in_specs=[pl.BlockSpec((1, 1, 2 * tc), lambda i: (i, 0, 0), memory_space=pltpu.SMEM),
                  pl.BlockSpec((1, 1, 2 * tc), lambda i: (jnp.minimum(i + 1, n - 1), 0, 0),
                               memory_space=pltpu.SMEM),
                  pl.BlockSpec(memory_space=pl.ANY),
                  pl.BlockSpec((tc, LANES), lambda i: (i, 0)),
                  pl.BlockSpec((tc, d), lambda i: (i, 0)),
                  pl.BlockSpec((1, 1, d), lambda i: (i // per_b, 0, 0)),
                  pl.BlockSpec((1, d), lambda i: (0, 0))],
        out_specs=pl.BlockSpec((tc, d), lambda i: (i, 0)),
        out_shape=jax.ShapeDtypeStruct((t, d), F32),
        scratch_shapes=[pltpu.VMEM((2, tc, d // 2), U32), pltpu.VMEM((2, tc, d // 2), U32),
                        pltpu.SemaphoreType.DMA((2,))],
        compiler_params=_params(("arbitrary",)),
    )(pos3, pos3, ys, route, x1, gate[:, None, :], norm_g.reshape(1, d))


def _dispatch(expert_id, n_tok):
    m = expert_id.shape[0]
    rows = MOE_ROWS
    nb = -(-m // rows) + N_EXPERTS
    onehot = (expert_id[:, None] == jnp.arange(N_EXPERTS, dtype=jnp.int32)[None, :]).astype(jnp.int32)
    csum = jnp.cumsum(onehot, axis=0)
    rank = jnp.take_along_axis(csum, expert_id[:, None], axis=1)[:, 0] - 1
    counts = csum[-1]
    padded = ((counts + rows - 1) // rows) * rows
    pends = jnp.cumsum(padded)
    pstarts = pends - padded
    dest = (pstarts[expert_id] + rank).astype(jnp.int32)
    tok = jnp.arange(m, dtype=jnp.int32) // (m // n_tok)
    row_tok = jnp.zeros((nb * rows,), jnp.int32).at[dest].set(tok)
    n_used = (pends[-1] // rows).astype(jnp.int32)
    all_blk = jnp.arange(nb, dtype=jnp.int32)
    blk = jnp.minimum(all_blk, n_used - 1)
    block_expert = jnp.minimum(jnp.searchsorted(pends, blk * rows, side='right'),
                               N_EXPERTS - 1).astype(jnp.int32)
    seg_end = (pstarts + counts)[block_expert]
    n_valid = jnp.where(all_blk < n_used, jnp.clip(seg_end - all_blk * rows, 0, rows), 0)
    n_valid = ((n_valid + GATHER_GROUP - 1) // GATHER_GROUP) * GATHER_GROUP
    return block_expert, n_used.reshape(1), n_valid.astype(jnp.int32), row_tok, dest


def _layer(x, c, w_cond, b_cond, norm1_g, w_in, rwkv_mu, rwkv_w0, rwkv_w_up, rwkv_a0, rwkv_a_up,
           rwkv_g_up, rwkv_k_k, rwkv_k_a, rwkv_r_k, rwkv_lnx_w, rwkv_lnx_b, attn_sinks, attn_out_g,
           w_out, norm2_g, router_group, router_group_bias, router_expert, router_expert_bias,
           expert_w_gate, expert_w_up, expert_w_down):
    b, s, d = x.shape
    t = b * s
    d_rwkv = rwkv_w0.shape[-1]
    d_attn = attn_out_g.shape[-1]
    dw, da, dg = rwkv_w_up.shape[0], rwkv_a_up.shape[0], rwkv_g_up.shape[0]
    n_lora = dw + da + dg
    lp = _round_up(n_lora, LANES)
    while (3 * d_rwkv) % lp:
        lp += LANES
    n_rwkv_cols = 3 * d_rwkv + n_lora

    c_pad = jnp.zeros((_round_up(b, 8), d), F32).at[:b].set(c)
    mod = _cond(c_pad, w_cond, b_cond.reshape(1, -1))[:b]
    sh1, sc1, g1, sh2, sc2, g2 = jnp.split(mod, 6, axis=-1)

    x2d = x.reshape(t, d)
    u1 = _norm_mod(x2d, norm1_g, sc1, sh1, s, BF16)

    zpad = jnp.zeros((d, lp - n_lora), F32)
    w_r = jnp.concatenate([w_in[:, :n_rwkv_cols], zpad], axis=1).astype(BF16)
    w_a = w_in[:, n_rwkv_cols:].astype(BF16)
    proj_r, _ = _matmul(u1, w_r, BF16)
    proj_a, _ = _matmul(u1, w_a, BF16)

    def lora_rows(wmat, off):
        return jnp.zeros((lp, d_rwkv), F32).at[off:off + wmat.shape[0]].set(wmat).astype(BF16)

    mu = jnp.concatenate([rwkv_mu, jnp.zeros((lp - n_lora,), F32)]).reshape(1, -1)
    row = lambda v: v.reshape(1, d_rwkv)
    n_exp = expert_w_gate.shape[0]
    halves = lambda w: w.reshape(2 * n_exp, w.shape[1] // 2, w.shape[2])
    y_rwkv, (wg_b, wu_b) = _rwkv(
        proj_r.reshape(b, s, -1), mu, row(rwkv_w0), row(rwkv_a0), row(rwkv_k_k),
        row(rwkv_k_a), row(rwkv_r_k), row(rwkv_lnx_w), row(rwkv_lnx_b),
        lora_rows(rwkv_w_up, 0), lora_rows(rwkv_a_up, dw), lora_rows(rwkv_g_up, dw + da),
        d_rwkv=d_rwkv, lp=lp, dw=dw, da=da, side=(halves(expert_w_gate), halves(expert_w_up)))
    y_rwkv = y_rwkv.reshape(t, d_rwkv)
    wg_b = wg_b.reshape(expert_w_gate.shape)
    wu_b = wu_b.reshape(expert_w_up.shape)

    quarters = lambda w: w.reshape(4 * n_exp, w.shape[1] // 4, w.shape[2])
    y_attn, wd_b = _attention(proj_a, attn_sinks, attn_out_g, batch=b, seq=s, d_attn=d_attn,
                              side=quarters(expert_w_down))
    wd_b = wd_b.reshape(expert_w_down.shape)

    x1 = _outproj(y_rwkv, y_attn, w_out.astype(BF16), x2d, g1, s)

    n_r = N_GROUPS + N_EXPERTS
    w_router = jnp.concatenate([router_group, router_expert, jnp.zeros((d, LANES - n_r), F32)],
                               axis=1).astype(BF16)
    b_router = jnp.concatenate([router_group_bias, router_expert_bias,
                                jnp.zeros((LANES - n_r,), F32)]).reshape(1, LANES)
    u2, route = _norm_route(x1, norm2_g, sc2, sh2, w_router, b_router, s)
    expert_id = route[:, :2].astype(jnp.int32).reshape(-1)
    block_expert, n_used, n_valid, row_tok, dest = _dispatch(expert_id, t)
    ys = _experts(block_expert, n_used, n_valid, row_tok, u2, wg_b, wu_b, wd_b)
    return ys, dest.reshape(t, 2), route, x1, g2


def kernel(x, c, w_cond, b_cond, norm1_g, w_in, rwkv_mu, rwkv_w0, rwkv_w_up, rwkv_a0, rwkv_a_up, rwkv_g_up, rwkv_k_k, rwkv_k_a, rwkv_r_k, rwkv_lnx_w, rwkv_lnx_b, attn_sinks, attn_out_g, w_out, norm2_g, router_group, router_group_bias, router_expert, router_expert_bias, expert_w_gate, expert_w_up, expert_w_down, norm_f_g):
    b, s, d = x.shape
    depth = w_cond.shape[0]
    assert depth == 1, "the fused final norm assumes a single layer"
    l = 0
    ys, pos, route, x1, g2 = _layer(
        x, c, w_cond[l], b_cond[l], norm1_g[l], w_in[l], rwkv_mu[l], rwkv_w0[l], rwkv_w_up[l],
        rwkv_a0[l], rwkv_a_up[l], rwkv_g_up[l], rwkv_k_k[l], rwkv_k_a[l], rwkv_r_k[l],
        rwkv_lnx_w[l], rwkv_lnx_b[l], attn_sinks[l], attn_out_g[l], w_out[l], norm2_g[l],
        router_group[l], router_group_bias[l], router_expert[l], router_expert_bias[l],
        expert_w_gate[l], expert_w_up[l], expert_w_down[l])
    out = _combine(pos, route, ys, x1, g2, norm_f_g, s)
    return out.reshape(b, s, d)
```

```python
import functools
import math

import jax
import jax.numpy as jnp
from jax import lax
from jax.experimental import pallas as pl
from jax.experimental.pallas import tpu as pltpu

F32 = jnp.float32
BF16 = jnp.bfloat16

HEAD_DIM = 64
N_KV_HEADS = 8
WINDOW = 128
N_GROUPS = 8
EXPERTS_PER_GROUP = 8
N_EXPERTS = N_GROUPS * EXPERTS_PER_GROUP
RWKV_GN_EPS = 64e-5
NORM_EPS = 1e-6
RWKV_CHUNK = 64
RWKV_HEADS_PER_STEP = 4
RWKV_GROUPS_PER_STEP = 4
MOE_ROWS = 256
LANES = 128
MIB = 1024 * 1024


def _params(sem, vmem_mib=48):
    return pltpu.CompilerParams(dimension_semantics=sem, vmem_limit_bytes=vmem_mib * MIB)


def _round_up(n, m):
    return -(-n // m) * m


def _pick_tile(n, candidates):
    for t in candidates:
        if n % t == 0:
            return t
    raise ValueError(f"no tile in {candidates} divides {n}")


def _cond_kernel(c_ref, w_ref, b_ref, o_ref):
    c = c_ref[...]
    sc = (c * jax.nn.sigmoid(c)).astype(BF16)
    o_ref[...] = jnp.dot(sc, w_ref[...].astype(BF16), preferred_element_type=F32) + b_ref[...]


def _cond(c_pad, w, b):
    rows, d = c_pad.shape
    n = w.shape[1]
    tn = _pick_tile(n, (1024, 512, 256, 128))
    return pl.pallas_call(
        _cond_kernel,
        grid=(n // tn,),
        in_specs=[pl.BlockSpec((rows, d), lambda j: (0, 0)),
                  pl.BlockSpec((d, tn), lambda j: (0, j)),
                  pl.BlockSpec((1, tn), lambda j: (0, j))],
        out_specs=pl.BlockSpec((rows, tn), lambda j: (0, j)),
        out_shape=jax.ShapeDtypeStruct((rows, n), F32),
        compiler_params=_params(("arbitrary",)),
    )(c_pad, w, b)


def _norm_mod_kernel(x_ref, g_ref, sc_ref, sh_ref, o_ref):
    x = x_ref[...]
    y = x * lax.rsqrt(jnp.mean(x * x, axis=-1, keepdims=True) + NORM_EPS)
    y = y * g_ref[...]
    o_ref[...] = (y * (1.0 + sc_ref[0]) + sh_ref[0]).astype(o_ref.dtype)


def _norm_mod(x2d, g, scale, shift, seq, out_dtype):
    t, d = x2d.shape
    tm = _pick_tile(seq, (512, 256, 128))
    per_b = seq // tm
    return pl.pallas_call(
        _norm_mod_kernel,
        grid=(t // tm,),
        in_specs=[pl.BlockSpec((tm, d), lambda i: (i, 0)),
                  pl.BlockSpec((1, d), lambda i: (0, 0)),
                  pl.BlockSpec((1, 1, d), lambda i: (i // per_b, 0, 0)),
                  pl.BlockSpec((1, 1, d), lambda i: (i // per_b, 0, 0))],
        out_specs=pl.BlockSpec((tm, d), lambda i: (i, 0)),
        out_shape=jax.ShapeDtypeStruct((t, d), out_dtype),
        compiler_params=_params(("arbitrary",)),
    )(x2d, g.reshape(1, d), scale[:, None, :], shift[:, None, :])


def _mm_kernel(a_ref, b_ref, o_ref):
    o_ref[...] = jnp.dot(a_ref[...], b_ref[...], preferred_element_type=F32).astype(o_ref.dtype)


def _matmul(a, b, out_dtype):
    m, k = a.shape
    n = b.shape[1]
    tm = _pick_tile(m, (1024, 512, 256, 128))
    tn = _pick_tile(n, (512, 256, 128))
    return pl.pallas_call(
        _mm_kernel,
        grid=(m // tm, n // tn),
        in_specs=[pl.BlockSpec((tm, k), lambda i, j: (i, 0)),
                  pl.BlockSpec((k, tn), lambda i, j: (0, j))],
        out_specs=pl.BlockSpec((tm, tn), lambda i, j: (i, j)),
        out_shape=jax.ShapeDtypeStruct((m, n), out_dtype),
        compiler_params=_params(("arbitrary", "arbitrary")),
    )(a, b)


def _softplus(z):
    return jnp.maximum(z, 0.0) + jnp.log(1.0 + jnp.exp(-jnp.abs(z)))


def _rwkv_kernel(r_ref, k_ref, v_ref, l_ref, mur_ref, muk_ref, muv_ref, mul_ref,
                 w0_ref, a0_ref, kk_ref, ka_ref, rk_ref, lw_ref, lb_ref,
                 ww_ref, wa_ref, wg_ref, o_ref,
                 s_ref, pr_ref, pk_ref, pv_ref, pl_ref, *, n_batch, dw, da):
    c = RWKV_CHUNK
    nh = RWKV_HEADS_PER_STEP
    hw = nh * HEAD_DIM
    groups = r_ref.shape[-1] // hw
    lp = l_ref.shape[-1]
    chunk = pl.program_id(1)

    @pl.when(chunk == 0)
    def _():
        s_ref[...] = jnp.zeros_like(s_ref)
        pr_ref[...] = jnp.zeros_like(pr_ref)
        pk_ref[...] = jnp.zeros_like(pk_ref)
        pv_ref[...] = jnp.zeros_like(pv_ref)
        pl_ref[...] = jnp.zeros_like(pl_ref)

    row_w = lax.broadcasted_iota(jnp.int32, (c, hw), 0)
    col_w = lax.broadcasted_iota(jnp.int32, (c, hw), 1)
    s_idx = col_w % c
    strict = row_w > s_idx
    incl = row_w >= s_idx
    eye_w = (row_w == s_idx).astype(F32)
    merge_masks = [strict & (row_w // 2 == s_idx // 2)]
    size = 2
    while size < c:
        merge_masks.append(strict & (row_w // (2 * size) == s_idx // (2 * size))
                           & (row_w // size != s_idx // size))
        size *= 2
    bd_row = lax.broadcasted_iota(jnp.int32, (hw, hw), 0) // HEAD_DIM
    bd_col = lax.broadcasted_iota(jnp.int32, (hw, hw), 1) // HEAD_DIM
    bd_mask = bd_row == bd_col
    bd_ones = bd_mask.astype(BF16)
    tri = (lax.broadcasted_iota(jnp.int32, (c, c), 0)
           >= lax.broadcasted_iota(jnp.int32, (c, c), 1)).astype(BF16)
    row_l = lax.broadcasted_iota(jnp.int32, (c, lp), 0)
    col_l = lax.broadcasted_iota(jnp.int32, (c, lp), 1)

    def shift_mix(x, prev, mu, row):
        shifted = jnp.where(row == 0, prev, pltpu.roll(x, 1, axis=0))
        return x + (shifted - x) * mu

    def split2(x):
        hi = x.astype(BF16)
        lo = (x - hi.astype(F32)).astype(BF16)
        return hi, lo

    chains = [(bi, gi) for gi in range(groups) for bi in range(n_batch)]
    nb = range(len(chains))

    def cols(x, gi):
        return x[..., gi * hw:(gi + 1) * hw]

    def vec(ref, ci):
        return cols(ref[...], chains[ci][1])

    def rows_split(x):
        return [x[i * c:(i + 1) * c] for i in range(x.shape[0] // c)]

    def chain_split(x):
        return [cols(x[bi * c:(bi + 1) * c], gi) for bi, gi in chains]

    def seg_sum(xs):
        stacked = jnp.concatenate([x.astype(BF16) for x in xs], axis=0)
        return rows_split(jnp.dot(stacked, bd_ones, preferred_element_type=F32))

    def bd(x):
        return jnp.where(bd_mask, jnp.tile(x, (nh, 1)), 0.0).astype(BF16)

    def mm(a, b):
        return jnp.dot(a.astype(BF16), b, preferred_element_type=F32)

    def mm_nt(a, b):
        return lax.dot_general(a.astype(BF16), b, (((1,), (1,)), ((), ())),
                               preferred_element_type=F32)

    def mixed(x_ref, p_ref, mu_ref):
        return [shift_mix(cols(x_ref[bi], gi).astype(F32), cols(p_ref[bi], gi),
                          cols(mu_ref[...], gi), row_w) for bi, gi in chains]

    xr = mixed(r_ref, pr_ref, mur_ref)
    xk = mixed(k_ref, pk_ref, muk_ref)
    xv = mixed(v_ref, pv_ref, muv_ref)
    xl = [shift_mix(l_ref[bi].astype(F32), pl_ref[bi], mul_ref[...], row_l) for bi in range(n_batch)]
    states = [s_ref[bi, gi] for bi, gi in chains]

    act = jnp.concatenate(
        [jnp.where(col_l < dw, jnp.tanh(x), jnp.where(col_l < dw + da, x, jax.nn.sigmoid(x)))
         for x in xl], axis=0).astype(BF16)
    kw = _round_up(dw, LANES)
    ka = _round_up(dw + da, LANES)
    lora_w = chain_split(jnp.dot(act[:, :kw], ww_ref[:kw, :], preferred_element_type=F32))
    lora_a = chain_split(jnp.dot(act[:, :ka], wa_ref[:ka, :], preferred_element_type=F32))
    gate = chain_split(jnp.dot(act, wg_ref[...], preferred_element_type=F32))

    e = [jnp.exp(-_softplus(-(vec(w0_ref, ci) + lora_w[ci])) - 0.5) for ci in nb]
    lr = [jax.nn.sigmoid(vec(a0_ref, ci) + lora_a[ci]) for ci in nb]
    kk = [xk[ci] * vec(kk_ref, ci) for ci in nb]
    kk_ss = seg_sum([x * x for x in kk])
    kk = [x / jnp.maximum(jnp.sqrt(ss), 1e-12) for x, ss in zip(kk, kk_ss)]
    kmod = [xk[ci] * (1.0 + (lr[ci] - 1.0) * vec(ka_ref, ci)) for ci in nb]

    cum = []
    for ei in e:
        e_hi, e_lo = split2(ei)
        cum2 = jnp.dot(tri, jnp.concatenate([e_hi, e_lo], axis=1), preferred_element_type=F32)
        cum.append(cum2[:, :hw] + cum2[:, hw:])
    dec = [jnp.exp(-x) for x in cum]
    inv = [jnp.exp(x) for x in cum]
    at = [-kk[bi] * jnp.exp(e[bi] - cum[bi]) for bi in nb]
    rt = [xr[bi] * dec[bi] for bi in nb]
    bt = [kk[bi] * lr[bi] * inv[bi] for bi in nb]
    kt = [kmod[bi] * inv[bi] for bi in nb]

    ar = [jnp.concatenate([at[bi], rt[bi]], axis=0).astype(BF16) for bi in nb]
    g = [mm_nt(ar[bi], jnp.concatenate([bd(bt[bi]), bd(kt[bi])], axis=0)) for bi in nb]
    m_ab = [jnp.where(strict, x[:c, :hw], 0.0) for x in g]
    m_ak = [jnp.where(strict, x[:c, hw:], 0.0) for x in g]
    m_rb = [jnp.where(incl, x[c:, :hw], 0.0) for x in g]
    m_rk = [jnp.where(incl, x[c:, hw:], 0.0) for x in g]

    t_inv = [eye_w + jnp.where(merge_masks[0], m, 0.0) for m in m_ab]
    for mask in merge_masks[1:]:
        z = [mm(jnp.where(mask, m_ab[bi], 0.0), bd(t_inv[bi])) for bi in nb]
        t_inv = [t_inv[bi] + mm(t_inv[bi], bd(z[bi])) for bi in nb]

    s_bf = [s.astype(BF16) for s in states]
    bd_v = [bd(x) for x in xv]
    ar_s = [mm_nt(ar[bi], s_bf[bi]) for bi in nb]
    mv = [mm(jnp.concatenate([m_ak[bi], m_rk[bi]], axis=0), bd_v[bi]) for bi in nb]
    u = [mm(t_inv[bi], bd(ar_s[bi][:c] + mv[bi][:c])) for bi in nb]
    y = [ar_s[bi][c:] + mm(m_rb[bi], bd(u[bi])) + mv[bi][c:] for bi in nb]
    upd = [lax.dot_general(jnp.concatenate([u[bi], xv[bi]], axis=0).astype(BF16),
                           jnp.concatenate([bt[bi], kt[bi]], axis=0).astype(BF16),
                           (((0,), (0,)), ((), ())), preferred_element_type=F32) for bi in nb]

    mean = [m * (1.0 / HEAD_DIM) for m in seg_sum(y)]
    dy = [y[bi] - mean[bi] for bi in nb]
    var = [v * (1.0 / HEAD_DIM) for v in seg_sum([d * d for d in dy])]
    rk_sum = seg_sum([xr[ci] * kmod[ci] * vec(rk_ref, ci) for ci in nb])

    for ci, (bi, gi) in enumerate(chains):
        yn = dy[ci] * lax.rsqrt(var[ci] + RWKV_GN_EPS) * vec(lw_ref, ci) + vec(lb_ref, ci)
        o_ref[bi, :, gi * hw:(gi + 1) * hw] = ((yn + rk_sum[ci] * xv[ci]) * gate[ci]).astype(o_ref.dtype)
        s_ref[bi, gi] = (states[ci] + jnp.where(bd_mask, upd[ci], 0.0)) * dec[ci][c - 1:c]
    def last_row(x_ref, bi):
        return x_ref[bi, c - 16:c, :].astype(F32)[15:16]

    for bi in range(n_batch):
        pr_ref[bi] = last_row(r_ref, bi)
        pk_ref[bi] = last_row(k_ref, bi)
        pv_ref[bi] = last_row(v_ref, bi)
        pl_ref[bi] = last_row(l_ref, bi)


N_RWKV_INPUTS = 18
N_RWKV_SCRATCH = 5


def _rwkv_side_kernel(*refs, n_side, **kw):
    ins = refs[:N_RWKV_INPUTS]
    side_in = refs[N_RWKV_INPUTS:N_RWKV_INPUTS + n_side]
    o_ref = refs[N_RWKV_INPUTS + n_side]
    side_out = refs[N_RWKV_INPUTS + n_side + 1:N_RWKV_INPUTS + 2 * n_side + 1]
    scratch = refs[N_RWKV_INPUTS + 2 * n_side + 1:]
    assert len(scratch) == N_RWKV_SCRATCH
    _rwkv_kernel(*ins, o_ref, *scratch, **kw)
    for src, dst in zip(side_in, side_out):
        dst[...] = src[...].astype(dst.dtype)


def _rwkv(proj, mu, w0, a0, k_k, k_a, r_k, lnx_w, lnx_b, ww, wa, wg, *, d_rwkv, lp, dw, da, side=()):
    b, s, _ = proj.shape
    c = RWKV_CHUNK
    hw1 = RWKV_HEADS_PER_STEP * HEAD_DIM
    groups = RWKV_GROUPS_PER_STEP if d_rwkv % (RWKV_GROUPS_PER_STEP * hw1) == 0 else 1
    hw = groups * hw1
    assert c == HEAD_DIM and d_rwkv % hw == 0 and s % c == 0 and (3 * d_rwkv) % lp == 0
    nhg = d_rwkv // hw
    lblk = (3 * d_rwkv) // lp

    def act_spec(off):
        return pl.BlockSpec((b, c, hw), lambda h, t: (0, t, off + h))

    def vec_spec(off):
        return pl.BlockSpec((1, hw), lambda h, t: (0, off + h))

    vec = vec_spec(0)
    lora_spec = pl.BlockSpec((lp, hw), lambda h, t: (0, h))
    nt = s // c
    if any(u.shape[0] > nhg * nt for u in side):
        y, _ = _rwkv(proj, mu, w0, a0, k_k, k_a, r_k, lnx_w, lnx_b, ww, wa, wg,
                     d_rwkv=d_rwkv, lp=lp, dw=dw, da=da)
        return y, [u.astype(BF16) for u in side]

    def side_spec(u):
        units = u.shape[0]
        return pl.BlockSpec((1,) + u.shape[1:],
                            lambda h, t: (jnp.minimum(h * nt + t, units - 1), 0, 0))

    in_specs = [act_spec(0), act_spec(nhg), act_spec(2 * nhg),
                pl.BlockSpec((b, c, lp), lambda h, t: (0, t, lblk)),
                vec_spec(0), vec_spec(nhg), vec_spec(2 * nhg),
                pl.BlockSpec((1, lp), lambda h, t: (0, lblk)),
                vec, vec, vec, vec, vec, vec, vec,
                lora_spec, lora_spec, lora_spec]
    assert len(in_specs) == N_RWKV_INPUTS
    side_mib = 2 * sum(6 * math.prod(u.shape[1:]) for u in side) // MIB
    kern = functools.partial(_rwkv_side_kernel, n_side=len(side), n_batch=b, dw=dw, da=da)
    outs = pl.pallas_call(
        kern,
        grid=(nhg, nt),
        in_specs=in_specs + [side_spec(u) for u in side],
        out_specs=[pl.BlockSpec((b, c, hw), lambda h, t: (0, t, h))] + [side_spec(u) for u in side],
        out_shape=[jax.ShapeDtypeStruct((b, s, d_rwkv), BF16)]
                  + [jax.ShapeDtypeStruct(u.shape, BF16) for u in side],
        scratch_shapes=[pltpu.VMEM((b, groups, hw1, hw1), F32),
                        pltpu.VMEM((b, 1, hw), F32), pltpu.VMEM((b, 1, hw), F32),
                        pltpu.VMEM((b, 1, hw), F32), pltpu.VMEM((b, 1, lp), F32)],
        compiler_params=_params(("arbitrary", "arbitrary"), vmem_mib=min(32 + side_mib, 58)),
    )(proj, proj, proj, proj, mu, mu, mu, mu,
      w0, a0, k_k, k_a, r_k, lnx_w, lnx_b, ww, wa, wg, *side)
    return outs[0], outs[1:]


def _attn_kernel(sink_ref, slope_ref, q_ref, kp_ref, kc_ref, vp_ref, vc_ref, og_ref, *rest, gqa, qb):
    o_ref = rest[len(rest) // 2]
    if len(rest) == 3:
        rest[2][...] = rest[0][...].astype(rest[2].dtype)
    step = pl.program_id(2)
    pair = pl.program_id(1)
    wd = WINDOW
    qi = lax.broadcasted_iota(jnp.int32, (wd, 2 * wd), 0)
    kj = lax.broadcasted_iota(jnp.int32, (wd, 2 * wd), 1)
    dist = qi + wd - kj
    band = (dist >= 0) & (dist < wd)
    valid = [band & ((kj >= wd) | (step > 0))] + [band] * (qb - 1)
    dist_f = dist.astype(F32)
    scale = HEAD_DIM ** -0.5
    k_all = jnp.concatenate([kp_ref[...], kc_ref[...]], axis=0)
    v_all = jnp.concatenate([vp_ref[...], vc_ref[...]], axis=0)
    heads = range(2 * gqa)
    units = [(sb, hl) for sb in range(qb) for hl in heads]
    nu = range(len(units))

    def kv(x, sb, hl):
        kvh = hl // gqa
        return x[sb * wd:(sb + 2) * wd, kvh * HEAD_DIM:(kvh + 1) * HEAD_DIM]

    slopes = [slope_ref[pair * (2 * gqa) + hl] for _, hl in units]
    sinks = [sink_ref[pair * (2 * gqa) + hl] for _, hl in units]
    s = [lax.dot_general(q_ref[sb * wd:(sb + 1) * wd, hl * HEAD_DIM:(hl + 1) * HEAD_DIM],
                         kv(k_all, sb, hl), (((1,), (1,)), ((), ())), preferred_element_type=F32)
         for sb, hl in units]
    s = [jnp.where(valid[units[u][0]], s[u] * scale - slopes[u] * dist_f, -jnp.inf) for u in nu]
    m = [jnp.maximum(jnp.max(s[u], axis=-1, keepdims=True), sinks[u]) for u in nu]
    p = [jnp.exp(s[u] - m[u]) for u in nu]
    denom = [jnp.sum(p[u], axis=-1, keepdims=True) + jnp.exp(sinks[u] - m[u]) for u in nu]
    o = [jnp.dot(p[u].astype(BF16), kv(v_all, *units[u]), preferred_element_type=F32) / denom[u]
         for u in nu]
    o = [x * lax.rsqrt(jnp.mean(x * x, axis=-1, keepdims=True) + NORM_EPS) for x in o]
    nh = len(heads)
    o_all = jnp.concatenate([jnp.concatenate(o[sb * nh:(sb + 1) * nh], axis=-1) for sb in range(qb)],
                            axis=0) * og_ref[...]
    o_ref[...] = o_all.astype(o_ref.dtype)


def _attention(qkv, sinks, out_g, *, batch, seq, d_attn, side=None):
    d_kv = N_KV_HEADS * HEAD_DIM
    n_q_heads = d_attn // HEAD_DIM
    gqa = n_q_heads // N_KV_HEADS
    qw = 2 * gqa * HEAD_DIM
    n_pairs = N_KV_HEADS // 2
    nb = seq // WINDOW
    assert qw % LANES == 0 and d_attn % LANES == 0
    k_off = d_attn // LANES
    v_off = (d_attn + d_kv) // LANES

    qb = 2 if nb % 2 == 0 else 1
    ns = nb // qb

    def cur(off):
        return pl.BlockSpec((qb * WINDOW, LANES), lambda b, p, i, *_: (b * ns + i, off + p))

    def prev(off):
        return pl.BlockSpec((WINDOW, LANES),
                            lambda b, p, i, *_: (b * nb + jnp.maximum(qb * i - 1, 0), off + p))

    slopes = jnp.exp2(-8.0 * jnp.arange(1, n_q_heads + 1, dtype=F32) / n_q_heads)
    if side is not None and side.shape[0] > batch * n_pairs * ns:
        return _attention(qkv, sinks, out_g, batch=batch, seq=seq, d_attn=d_attn), side.astype(BF16)
    sides = [] if side is None else [side]

    def side_spec(u):
        units = u.shape[0]
        return pl.BlockSpec((1,) + u.shape[1:], lambda b, p, i, *_: (
            jnp.minimum((b * n_pairs + p) * ns + i, units - 1), 0, 0))

    grid_spec = pltpu.PrefetchScalarGridSpec(
        num_scalar_prefetch=2,
        grid=(batch, n_pairs, ns),
        in_specs=[pl.BlockSpec((qb * WINDOW, qw), lambda b, p, i, *_: (b * ns + i, p)),
                  prev(k_off), cur(k_off), prev(v_off), cur(v_off),
                  pl.BlockSpec((1, qw), lambda b, p, i, *_: (0, p))] + [side_spec(u) for u in sides],
        out_specs=[pl.BlockSpec((qb * WINDOW, qw), lambda b, p, i, *_: (b * ns + i, p))]
                  + [side_spec(u) for u in sides],
    )
    outs = pl.pallas_call(
        functools.partial(_attn_kernel, gqa=gqa, qb=qb),
        grid_spec=grid_spec,
        out_shape=[jax.ShapeDtypeStruct((batch * seq, d_attn), BF16)]
                  + [jax.ShapeDtypeStruct(u.shape, BF16) for u in sides],
        compiler_params=_params(("arbitrary", "arbitrary", "arbitrary")),
    )(sinks, slopes, qkv, qkv, qkv, qkv, qkv, out_g.reshape(1, d_attn), *sides)
    return outs[0] if side is None else (outs[0], outs[1])


def _outproj_kernel(yr_ref, ya_ref, w1_ref, w2_ref, x_ref, g_ref, o_ref):
    acc = jnp.dot(yr_ref[...], w1_ref[...], preferred_element_type=F32)
    acc = acc + jnp.dot(ya_ref[...], w2_ref[...], preferred_element_type=F32)
    o_ref[...] = x_ref[...] + g_ref[0] * acc


def _outproj(y_rwkv, y_attn, w_out, x2d, gate, seq):
    t, d = x2d.shape
    k1 = y_rwkv.shape[1]
    k2 = y_attn.shape[1]
    assert k1 == k2
    tm = _pick_tile(seq, (1024, 512, 256, 128))
    tn = _pick_tile(d, (512, 256, 128))
    per_b = seq // tm
    return pl.pallas_call(
        _outproj_kernel,
        grid=(t // tm, d // tn),
        in_specs=[pl.BlockSpec((tm, k1), lambda i, j: (i, 0)),
                  pl.BlockSpec((tm, k2), lambda i, j: (i, 0)),
                  pl.BlockSpec((k1, tn), lambda i, j: (0, j)),
                  pl.BlockSpec((k2, tn), lambda i, j: (1, j)),
                  pl.BlockSpec((tm, tn), lambda i, j: (i, j)),
                  pl.BlockSpec((1, 1, tn), lambda i, j: (i // per_b, 0, j))],
        out_specs=pl.BlockSpec((tm, tn), lambda i, j: (i, j)),
        out_shape=jax.ShapeDtypeStruct((t, d), F32),
        compiler_params=_params(("arbitrary", "arbitrary")),
    )(y_rwkv, y_attn, w_out, w_out, x2d, gate[:, None, :])


U32 = jnp.uint32


def _pack_pairs(x):
    n = x.shape[1] // 2
    lo = lax.bitcast_convert_type(x[:, :n].astype(BF16).astype(F32), U32)
    hi = lax.bitcast_convert_type(x[:, n:].astype(BF16).astype(F32), U32)
    return (lo >> 16) | hi


def _unpack_pairs(w):
    lo = lax.bitcast_convert_type(w << 16, F32)
    hi = lax.bitcast_convert_type(w & jnp.uint32(0xFFFF0000), F32)
    return jnp.concatenate([lo, hi], axis=-1)


def _slab_pitch(ns):
    return ns + 8 if ns % 16 == 0 else _round_up(ns, 8)


def _to_slabs(ref, x):
    rows, d = x.shape
    ns = d // LANES
    pitch = _slab_pitch(ns)
    ref[...] = jnp.zeros_like(ref)
    for s in range(ns):
        ref[pl.ds(s, rows, stride=pitch), :] = x[:, s * LANES:(s + 1) * LANES]


def _from_slabs(ref, ns):
    pitch = _slab_pitch(ns)
    rows = ref.shape[0] // pitch
    return jnp.concatenate([ref[pl.ds(s, rows, stride=pitch), :] for s in range(ns)], axis=-1)


def _norm_route_kernel(x_ref, g_ref, sc_ref, sh_ref, w_ref, b_ref, u_ref, o_ref):
    x = x_ref[...]
    y = x * lax.rsqrt(jnp.mean(x * x, axis=-1, keepdims=True) + NORM_EPS)
    y = y * g_ref[...]
    y = y * (1.0 + sc_ref[0]) + sh_ref[0]
    _to_slabs(u_ref, _pack_pairs(y))
    logits = jnp.dot(y.astype(BF16), w_ref[...], preferred_element_type=F32) + b_ref[...]
    col = lax.broadcasted_iota(jnp.int32, logits.shape, 1).astype(F32)
    ng = float(N_GROUPS)
    epg = float(EXPERTS_PER_GROUP)
    big = 1e9
    is_g = col < ng
    lg = jnp.where(is_g, logits, -jnp.inf)
    mg = jnp.max(lg, axis=-1, keepdims=True)
    gidx = jnp.min(jnp.where(lg == mg, col, big), axis=-1, keepdims=True)
    zg = jnp.sum(jnp.where(is_g, jnp.exp(lg - mg), 0.0), axis=-1, keepdims=True)
    g_w = 1.0 / zg
    lo = ng + gidx * epg
    in_grp = (col >= lo) & (col < lo + epg)
    le = jnp.where(in_grp, logits, -jnp.inf)
    m1 = jnp.max(le, axis=-1, keepdims=True)
    i1 = jnp.min(jnp.where(le == m1, col, big), axis=-1, keepdims=True)
    le2 = jnp.where(col == i1, -jnp.inf, le)
    m2 = jnp.max(le2, axis=-1, keepdims=True)
    i2 = jnp.min(jnp.where(le2 == m2, col, big), axis=-1, keepdims=True)
    e2 = jnp.exp(m2 - m1)
    w1 = 1.0 / (1.0 + e2)
    w2 = e2 / (1.0 + e2)
    out = jnp.where(col == 0.0, i1 - ng,
                    jnp.where(col == 1.0, i2 - ng,
                              jnp.where(col == 2.0, g_w * w1,
                                        jnp.where(col == 3.0, g_w * w2, 0.0))))
    o_ref[...] = out


def _norm_route(x2d, g, scale, shift, w_r, b_r, seq):
    t, d = x2d.shape
    tm = _pick_tile(seq, (256, 128))
    per_b = seq // tm
    ns = d // (2 * LANES)
    return pl.pallas_call(
        _norm_route_kernel,
        grid=(t // tm,),
        in_specs=[pl.BlockSpec((tm, d), lambda i: (i, 0)),
                  pl.BlockSpec((1, d), lambda i: (0, 0)),
                  pl.BlockSpec((1, 1, d), lambda i: (i // per_b, 0, 0)),
                  pl.BlockSpec((1, 1, d), lambda i: (i // per_b, 0, 0)),
                  pl.BlockSpec((d, LANES), lambda i: (0, 0)),
                  pl.BlockSpec((1, LANES), lambda i: (0, 0))],
        out_specs=[pl.BlockSpec((tm * _slab_pitch(ns), LANES), lambda i: (i, 0)),
                   pl.BlockSpec((tm, LANES), lambda i: (i, 0))],
        out_shape=[jax.ShapeDtypeStruct((t * _slab_pitch(ns), LANES), U32),
                   jax.ShapeDtypeStruct((t, LANES), F32)],
        compiler_params=_params(("arbitrary",)),
    )(x2d, g.reshape(1, d), scale[:, None, :], shift[:, None, :], w_r, b_r)


GATHER_GROUP = 8


def _gather_rows(idx_ref, n_rows, src_hbm, dst, sem, col0=0, unroll=False, slab_rows=0):
    pitch = _slab_pitch(slab_rows) if slab_rows else 1
    size = slab_rows if slab_rows else 1

    def issue(r):
        tok = idx_ref[0, 0, col0 + r]
        if slab_rows:
            src = src_hbm.at[pl.ds(pl.multiple_of(tok * pitch, 8), size)]
            dst_r = dst.at[pl.ds(pl.multiple_of(r * pitch, 8), size)]
        else:
            src, dst_r = src_hbm.at[pl.ds(tok, 1)], dst.at[pl.ds(r, 1)]
        pltpu.make_async_copy(src, dst_r, sem).start()

    if unroll is True:
        for r in range(n_rows):
            issue(r)
    elif isinstance(n_rows, int):
        def body(r, carry):
            issue(r)
            return carry
        lax.fori_loop(0, n_rows, body, 0, unroll=max(int(unroll), 1))
    else:
        def group(g, carry):
            for k in range(GATHER_GROUP):
                issue(g * GATHER_GROUP + k)
            return carry
        lax.fori_loop(0, n_rows // GATHER_GROUP, group, 0)


def _wait_rows(n_rows, src_hbm, dst, sem):
    if not isinstance(n_rows, int):
        n_rows = pl.multiple_of(n_rows, 8)
    pltpu.make_async_copy(src_hbm.at[pl.ds(0, n_rows)], dst.at[pl.ds(0, n_rows)], sem).wait()


def _expert_kernel(be_ref, nu_ref, nv_ref, tok_ref, tokn_ref, u_hbm, wg_ref, wu_ref, wd_ref,
                   o_ref, xbuf_a, xbuf_b, sem):
    i = pl.program_id(0)
    last = pl.num_programs(0) - 1
    n_used = nu_ref[0]
    rows, n_words = o_ref.shape
    ns = n_words // LANES
    n_next = jnp.where(i < last, nv_ref[jnp.minimum(i + 1, last)], 0)

    @pl.when(i == 0)
    def _():
        xbuf_a[...] = jnp.zeros_like(xbuf_a)
        xbuf_b[...] = jnp.zeros_like(xbuf_b)
        _gather_rows(tok_ref, nv_ref[0], u_hbm, xbuf_a, sem.at[0], slab_rows=ns)

    def step(cur, cur_sem, nxt, nxt_sem):
        @pl.when(i < n_used)
        def _():
            _wait_rows(nv_ref[i] * ns, u_hbm, cur, cur_sem)
            _gather_rows(tokn_ref, n_next, u_hbm, nxt, nxt_sem, slab_rows=ns)
            xb = _unpack_pairs(_from_slabs(cur, ns)).astype(BF16)
            gate = jnp.dot(xb, wg_ref[0], preferred_element_type=F32)
            up = jnp.dot(xb, wu_ref[0], preferred_element_type=F32)
            h = (gate * jax.nn.sigmoid(gate) * up).astype(BF16)
            o_ref[...] = _pack_pairs(jnp.dot(h, wd_ref[0], preferred_element_type=F32))

        @pl.when(i >= n_used)
        def _():
            o_ref[...] = jnp.zeros_like(o_ref)

    @pl.when(i % 2 == 0)
    def _():
        step(xbuf_a, sem.at[0], xbuf_b, sem.at[1])

    @pl.when(i % 2 == 1)
    def _():
        step(xbuf_b, sem.at[1], xbuf_a, sem.at[0])


def _experts(block_expert, n_used, n_valid, row_tok, u2, wg, wu, wd):
    d = wg.shape[1]
    n_words = d // 2
    pitch = _slab_pitch(n_words // LANES)
    nb = block_expert.shape[0]
    rows = MOE_ROWS
    de = wg.shape[-1]
    tok3 = row_tok.reshape(nb, 1, rows)
    grid_spec = pltpu.PrefetchScalarGridSpec(
        num_scalar_prefetch=3,
        grid=(nb,),
        in_specs=[pl.BlockSpec((1, 1, rows), lambda i, be, nu, nv: (i, 0, 0), memory_space=pltpu.SMEM),
                  pl.BlockSpec((1, 1, rows), lambda i, be, nu, nv: (jnp.minimum(i + 1, nb - 1), 0, 0),
                               memory_space=pltpu.SMEM),
                  pl.BlockSpec(memory_space=pl.ANY),
                  pl.BlockSpec((1, d, de), lambda i, be, nu, nv: (be[i], 0, 0)),
                  pl.BlockSpec((1, d, de), lambda i, be, nu, nv: (be[i], 0, 0)),
                  pl.BlockSpec((1, de, d), lambda i, be, nu, nv: (be[i], 0, 0))],
        out_specs=pl.BlockSpec((rows, n_words), lambda i, be, nu, nv: (i, 0)),
        scratch_shapes=[pltpu.VMEM((rows * pitch, LANES), U32),
                        pltpu.VMEM((rows * pitch, LANES), U32),
                        pltpu.SemaphoreType.DMA((2,))],
    )
    return pl.pallas_call(
        _expert_kernel,
        grid_spec=grid_spec,
        out_shape=jax.ShapeDtypeStruct((nb * rows, n_words), U32),
        compiler_params=_params(("arbitrary",), vmem_mib=58),
    )(block_expert, n_used, n_valid, tok3, tok3, u2, wg, wu, wd)


def _combine_kernel(pos_ref, posn_ref, ys_hbm, r_ref, x_ref, g_ref, ng_ref, o_ref, buf_a, buf_b, sem):
    i = pl.program_id(0)
    last = pl.num_programs(0) - 1
    tc = x_ref.shape[0]

    def gather(idx_ref, buf, s, unroll):
        _gather_rows(idx_ref, tc, ys_hbm, buf.at[0], s, 0, unroll)
        _gather_rows(idx_ref, tc, ys_hbm, buf.at[1], s, tc, unroll)

    def wait(buf, s):
        _wait_rows(tc, ys_hbm, buf.at[0], s)
        _wait_rows(tc, ys_hbm, buf.at[1], s)

    @pl.when(i == 0)
    def _():
        gather(pos_ref, buf_a, sem.at[0], False)

    def step(cur, cur_sem, nxt, nxt_sem):
        wait(cur, cur_sem)
        gather(posn_ref, nxt, nxt_sem, True)
        moe = _unpack_pairs(cur[0]) * r_ref[:, 2:3] + _unpack_pairs(cur[1]) * r_ref[:, 3:4]
        x = x_ref[...] + g_ref[0] * moe
        y = x * lax.rsqrt(jnp.mean(x * x, axis=-1, keepdims=True) + NORM_EPS)
        o_ref[...] = y * ng_ref[...]

        @pl.when(i == last)
        def _():
            wait(nxt, nxt_sem)

    @pl.when(i % 2 == 0)
    def _():
        step(buf_a, sem.at[0], buf_b, sem.at[1])

    @pl.when(i % 2 == 1)
    def _():
        step(buf_b, sem.at[1], buf_a, sem.at[0])


def _combine(pos, route, ys, x1, gate, norm_g, seq):
    t, d = x1.shape
    tc = _pick_tile(seq, (128,))
    n = t // tc
    per_b = seq // tc
    pos3 = pos.reshape(n, tc, 2).transpose(0, 2, 1).reshape(n, 1, 2 * tc)
    return pl.pallas_call(
        _combine_kernel,
        grid=(n,),
        in_specs=[pl.BlockSpec((1, 1, 2 * tc), lambda i: (i, 0, 0), memory_space=pltpu.SMEM),
                  pl.BlockSpec((1, 1, 2 * tc), lambda i: (jnp.minimum(i + 1, n - 1), 0, 0),
                               memory_space=pltpu.SMEM),
                  pl.BlockSpec(memory_space=pl.ANY),
                  pl.BlockSpec((tc, LANES), lambda i: (i, 0)),
                  pl.BlockSpec((tc, d), lambda i: (i, 0)),
                  pl.BlockSpec((1, 1, d), lambda i: (i // per_b, 0, 0)),
                  pl.BlockSpec((1, d), lambda i: (0, 0))],
        out_specs=pl.BlockSpec((tc, d), lambda i: (i, 0)),
        out_shape=jax.ShapeDtypeStruct((t, d), F32),
        scratch_shapes=[pltpu.VMEM((2, tc, d // 2), U32), pltpu.VMEM((2, tc, d // 2), U32),
                        pltpu.SemaphoreType.DMA((2,))],
        compiler_params=_params(("arbitrary",)),
    )(pos3, pos3, ys, route, x1, gate[:, None, :], norm_g.reshape(1, d))


def _dispatch(expert_id, n_tok):
    m = expert_id.shape[0]
    rows = MOE_ROWS
    nb = -(-m // rows) + N_EXPERTS
    onehot = (expert_id[:, None] == jnp.arange(N_EXPERTS, dtype=jnp.int32)[None, :]).astype(jnp.int32)
    csum = jnp.cumsum(onehot, axis=0)
    rank = jnp.take_along_axis(csum, expert_id[:, None], axis=1)[:, 0] - 1
    counts = csum[-1]
    padded = ((counts + rows - 1) // rows) * rows
    pends = jnp.cumsum(padded)
    pstarts = pends - padded
    dest = (pstarts[expert_id] + rank).astype(jnp.int32)
    tok = jnp.arange(m, dtype=jnp.int32) // (m // n_tok)
    row_tok = jnp.zeros((nb * rows,), jnp.int32).at[dest].set(tok)
    n_used = (pends[-1] // rows).astype(jnp.int32)
    all_blk = jnp.arange(nb, dtype=jnp.int32)
    blk = jnp.minimum(all_blk, n_used - 1)
    block_expert = jnp.minimum(jnp.searchsorted(pends, blk * rows, side='right'),
                               N_EXPERTS - 1).astype(jnp.int32)
    seg_end = (pstarts + counts)[block_expert]
    n_valid = jnp.where(all_blk < n_used, jnp.clip(seg_end - all_blk * rows, 0, rows), 0)
    n_valid = ((n_valid + GATHER_GROUP - 1) // GATHER_GROUP) * GATHER_GROUP
    return block_expert, n_used.reshape(1), n_valid.astype(jnp.int32), row_tok, dest


def _layer(x, c, w_cond, b_cond, norm1_g, w_in, rwkv_mu, rwkv_w0, rwkv_w_up, rwkv_a0, rwkv_a_up,
           rwkv_g_up, rwkv_k_k, rwkv_k_a, rwkv_r_k, rwkv_lnx_w, rwkv_lnx_b, attn_sinks, attn_out_g,
           w_out, norm2_g, router_group, router_group_bias, router_expert, router_expert_bias,
           expert_w_gate, expert_w_up, expert_w_down):
    b, s, d = x.shape
    t = b * s
    d_rwkv = rwkv_w0.shape[-1]
    d_attn = attn_out_g.shape[-1]
    dw, da, dg = rwkv_w_up.shape[0], rwkv_a_up.shape[0], rwkv_g_up.shape[0]
    n_lora = dw + da + dg
    lp = _round_up(n_lora, LANES)
    while (3 * d_rwkv) % lp:
        lp += LANES
    n_rwkv_cols = 3 * d_rwkv + n_lora

    c_pad = jnp.zeros((_round_up(b, 8), d), F32).at[:b].set(c)
    mod = _cond(c_pad, w_cond, b_cond.reshape(1, -1))[:b]
    sh1, sc1, g1, sh2, sc2, g2 = jnp.split(mod, 6, axis=-1)

    x2d = x.reshape(t, d)
    u1 = _norm_mod(x2d, norm1_g, sc1, sh1, s, BF16)

    zpad = jnp.zeros((d, lp - n_lora), F32)
    w_r = jnp.concatenate([w_in[:, :n_rwkv_cols], zpad], axis=1).astype(BF16)
    w_a = w_in[:, n_rwkv_cols:].astype(BF16)
    proj_r = _matmul(u1, w_r, BF16)
    proj_a = _matmul(u1, w_a, BF16)

    def lora_rows(wmat, off):
        return jnp.zeros((lp, d_rwkv), F32).at[off:off + wmat.shape[0]].set(wmat).astype(BF16)

    mu = jnp.concatenate([rwkv_mu, jnp.zeros((lp - n_lora,), F32)]).reshape(1, -1)
    row = lambda v: v.reshape(1, d_rwkv)
    n_exp = expert_w_gate.shape[0]
    halves = lambda w: w.reshape(2 * n_exp, w.shape[1] // 2, w.shape[2])
    w_out_bands = w_out.reshape(2 * n_exp, w_out.shape[0] // (2 * n_exp), w_out.shape[1])
    y_rwkv, (wg_b, wu_b, w_out_b) = _rwkv(
        proj_r.reshape(b, s, -1), mu, row(rwkv_w0), row(rwkv_a0), row(rwkv_k_k),
        row(rwkv_k_a), row(rwkv_r_k), row(rwkv_lnx_w), row(rwkv_lnx_b),
        lora_rows(rwkv_w_up, 0), lora_rows(rwkv_a_up, dw), lora_rows(rwkv_g_up, dw + da),
        d_rwkv=d_rwkv, lp=lp, dw=dw, da=da,
        side=(halves(expert_w_gate), halves(expert_w_up), w_out_bands))
    y_rwkv = y_rwkv.reshape(t, d_rwkv)
    wg_b = wg_b.reshape(expert_w_gate.shape)
    wu_b = wu_b.reshape(expert_w_up.shape)
    w_out_b = w_out_b.reshape(w_out.shape)

    quarters = lambda w: w.reshape(4 * n_exp, w.shape[1] // 4, w.shape[2])
    y_attn, wd_b = _attention(proj_a, attn_sinks, attn_out_g, batch=b, seq=s, d_attn=d_attn,
                              side=quarters(expert_w_down))
    wd_b = wd_b.reshape(expert_w_down.shape)

    x1 = _outproj(y_rwkv, y_attn, w_out_b, x2d, g1, s)

    n_r = N_GROUPS + N_EXPERTS
    w_router = jnp.concatenate([router_group, router_expert, jnp.zeros((d, LANES - n_r), F32)],
                               axis=1).astype(BF16)
    b_router = jnp.concatenate([router_group_bias, router_expert_bias,
                                jnp.zeros((LANES - n_r,), F32)]).reshape(1, LANES)
    u2, route = _norm_route(x1, norm2_g, sc2, sh2, w_router, b_router, s)
    expert_id = route[:, :2].astype(jnp.int32).reshape(-1)
    block_expert, n_used, n_valid, row_tok, dest = _dispatch(expert_id, t)
    ys = _experts(block_expert, n_used, n_valid, row_tok, u2, wg_b, wu_b, wd_b)
    return ys, dest.reshape(t, 2), route, x1, g2


def kernel(x, c, w_cond, b_cond, norm1_g, w_in, rwkv_mu, rwkv_w0, rwkv_w_up, rwkv_a0, rwkv_a_up, rwkv_g_up, rwkv_k_k, rwkv_k_a, rwkv_r_k, rwkv_lnx_w, rwkv_lnx_b, attn_sinks, attn_out_g, w_out, norm2_g, router_group, router_group_bias, router_expert, router_expert_bias, expert_w_gate, expert_w_up, expert_w_down, norm_f_g):
    b, s, d = x.shape
    depth = w_cond.shape[0]
    assert depth == 1, "the fused final norm assumes a single layer"
    l = 0
    ys, pos, route, x1, g2 = _layer(
        x, c, w_cond[l], b_cond[l], norm1_g[l], w_in[l], rwkv_mu[l], rwkv_w0[l], rwkv_w_up[l],
        rwkv_a0[l], rwkv_a_up[l], rwkv_g_up[l], rwkv_k_k[l], rwkv_k_a[l], rwkv_r_k[l],
        rwkv_lnx_w[l], rwkv_lnx_b[l], attn_sinks[l], attn_out_g[l], w_out[l], norm2_g[l],
        router_group[l], router_group_bias[l], router_expert[l], router_expert_bias[l],
        expert_w_gate[l], expert_w_up[l], expert_w_down[l])
    out = _combine(pos, route, ys, x1, g2, norm_f_g, s)
    return out.reshape(b, s, d)
```

```python
import functools
import math

import jax
import jax.numpy as jnp
from jax import lax
from jax.experimental import pallas as pl
from jax.experimental.pallas import tpu as pltpu

F32 = jnp.float32
BF16 = jnp.bfloat16

HEAD_DIM = 64
N_KV_HEADS = 8
WINDOW = 128
N_GROUPS = 8
EXPERTS_PER_GROUP = 8
N_EXPERTS = N_GROUPS * EXPERTS_PER_GROUP
RWKV_GN_EPS = 64e-5
NORM_EPS = 1e-6
RWKV_CHUNK = 64
RWKV_HEADS_PER_STEP = 4
RWKV_GROUPS_PER_STEP = 4
MOE_ROWS = 256
LANES = 128
MIB = 1024 * 1024


def _params(sem, vmem_mib=48):
    return pltpu.CompilerParams(dimension_semantics=sem, vmem_limit_bytes=vmem_mib * MIB)


def _round_up(n, m):
    return -(-n // m) * m


def _pick_tile(n, candidates):
    for t in candidates:
        if n % t == 0:
            return t
    raise ValueError(f"no tile in {candidates} divides {n}")


def _cond_kernel(c_ref, w_ref, b_ref, o_ref):
    c = c_ref[...]
    sc = (c * jax.nn.sigmoid(c)).astype(BF16)
    o_ref[...] = jnp.dot(sc, w_ref[...].astype(BF16), preferred_element_type=F32) + b_ref[...]


def _cond(c_pad, w, b):
    rows, d = c_pad.shape
    n = w.shape[1]
    tn = _pick_tile(n, (1024, 512, 256, 128))
    return pl.pallas_call(
        _cond_kernel,
        grid=(n // tn,),
        in_specs=[pl.BlockSpec((rows, d), lambda j: (0, 0)),
                  pl.BlockSpec((d, tn), lambda j: (0, j)),
                  pl.BlockSpec((1, tn), lambda j: (0, j))],
        out_specs=pl.BlockSpec((rows, tn), lambda j: (0, j)),
        out_shape=jax.ShapeDtypeStruct((rows, n), F32),
        compiler_params=_params(("arbitrary",)),
    )(c_pad, w, b)


def _norm_mod_kernel(x_ref, g_ref, sc_ref, sh_ref, o_ref):
    x = x_ref[...]
    y = x * lax.rsqrt(jnp.mean(x * x, axis=-1, keepdims=True) + NORM_EPS)
    y = y * g_ref[...]
    o_ref[...] = (y * (1.0 + sc_ref[0]) + sh_ref[0]).astype(o_ref.dtype)


def _norm_mod(x2d, g, scale, shift, seq, out_dtype):
    t, d = x2d.shape
    tm = _pick_tile(seq, (512, 256, 128))
    per_b = seq // tm
    return pl.pallas_call(
        _norm_mod_kernel,
        grid=(t // tm,),
        in_specs=[pl.BlockSpec((tm, d), lambda i: (i, 0)),
                  pl.BlockSpec((1, d), lambda i: (0, 0)),
                  pl.BlockSpec((1, 1, d), lambda i: (i // per_b, 0, 0)),
                  pl.BlockSpec((1, 1, d), lambda i: (i // per_b, 0, 0))],
        out_specs=pl.BlockSpec((tm, d), lambda i: (i, 0)),
        out_shape=jax.ShapeDtypeStruct((t, d), out_dtype),
        compiler_params=_params(("arbitrary",)),
    )(x2d, g.reshape(1, d), scale[:, None, :], shift[:, None, :])


def _mm_kernel(a_ref, b_ref, o_ref):
    o_ref[...] = jnp.dot(a_ref[...], b_ref[...], preferred_element_type=F32).astype(o_ref.dtype)


def _matmul(a, b, out_dtype):
    m, k = a.shape
    n = b.shape[1]
    tm = _pick_tile(m, (1024, 512, 256, 128))
    tn = _pick_tile(n, (512, 256, 128))
    return pl.pallas_call(
        _mm_kernel,
        grid=(m // tm, n // tn),
        in_specs=[pl.BlockSpec((tm, k), lambda i, j: (i, 0)),
                  pl.BlockSpec((k, tn), lambda i, j: (0, j))],
        out_specs=pl.BlockSpec((tm, tn), lambda i, j: (i, j)),
        out_shape=jax.ShapeDtypeStruct((m, n), out_dtype),
        compiler_params=_params(("arbitrary", "arbitrary")),
    )(a, b)


def _softplus(z):
    return jnp.maximum(z, 0.0) + jnp.log(1.0 + jnp.exp(-jnp.abs(z)))


def _rwkv_kernel(r_ref, k_ref, v_ref, l_ref, mur_ref, muk_ref, muv_ref, mul_ref,
                 w0_ref, a0_ref, kk_ref, ka_ref, rk_ref, lw_ref, lb_ref,
                 ww_ref, wa_ref, wg_ref, o_ref,
                 s_ref, pr_ref, pk_ref, pv_ref, pl_ref, *, n_batch, dw, da):
    c = RWKV_CHUNK
    nh = RWKV_HEADS_PER_STEP
    hw = nh * HEAD_DIM
    groups = r_ref.shape[-1] // hw
    lp = l_ref.shape[-1]
    chunk = pl.program_id(1)

    @pl.when(chunk == 0)
    def _():
        s_ref[...] = jnp.zeros_like(s_ref)
        pr_ref[...] = jnp.zeros_like(pr_ref)
        pk_ref[...] = jnp.zeros_like(pk_ref)
        pv_ref[...] = jnp.zeros_like(pv_ref)
        pl_ref[...] = jnp.zeros_like(pl_ref)

    row_w = lax.broadcasted_iota(jnp.int32, (c, hw), 0)
    col_w = lax.broadcasted_iota(jnp.int32, (c, hw), 1)
    s_idx = col_w % c
    strict = row_w > s_idx
    incl = row_w >= s_idx
    eye_w = (row_w == s_idx).astype(F32)
    merge_masks = [strict & (row_w // 2 == s_idx // 2)]
    size = 2
    while size < c:
        merge_masks.append(strict & (row_w // (2 * size) == s_idx // (2 * size))
                           & (row_w // size != s_idx // size))
        size *= 2
    bd_row = lax.broadcasted_iota(jnp.int32, (hw, hw), 0) // HEAD_DIM
    bd_col = lax.broadcasted_iota(jnp.int32, (hw, hw), 1) // HEAD_DIM
    bd_mask = bd_row == bd_col
    bd_ones = bd_mask.astype(BF16)
    tri = (lax.broadcasted_iota(jnp.int32, (c, c), 0)
           >= lax.broadcasted_iota(jnp.int32, (c, c), 1)).astype(BF16)
    row_l = lax.broadcasted_iota(jnp.int32, (c, lp), 0)
    col_l = lax.broadcasted_iota(jnp.int32, (c, lp), 1)

    def shift_mix(x, prev, mu, row):
        shifted = jnp.where(row == 0, prev, pltpu.roll(x, 1, axis=0))
        return x + (shifted - x) * mu

    def split2(x):
        hi = x.astype(BF16)
        lo = (x - hi.astype(F32)).astype(BF16)
        return hi, lo

    chains = [(bi, gi) for gi in range(groups) for bi in range(n_batch)]
    nb = range(len(chains))

    def cols(x, gi):
        return x[..., gi * hw:(gi + 1) * hw]

    def vec(ref, ci):
        return cols(ref[...], chains[ci][1])

    def rows_split(x):
        return [x[i * c:(i + 1) * c] for i in range(x.shape[0] // c)]

    def chain_split(x):
        return [cols(x[bi * c:(bi + 1) * c], gi) for bi, gi in chains]

    def seg_sum(xs):
        stacked = jnp.concatenate([x.astype(BF16) for x in xs], axis=0)
        return rows_split(jnp.dot(stacked, bd_ones, preferred_element_type=F32))

    def bd(x):
        return jnp.where(bd_mask, jnp.tile(x, (nh, 1)), 0.0).astype(BF16)

    def mm(a, b):
        return jnp.dot(a.astype(BF16), b, preferred_element_type=F32)

    def mm_nt(a, b):
        return lax.dot_general(a.astype(BF16), b, (((1,), (1,)), ((), ())),
                               preferred_element_type=F32)

    def mixed(x_ref, p_ref, mu_ref):
        return [shift_mix(cols(x_ref[bi], gi).astype(F32), cols(p_ref[bi], gi),
                          cols(mu_ref[...], gi), row_w) for bi, gi in chains]

    xr = mixed(r_ref, pr_ref, mur_ref)
    xk = mixed(k_ref, pk_ref, muk_ref)
    xv = mixed(v_ref, pv_ref, muv_ref)
    xl = [shift_mix(l_ref[bi].astype(F32), pl_ref[bi], mul_ref[...], row_l) for bi in range(n_batch)]
    states = [s_ref[bi, gi] for bi, gi in chains]

    act = jnp.concatenate(
        [jnp.where(col_l < dw, jnp.tanh(x), jnp.where(col_l < dw + da, x, jax.nn.sigmoid(x)))
         for x in xl], axis=0).astype(BF16)
    kw = _round_up(dw, LANES)
    ka = _round_up(dw + da, LANES)
    lora_w = chain_split(jnp.dot(act[:, :kw], ww_ref[:kw, :], preferred_element_type=F32))
    lora_a = chain_split(jnp.dot(act[:, :ka], wa_ref[:ka, :], preferred_element_type=F32))
    gate = chain_split(jnp.dot(act, wg_ref[...], preferred_element_type=F32))

    e = [jnp.exp(-_softplus(-(vec(w0_ref, ci) + lora_w[ci])) - 0.5) for ci in nb]
    lr = [jax.nn.sigmoid(vec(a0_ref, ci) + lora_a[ci]) for ci in nb]
    kk = [xk[ci] * vec(kk_ref, ci) for ci in nb]
    kk_ss = seg_sum([x * x for x in kk])
    kk = [x / jnp.maximum(jnp.sqrt(ss), 1e-12) for x, ss in zip(kk, kk_ss)]
    kmod = [xk[ci] * (1.0 + (lr[ci] - 1.0) * vec(ka_ref, ci)) for ci in nb]

    cum = []
    for ei in e:
        e_hi, e_lo = split2(ei)
        cum2 = jnp.dot(tri, jnp.concatenate([e_hi, e_lo], axis=1), preferred_element_type=F32)
        cum.append(cum2[:, :hw] + cum2[:, hw:])
    dec = [jnp.exp(-x) for x in cum]
    inv = [jnp.exp(x) for x in cum]
    at = [-kk[bi] * jnp.exp(e[bi] - cum[bi]) for bi in nb]
    rt = [xr[bi] * dec[bi] for bi in nb]
    bt = [kk[bi] * lr[bi] * inv[bi] for bi in nb]
    kt = [kmod[bi] * inv[bi] for bi in nb]

    ar = [jnp.concatenate([at[bi], rt[bi]], axis=0).astype(BF16) for bi in nb]
    g = [mm_nt(ar[bi], jnp.concatenate([bd(bt[bi]), bd(kt[bi])], axis=0)) for bi in nb]
    m_ab = [jnp.where(strict, x[:c, :hw], 0.0) for x in g]
    m_ak = [jnp.where(strict, x[:c, hw:], 0.0) for x in g]
    m_rb = [jnp.where(incl, x[c:, :hw], 0.0) for x in g]
    m_rk = [jnp.where(incl, x[c:, hw:], 0.0) for x in g]

    t_inv = [eye_w + jnp.where(merge_masks[0], m, 0.0) for m in m_ab]
    for mask in merge_masks[1:]:
        z = [mm(jnp.where(mask, m_ab[bi], 0.0), bd(t_inv[bi])) for bi in nb]
        t_inv = [t_inv[bi] + mm(t_inv[bi], bd(z[bi])) for bi in nb]

    s_bf = [s.astype(BF16) for s in states]
    bd_v = [bd(x) for x in xv]
    ar_s = [mm_nt(ar[bi], s_bf[bi]) for bi in nb]
    mv = [mm(jnp.concatenate([m_ak[bi], m_rk[bi]], axis=0), bd_v[bi]) for bi in nb]
    u = [mm(t_inv[bi], bd(ar_s[bi][:c] + mv[bi][:c])) for bi in nb]
    y = [ar_s[bi][c:] + mm(m_rb[bi], bd(u[bi])) + mv[bi][c:] for bi in nb]
    upd = [lax.dot_general(jnp.concatenate([u[bi], xv[bi]], axis=0).astype(BF16),
                           jnp.concatenate([bt[bi], kt[bi]], axis=0).astype(BF16),
                           (((0,), (0,)), ((), ())), preferred_element_type=F32) for bi in nb]

    mean = [m * (1.0 / HEAD_DIM) for m in seg_sum(y)]
    dy = [y[bi] - mean[bi] for bi in nb]
    var = [v * (1.0 / HEAD_DIM) for v in seg_sum([d * d for d in dy])]
    rk_sum = seg_sum([xr[ci] * kmod[ci] * vec(rk_ref, ci) for ci in nb])

    for ci, (bi, gi) in enumerate(chains):
        yn = dy[ci] * lax.rsqrt(var[ci] + RWKV_GN_EPS) * vec(lw_ref, ci) + vec(lb_ref, ci)
        o_ref[bi, :, gi * hw:(gi + 1) * hw] = ((yn + rk_sum[ci] * xv[ci]) * gate[ci]).astype(o_ref.dtype)
        s_ref[bi, gi] = (states[ci] + jnp.where(bd_mask, upd[ci], 0.0)) * dec[ci][c - 1:c]
    def last_row(x_ref, bi):
        return x_ref[bi, c - 16:c, :].astype(F32)[15:16]

    for bi in range(n_batch):
        pr_ref[bi] = last_row(r_ref, bi)
        pk_ref[bi] = last_row(k_ref, bi)
        pv_ref[bi] = last_row(v_ref, bi)
        pl_ref[bi] = last_row(l_ref, bi)


N_RWKV_INPUTS = 18
N_RWKV_SCRATCH = 5


def _rwkv_side_kernel(*refs, n_side, **kw):
    ins = refs[:N_RWKV_INPUTS]
    side_in = refs[N_RWKV_INPUTS:N_RWKV_INPUTS + n_side]
    o_ref = refs[N_RWKV_INPUTS + n_side]
    side_out = refs[N_RWKV_INPUTS + n_side + 1:N_RWKV_INPUTS + 2 * n_side + 1]
    scratch = refs[N_RWKV_INPUTS + 2 * n_side + 1:]
    assert len(scratch) == N_RWKV_SCRATCH
    _rwkv_kernel(*ins, o_ref, *scratch, **kw)
    for src, dst in zip(side_in, side_out):
        dst[...] = src[...].astype(dst.dtype)


def _rwkv(proj, mu, w0, a0, k_k, k_a, r_k, lnx_w, lnx_b, ww, wa, wg, *, d_rwkv, lp, dw, da, side=()):
    b, s, _ = proj.shape
    c = RWKV_CHUNK
    hw1 = RWKV_HEADS_PER_STEP * HEAD_DIM
    groups = RWKV_GROUPS_PER_STEP if d_rwkv % (RWKV_GROUPS_PER_STEP * hw1) == 0 else 1
    hw = groups * hw1
    assert c == HEAD_DIM and d_rwkv % hw == 0 and s % c == 0 and (3 * d_rwkv) % lp == 0
    nhg = d_rwkv // hw
    lblk = (3 * d_rwkv) // lp

    def act_spec(off):
        return pl.BlockSpec((b, c, hw), lambda h, t: (0, t, off + h))

    def vec_spec(off):
        return pl.BlockSpec((1, hw), lambda h, t: (0, off + h))

    vec = vec_spec(0)
    lora_spec = pl.BlockSpec((lp, hw), lambda h, t: (0, h))
    nt = s // c
    if any(u.shape[0] > nhg * nt for u in side):
        y, _ = _rwkv(proj, mu, w0, a0, k_k, k_a, r_k, lnx_w, lnx_b, ww, wa, wg,
                     d_rwkv=d_rwkv, lp=lp, dw=dw, da=da)
        return y, [u.astype(BF16) for u in side]

    def side_spec(u):
        units = u.shape[0]
        return pl.BlockSpec((1,) + u.shape[1:],
                            lambda h, t: (jnp.minimum(h * nt + t, units - 1), 0, 0))

    in_specs = [act_spec(0), act_spec(nhg), act_spec(2 * nhg),
                pl.BlockSpec((b, c, lp), lambda h, t: (0, t, lblk)),
                vec_spec(0), vec_spec(nhg), vec_spec(2 * nhg),
                pl.BlockSpec((1, lp), lambda h, t: (0, lblk)),
                vec, vec, vec, vec, vec, vec, vec,
                lora_spec, lora_spec, lora_spec]
    assert len(in_specs) == N_RWKV_INPUTS
    side_mib = 2 * sum(6 * math.prod(u.shape[1:]) for u in side) // MIB
    kern = functools.partial(_rwkv_side_kernel, n_side=len(side), n_batch=b, dw=dw, da=da)
    outs = pl.pallas_call(
        kern,
        grid=(nhg, nt),
        in_specs=in_specs + [side_spec(u) for u in side],
        out_specs=[pl.BlockSpec((b, c, hw), lambda h, t: (0, t, h))] + [side_spec(u) for u in side],
        out_shape=[jax.ShapeDtypeStruct((b, s, d_rwkv), BF16)]
                  + [jax.ShapeDtypeStruct(u.shape, BF16) for u in side],
        scratch_shapes=[pltpu.VMEM((b, groups, hw1, hw1), F32),
                        pltpu.VMEM((b, 1, hw), F32), pltpu.VMEM((b, 1, hw), F32),
                        pltpu.VMEM((b, 1, hw), F32), pltpu.VMEM((b, 1, lp), F32)],
        compiler_params=_params(("arbitrary", "arbitrary"), vmem_mib=min(32 + side_mib, 58)),
    )(proj, proj, proj, proj, mu, mu, mu, mu,
      w0, a0, k_k, k_a, r_k, lnx_w, lnx_b, ww, wa, wg, *side)
    return outs[0], outs[1:]


def _attn_kernel(sink_ref, slope_ref, q_ref, kp_ref, kc_ref, vp_ref, vc_ref, og_ref, *rest, gqa, qb):
    o_ref = rest[len(rest) // 2]
    if len(rest) == 3:
        rest[2][...] = rest[0][...].astype(rest[2].dtype)
    step = pl.program_id(2)
    pair = pl.program_id(1)
    wd = WINDOW
    qi = lax.broadcasted_iota(jnp.int32, (wd, 2 * wd), 0)
    kj = lax.broadcasted_iota(jnp.int32, (wd, 2 * wd), 1)
    dist = qi + wd - kj
    band = (dist >= 0) & (dist < wd)
    valid = [band & ((kj >= wd) | (step > 0))] + [band] * (qb - 1)
    dist_f = dist.astype(F32)
    scale = HEAD_DIM ** -0.5
    k_all = jnp.concatenate([kp_ref[...], kc_ref[...]], axis=0)
    v_all = jnp.concatenate([vp_ref[...], vc_ref[...]], axis=0)
    heads = range(2 * gqa)
    units = [(sb, hl) for sb in range(qb) for hl in heads]
    nu = range(len(units))

    def kv(x, sb, hl):
        kvh = hl // gqa
        return x[sb * wd:(sb + 2) * wd, kvh * HEAD_DIM:(kvh + 1) * HEAD_DIM]

    slopes = [slope_ref[pair * (2 * gqa) + hl] for _, hl in units]
    sinks = [sink_ref[pair * (2 * gqa) + hl] for _, hl in units]
    s = [lax.dot_general(q_ref[sb * wd:(sb + 1) * wd, hl * HEAD_DIM:(hl + 1) * HEAD_DIM],
                         kv(k_all, sb, hl), (((1,), (1,)), ((), ())), preferred_element_type=F32)
         for sb, hl in units]
    s = [jnp.where(valid[units[u][0]], s[u] * scale - slopes[u] * dist_f, -jnp.inf) for u in nu]
    m = [jnp.maximum(jnp.max(s[u], axis=-1, keepdims=True), sinks[u]) for u in nu]
    p = [jnp.exp(s[u] - m[u]) for u in nu]
    denom = [jnp.sum(p[u], axis=-1, keepdims=True) + jnp.exp(sinks[u] - m[u]) for u in nu]
    o = [jnp.dot(p[u].astype(BF16), kv(v_all, *units[u]), preferred_element_type=F32) / denom[u]
         for u in nu]
    o = [x * lax.rsqrt(jnp.mean(x * x, axis=-1, keepdims=True) + NORM_EPS) for x in o]
    nh = len(heads)
    o_all = jnp.concatenate([jnp.concatenate(o[sb * nh:(sb + 1) * nh], axis=-1) for sb in range(qb)],
                            axis=0) * og_ref[...]
    o_ref[...] = o_all.astype(o_ref.dtype)


def _attention(qkv, sinks, out_g, *, batch, seq, d_attn, side=None):
    d_kv = N_KV_HEADS * HEAD_DIM
    n_q_heads = d_attn // HEAD_DIM
    gqa = n_q_heads // N_KV_HEADS
    qw = 2 * gqa * HEAD_DIM
    n_pairs = N_KV_HEADS // 2
    nb = seq // WINDOW
    assert qw % LANES == 0 and d_attn % LANES == 0
    k_off = d_attn // LANES
    v_off = (d_attn + d_kv) // LANES

    qb = 2 if nb % 2 == 0 else 1
    ns = nb // qb

    def cur(off):
        return pl.BlockSpec((qb * WINDOW, LANES), lambda b, p, i, *_: (b * ns + i, off + p))

    def prev(off):
        return pl.BlockSpec((WINDOW, LANES),
                            lambda b, p, i, *_: (b * nb + jnp.maximum(qb * i - 1, 0), off + p))

    slopes = jnp.exp2(-8.0 * jnp.arange(1, n_q_heads + 1, dtype=F32) / n_q_heads)
    if side is not None and side.shape[0] > batch * n_pairs * ns:
        return _attention(qkv, sinks, out_g, batch=batch, seq=seq, d_attn=d_attn), side.astype(BF16)
    sides = [] if side is None else [side]

    def side_spec(u):
        units = u.shape[0]
        return pl.BlockSpec((1,) + u.shape[1:], lambda b, p, i, *_: (
            jnp.minimum((b * n_pairs + p) * ns + i, units - 1), 0, 0))

    grid_spec = pltpu.PrefetchScalarGridSpec(
        num_scalar_prefetch=2,
        grid=(batch, n_pairs, ns),
        in_specs=[pl.BlockSpec((qb * WINDOW, qw), lambda b, p, i, *_: (b * ns + i, p)),
                  prev(k_off), cur(k_off), prev(v_off), cur(v_off),
                  pl.BlockSpec((1, qw), lambda b, p, i, *_: (0, p))] + [side_spec(u) for u in sides],
        out_specs=[pl.BlockSpec((qb * WINDOW, qw), lambda b, p, i, *_: (b * ns + i, p))]
                  + [side_spec(u) for u in sides],
    )
    outs = pl.pallas_call(
        functools.partial(_attn_kernel, gqa=gqa, qb=qb),
        grid_spec=grid_spec,
        out_shape=[jax.ShapeDtypeStruct((batch * seq, d_attn), BF16)]
                  + [jax.ShapeDtypeStruct(u.shape, BF16) for u in sides],
        compiler_params=_params(("arbitrary", "arbitrary", "arbitrary")),
    )(sinks, slopes, qkv, qkv, qkv, qkv, qkv, out_g.reshape(1, d_attn), *sides)
    return outs[0] if side is None else (outs[0], outs[1])


def _outproj_kernel(yr_ref, ya_ref, w1_ref, w2_ref, x_ref, g_ref, o_ref):
    acc = jnp.dot(yr_ref[...], w1_ref[...], preferred_element_type=F32)
    acc = acc + jnp.dot(ya_ref[...], w2_ref[...], preferred_element_type=F32)
    o_ref[...] = x_ref[...] + g_ref[0] * acc


def _outproj(y_rwkv, y_attn, w_out, x2d, gate, seq):
    t, d = x2d.shape
    k1 = y_rwkv.shape[1]
    k2 = y_attn.shape[1]
    assert k1 == k2
    tm = _pick_tile(seq, (1024, 512, 256, 128))
    tn = _pick_tile(d, (512, 256, 128))
    per_b = seq // tm
    return pl.pallas_call(
        _outproj_kernel,
        grid=(t // tm, d // tn),
        in_specs=[pl.BlockSpec((tm, k1), lambda i, j: (i, 0)),
                  pl.BlockSpec((tm, k2), lambda i, j: (i, 0)),
                  pl.BlockSpec((k1, tn), lambda i, j: (0, j)),
                  pl.BlockSpec((k2, tn), lambda i, j: (1, j)),
                  pl.BlockSpec((tm, tn), lambda i, j: (i, j)),
                  pl.BlockSpec((1, 1, tn), lambda i, j: (i // per_b, 0, j))],
        out_specs=pl.BlockSpec((tm, tn), lambda i, j: (i, j)),
        out_shape=jax.ShapeDtypeStruct((t, d), F32),
        compiler_params=_params(("arbitrary", "arbitrary")),
    )(y_rwkv, y_attn, w_out, w_out, x2d, gate[:, None, :])


U32 = jnp.uint32


def _pack_pairs(x):
    n = x.shape[1] // 2
    lo = lax.bitcast_convert_type(x[:, :n].astype(BF16).astype(F32), U32)
    hi = lax.bitcast_convert_type(x[:, n:].astype(BF16).astype(F32), U32)
    return (lo >> 16) | hi


def _unpack_pairs(w):
    lo = lax.bitcast_convert_type(w << 16, F32)
    hi = lax.bitcast_convert_type(w & jnp.uint32(0xFFFF0000), F32)
    return jnp.concatenate([lo, hi], axis=-1)


def _slab_pitch(ns):
    return ns + 8 if ns % 16 == 0 else _round_up(ns, 8)


def _to_slabs(ref, x):
    rows, d = x.shape
    ns = d // LANES
    pitch = _slab_pitch(ns)
    ref[...] = jnp.zeros_like(ref)
    for s in range(ns):
        ref[pl.ds(s, rows, stride=pitch), :] = x[:, s * LANES:(s + 1) * LANES]


def _from_slabs(ref, ns):
    pitch = _slab_pitch(ns)
    rows = ref.shape[0] // pitch
    return jnp.concatenate([ref[pl.ds(s, rows, stride=pitch), :] for s in range(ns)], axis=-1)


def _norm_route_kernel(x_ref, g_ref, sc_ref, sh_ref, w_ref, b_ref, u_ref, o_ref):
    x = x_ref[...]
    y = x * lax.rsqrt(jnp.mean(x * x, axis=-1, keepdims=True) + NORM_EPS)
    y = y * g_ref[...]
    y = y * (1.0 + sc_ref[0]) + sh_ref[0]
    _to_slabs(u_ref, _pack_pairs(y))
    logits = jnp.dot(y.astype(BF16), w_ref[...], preferred_element_type=F32) + b_ref[...]
    col = lax.broadcasted_iota(jnp.int32, logits.shape, 1).astype(F32)
    ng = float(N_GROUPS)
    epg = float(EXPERTS_PER_GROUP)
    big = 1e9
    is_g = col < ng
    lg = jnp.where(is_g, logits, -jnp.inf)
    mg = jnp.max(lg, axis=-1, keepdims=True)
    gidx = jnp.min(jnp.where(lg == mg, col, big), axis=-1, keepdims=True)
    zg = jnp.sum(jnp.where(is_g, jnp.exp(lg - mg), 0.0), axis=-1, keepdims=True)
    g_w = 1.0 / zg
    lo = ng + gidx * epg
    in_grp = (col >= lo) & (col < lo + epg)
    le = jnp.where(in_grp, logits, -jnp.inf)
    m1 = jnp.max(le, axis=-1, keepdims=True)
    i1 = jnp.min(jnp.where(le == m1, col, big), axis=-1, keepdims=True)
    le2 = jnp.where(col == i1, -jnp.inf, le)
    m2 = jnp.max(le2, axis=-1, keepdims=True)
    i2 = jnp.min(jnp.where(le2 == m2, col, big), axis=-1, keepdims=True)
    e2 = jnp.exp(m2 - m1)
    w1 = 1.0 / (1.0 + e2)
    w2 = e2 / (1.0 + e2)
    out = jnp.where(col == 0.0, i1 - ng,
                    jnp.where(col == 1.0, i2 - ng,
                              jnp.where(col == 2.0, g_w * w1,
                                        jnp.where(col == 3.0, g_w * w2, 0.0))))
    o_ref[...] = out


def _norm_route(x2d, g, scale, shift, w_r, b_r, seq):
    t, d = x2d.shape
    tm = _pick_tile(seq, (256, 128))
    per_b = seq // tm
    ns = d // (2 * LANES)
    return pl.pallas_call(
        _norm_route_kernel,
        grid=(t // tm,),
        in_specs=[pl.BlockSpec((tm, d), lambda i: (i, 0)),
                  pl.BlockSpec((1, d), lambda i: (0, 0)),
                  pl.BlockSpec((1, 1, d), lambda i: (i // per_b, 0, 0)),
                  pl.BlockSpec((1, 1, d), lambda i: (i // per_b, 0, 0)),
                  pl.BlockSpec((d, LANES), lambda i: (0, 0)),
                  pl.BlockSpec((1, LANES), lambda i: (0, 0))],
        out_specs=[pl.BlockSpec((tm * _slab_pitch(ns), LANES), lambda i: (i, 0)),
                   pl.BlockSpec((tm, LANES), lambda i: (i, 0))],
        out_shape=[jax.ShapeDtypeStruct((t * _slab_pitch(ns), LANES), U32),
                   jax.ShapeDtypeStruct((t, LANES), F32)],
        compiler_params=_params(("arbitrary",)),
    )(x2d, g.reshape(1, d), scale[:, None, :], shift[:, None, :], w_r, b_r)


GATHER_GROUP = 8


def _gather_rows(idx_ref, n_rows, src_hbm, dst, sem, col0=0, unroll=False, slab_rows=0):
    pitch = _slab_pitch(slab_rows) if slab_rows else 1
    size = slab_rows if slab_rows else 1

    def issue(r):
        tok = idx_ref[0, 0, col0 + r]
        if slab_rows:
            src = src_hbm.at[pl.ds(pl.multiple_of(tok * pitch, 8), size)]
            dst_r = dst.at[pl.ds(pl.multiple_of(r * pitch, 8), size)]
        else:
            src, dst_r = src_hbm.at[pl.ds(tok, 1)], dst.at[pl.ds(r, 1)]
        pltpu.make_async_copy(src, dst_r, sem).start()

    if unroll is True:
        for r in range(n_rows):
            issue(r)
    elif isinstance(n_rows, int):
        def body(r, carry):
            issue(r)
            return carry
        lax.fori_loop(0, n_rows, body, 0, unroll=max(int(unroll), 1))
    else:
        def group(g, carry):
            for k in range(GATHER_GROUP):
                issue(g * GATHER_GROUP + k)
            return carry
        lax.fori_loop(0, n_rows // GATHER_GROUP, group, 0)


def _wait_rows(n_rows, src_hbm, dst, sem):
    if not isinstance(n_rows, int):
        n_rows = pl.multiple_of(n_rows, 8)
    pltpu.make_async_copy(src_hbm.at[pl.ds(0, n_rows)], dst.at[pl.ds(0, n_rows)], sem).wait()


def _expert_kernel(be_ref, nu_ref, nv_ref, tok_ref, tok1_ref, tok2_ref, u_hbm, wg_ref, wu_ref, wd_ref,
                   o_ref, xbuf_a, xbuf_b, xbuf_c, sem):
    i = pl.program_id(0)
    last = pl.num_programs(0) - 1
    n_used = nu_ref[0]
    rows, n_words = o_ref.shape
    ns = n_words // LANES

    def count(b):
        return jnp.where(b <= last, nv_ref[jnp.minimum(b, last)], 0)

    @pl.when(i == 0)
    def _():
        xbuf_a[...] = jnp.zeros_like(xbuf_a)
        xbuf_b[...] = jnp.zeros_like(xbuf_b)
        xbuf_c[...] = jnp.zeros_like(xbuf_c)
        _gather_rows(tok_ref, count(0), u_hbm, xbuf_a, sem.at[0], slab_rows=ns)
        _gather_rows(tok1_ref, count(1), u_hbm, xbuf_b, sem.at[1], slab_rows=ns)

    def step(cur, cur_sem, nxt, nxt_sem):
        @pl.when(i < n_used)
        def _():
            _wait_rows(nv_ref[i] * ns, u_hbm, cur, cur_sem)
            _gather_rows(tok2_ref, count(i + 2), u_hbm, nxt, nxt_sem, slab_rows=ns)
            xb = _unpack_pairs(_from_slabs(cur, ns)).astype(BF16)
            gate = jnp.dot(xb, wg_ref[0], preferred_element_type=F32)
            up = jnp.dot(xb, wu_ref[0], preferred_element_type=F32)
            h = (gate * jax.nn.sigmoid(gate) * up).astype(BF16)
            o_ref[...] = _pack_pairs(jnp.dot(h, wd_ref[0], preferred_element_type=F32))

        @pl.when(i >= n_used)
        def _():
            o_ref[...] = jnp.zeros_like(o_ref)

    @pl.when(i % 3 == 0)
    def _():
        step(xbuf_a, sem.at[0], xbuf_c, sem.at[2])

    @pl.when(i % 3 == 1)
    def _():
        step(xbuf_b, sem.at[1], xbuf_a, sem.at[0])

    @pl.when(i % 3 == 2)
    def _():
        step(xbuf_c, sem.at[2], xbuf_b, sem.at[1])


def _experts(block_expert, n_used, n_valid, row_tok, u2, wg, wu, wd):
    d = wg.shape[1]
    n_words = d // 2
    pitch = _slab_pitch(n_words // LANES)
    nb = block_expert.shape[0]
    rows = MOE_ROWS
    de = wg.shape[-1]
    tok3 = row_tok.reshape(nb, 1, rows)
    grid_spec = pltpu.PrefetchScalarGridSpec(
        num_scalar_prefetch=3,
        grid=(nb,),
        in_specs=[pl.BlockSpec((1, 1, rows), lambda i, be, nu, nv: (i, 0, 0), memory_space=pltpu.SMEM),
                  pl.BlockSpec((1, 1, rows), lambda i, be, nu, nv: (jnp.minimum(i + 1, nb - 1), 0, 0),
                               memory_space=pltpu.SMEM),
                  pl.BlockSpec((1, 1, rows), lambda i, be, nu, nv: (jnp.minimum(i + 2, nb - 1), 0, 0),
                               memory_space=pltpu.SMEM),
                  pl.BlockSpec(memory_space=pl.ANY),
                  pl.BlockSpec((1, d, de), lambda i, be, nu, nv: (be[i], 0, 0)),
                  pl.BlockSpec((1, d, de), lambda i, be, nu, nv: (be[i], 0, 0)),
                  pl.BlockSpec((1, de, d), lambda i, be, nu, nv: (be[i], 0, 0))],
        out_specs=pl.BlockSpec((rows, n_words), lambda i, be, nu, nv: (i, 0)),
        scratch_shapes=[pltpu.VMEM((rows * pitch, LANES), U32),
                        pltpu.VMEM((rows * pitch, LANES), U32),
                        pltpu.VMEM((rows * pitch, LANES), U32),
                        pltpu.SemaphoreType.DMA((3,))],
    )
    return pl.pallas_call(
        _expert_kernel,
        grid_spec=grid_spec,
        out_shape=jax.ShapeDtypeStruct((nb * rows, n_words), U32),
        compiler_params=_params(("arbitrary",), vmem_mib=58),
    )(block_expert, n_used, n_valid, tok3, tok3, tok3, u2, wg, wu, wd)


def _combine_kernel(pos_ref, posn_ref, ys_hbm, r_ref, x_ref, g_ref, ng_ref, o_ref, buf_a, buf_b, sem):
    i = pl.program_id(0)
    last = pl.num_programs(0) - 1
    tc = x_ref.shape[0]

    def gather(idx_ref, buf, s, unroll):
        _gather_rows(idx_ref, tc, ys_hbm, buf.at[0], s, 0, unroll)
        _gather_rows(idx_ref, tc, ys_hbm, buf.at[1], s, tc, unroll)

    def wait(buf, s):
        _wait_rows(tc, ys_hbm, buf.at[0], s)
        _wait_rows(tc, ys_hbm, buf.at[1], s)

    @pl.when(i == 0)
    def _():
        gather(pos_ref, buf_a, sem.at[0], False)

    def step(cur, cur_sem, nxt, nxt_sem):
        wait(cur, cur_sem)
        gather(posn_ref, nxt, nxt_sem, True)
        moe = _unpack_pairs(cur[0]) * r_ref[:, 2:3] + _unpack_pairs(cur[1]) * r_ref[:, 3:4]
        x = x_ref[...] + g_ref[0] * moe
        y = x * lax.rsqrt(jnp.mean(x * x, axis=-1, keepdims=True) + NORM_EPS)
        o_ref[...] = y * ng_ref[...]

        @pl.when(i == last)
        def _():
            wait(nxt, nxt_sem)

    @pl.when(i % 2 == 0)
    def _():
        step(buf_a, sem.at[0], buf_b, sem.at[1])

    @pl.when(i % 2 == 1)
    def _():
        step(buf_b, sem.at[1], buf_a, sem.at[0])


def _combine(pos, route, ys, x1, gate, norm_g, seq):
    t, d = x1.shape
    tc = _pick_tile(seq, (128,))
    n = t // tc
    per_b = seq // tc
    pos3 = pos.reshape(n, tc, 2).transpose(0, 2, 1).reshape(n, 1, 2 * tc)
    return pl.pallas_call(
        _combine_kernel,
        grid=(n,),
        in_specs=[pl.BlockSpec((1, 1, 2 * tc), lambda i: (i, 0, 0), memory_space=pltpu.SMEM),
                  pl.BlockSpec((1, 1, 2 * tc), lambda i: (jnp.minimum(i + 1, n - 1), 0, 0),
                               memory_space=pltpu.SMEM),
                  pl.BlockSpec(memory_space=pl.ANY),
                  pl.BlockSpec((tc, LANES), lambda i: (i, 0)),
                  pl.BlockSpec((tc, d), lambda i: (i, 0)),
                  pl.BlockSpec((1, 1, d), lambda i: (i // per_b, 0, 0)),
                  pl.BlockSpec((1, d), lambda i: (0, 0))],
        out_specs=pl.BlockSpec((tc, d), lambda i: (i, 0)),
        out_shape=jax.ShapeDtypeStruct((t, d), F32),
        scratch_shapes=[pltpu.VMEM((2, tc, d // 2), U32), pltpu.VMEM((2, tc, d // 2), U32),
                        pltpu.SemaphoreType.DMA((2,))],
        compiler_params=_params(("arbitrary",)),
    )(pos3, pos3, ys, route, x1, gate[:, None, :], norm_g.reshape(1, d))


def _dispatch(expert_id, n_tok):
    m = expert_id.shape[0]
    rows = MOE_ROWS
    nb = -(-m // rows) + N_EXPERTS
    onehot = (expert_id[:, None] == jnp.arange(N_EXPERTS, dtype=jnp.int32)[None, :]).astype(jnp.int32)
    csum = jnp.cumsum(onehot, axis=0)
    rank = jnp.take_along_axis(csum, expert_id[:, None], axis=1)[:, 0] - 1
    counts = csum[-1]
    padded = ((counts + rows - 1) // rows) * rows
    pends = jnp.cumsum(padded)
    pstarts = pends - padded
    dest = (pstarts[expert_id] + rank).astype(jnp.int32)
    tok = jnp.arange(m, dtype=jnp.int32) // (m // n_tok)
    row_tok = jnp.zeros((nb * rows,), jnp.int32).at[dest].set(tok)
    n_used = (pends[-1] // rows).astype(jnp.int32)
    all_blk = jnp.arange(nb, dtype=jnp.int32)
    blk = jnp.minimum(all_blk, n_used - 1)
    block_expert = jnp.minimum(jnp.searchsorted(pends, blk * rows, side='right'),
                               N_EXPERTS - 1).astype(jnp.int32)
    seg_end = (pstarts + counts)[block_expert]
    n_valid = jnp.where(all_blk < n_used, jnp.clip(seg_end - all_blk * rows, 0, rows), 0)
    n_valid = ((n_valid + GATHER_GROUP - 1) // GATHER_GROUP) * GATHER_GROUP
    return block_expert, n_used.reshape(1), n_valid.astype(jnp.int32), row_tok, dest


def _layer(x, c, w_cond, b_cond, norm1_g, w_in, rwkv_mu, rwkv_w0, rwkv_w_up, rwkv_a0, rwkv_a_up,
           rwkv_g_up, rwkv_k_k, rwkv_k_a, rwkv_r_k, rwkv_lnx_w, rwkv_lnx_b, attn_sinks, attn_out_g,
           w_out, norm2_g, router_group, router_group_bias, router_expert, router_expert_bias,
           expert_w_gate, expert_w_up, expert_w_down):
    b, s, d = x.shape
    t = b * s
    d_rwkv = rwkv_w0.shape[-1]
    d_attn = attn_out_g.shape[-1]
    dw, da, dg = rwkv_w_up.shape[0], rwkv_a_up.shape[0], rwkv_g_up.shape[0]
    n_lora = dw + da + dg
    lp = _round_up(n_lora, LANES)
    while (3 * d_rwkv) % lp:
        lp += LANES
    n_rwkv_cols = 3 * d_rwkv + n_lora

    c_pad = jnp.zeros((_round_up(b, 8), d), F32).at[:b].set(c)
    mod = _cond(c_pad, w_cond, b_cond.reshape(1, -1))[:b]
    sh1, sc1, g1, sh2, sc2, g2 = jnp.split(mod, 6, axis=-1)

    x2d = x.reshape(t, d)
    u1 = _norm_mod(x2d, norm1_g, sc1, sh1, s, BF16)

    zpad = jnp.zeros((d, lp - n_lora), F32)
    w_r = jnp.concatenate([w_in[:, :n_rwkv_cols], zpad], axis=1).astype(BF16)
    w_a = w_in[:, n_rwkv_cols:].astype(BF16)
    proj_r = _matmul(u1, w_r, BF16)
    proj_a = _matmul(u1, w_a, BF16)

    def lora_rows(wmat, off):
        return jnp.zeros((lp, d_rwkv), F32).at[off:off + wmat.shape[0]].set(wmat).astype(BF16)

    mu = jnp.concatenate([rwkv_mu, jnp.zeros((lp - n_lora,), F32)]).reshape(1, -1)
    row = lambda v: v.reshape(1, d_rwkv)
    n_exp = expert_w_gate.shape[0]
    halves = lambda w: w.reshape(2 * n_exp, w.shape[1] // 2, w.shape[2])
    w_out_bands = w_out.reshape(2 * n_exp, w_out.shape[0] // (2 * n_exp), w_out.shape[1])
    y_rwkv, (wg_b, wu_b, w_out_b) = _rwkv(
        proj_r.reshape(b, s, -1), mu, row(rwkv_w0), row(rwkv_a0), row(rwkv_k_k),
        row(rwkv_k_a), row(rwkv_r_k), row(rwkv_lnx_w), row(rwkv_lnx_b),
        lora_rows(rwkv_w_up, 0), lora_rows(rwkv_a_up, dw), lora_rows(rwkv_g_up, dw + da),
        d_rwkv=d_rwkv, lp=lp, dw=dw, da=da,
        side=(halves(expert_w_gate), halves(expert_w_up), w_out_bands))
    y_rwkv = y_rwkv.reshape(t, d_rwkv)
    wg_b = wg_b.reshape(expert_w_gate.shape)
    wu_b = wu_b.reshape(expert_w_up.shape)
    w_out_b = w_out_b.reshape(w_out.shape)

    quarters = lambda w: w.reshape(4 * n_exp, w.shape[1] // 4, w.shape[2])
    y_attn, wd_b = _attention(proj_a, attn_sinks, attn_out_g, batch=b, seq=s, d_attn=d_attn,
                              side=quarters(expert_w_down))
    wd_b = wd_b.reshape(expert_w_down.shape)

    x1 = _outproj(y_rwkv, y_attn, w_out_b, x2d, g1, s)

    n_r = N_GROUPS + N_EXPERTS
    w_router = jnp.concatenate([router_group, router_expert, jnp.zeros((d, LANES - n_r), F32)],
                               axis=1).astype(BF16)
    b_router = jnp.concatenate([router_group_bias, router_expert_bias,
                                jnp.zeros((LANES - n_r,), F32)]).reshape(1, LANES)
    u2, route = _norm_route(x1, norm2_g, sc2, sh2, w_router, b_router, s)
    expert_id = route[:, :2].astype(jnp.int32).reshape(-1)
    block_expert, n_used, n_valid, row_tok, dest = _dispatch(expert_id, t)
    ys = _experts(block_expert, n_used, n_valid, row_tok, u2, wg_b, wu_b, wd_b)
    return ys, dest.reshape(t, 2), route, x1, g2


def kernel(x, c, w_cond, b_cond, norm1_g, w_in, rwkv_mu, rwkv_w0, rwkv_w_up, rwkv_a0, rwkv_a_up, rwkv_g_up, rwkv_k_k, rwkv_k_a, rwkv_r_k, rwkv_lnx_w, rwkv_lnx_b, attn_sinks, attn_out_g, w_out, norm2_g, router_group, router_group_bias, router_expert, router_expert_bias, expert_w_gate, expert_w_up, expert_w_down, norm_f_g):
    b, s, d = x.shape
    depth = w_cond.shape[0]
    assert depth == 1, "the fused final norm assumes a single layer"
    l = 0
    ys, pos, route, x1, g2 = _layer(
        x, c, w_cond[l], b_cond[l], norm1_g[l], w_in[l], rwkv_mu[l], rwkv_w0[l], rwkv_w_up[l],
        rwkv_a0[l], rwkv_a_up[l], rwkv_g_up[l], rwkv_k_k[l], rwkv_k_a[l], rwkv_r_k[l],
        rwkv_lnx_w[l], rwkv_lnx_b[l], attn_sinks[l], attn_out_g[l], w_out[l], norm2_g[l],
        router_group[l], router_group_bias[l], router_expert[l], router_expert_bias[l],
        expert_w_gate[l], expert_w_up[l], expert_w_down[l])
    out = _combine(pos, route, ys, x1, g2, norm_f_g, s)
    return out.reshape(b, s, d)
```
